```python
import math
import jax, jax.numpy as jnp
from jax import lax
import numpy as np

D_MODEL = 4096
BATCH = 1
SEQ = 8192
DEPTH = 1
DEC_BATCH = 128
DEC_SEQ = 4
PAST_LEN = 2048
PAGE_SIZE = 128

N_HEADS = 16
N_KV_HEADS = 4
HEAD_DIM = 128
ROPE_THETA = 500000.0
N_IDX_HEADS = 16
IDX_DIM = 128
IDX_TOPK = 256
Q_BLOCK = 128
N_DELTA_HEADS = 16
DELTA_DK = 128
DELTA_DV = 128
CONV_WIDTH = 4
DELTA_CHUNK = 64
N_EXPERTS = 64
TOP_K = 8
N_GROUPS = 8
TOPK_GROUPS = 4
EXPERT_DIM = 1024
SHARED_DIM = 1024
ROUTED_SCALE = 2.5
EXPERT_BLOCK = 256
ALPHA = (2 * DEPTH) ** 0.25
BETA_INIT = (8 * DEPTH) ** -0.25
LN_EPS = 1e-5
ATTN_WIDTH = N_HEADS * HEAD_DIM
KV_WIDTH = N_KV_HEADS * HEAD_DIM
IDXQ_WIDTH = N_IDX_HEADS * IDX_DIM
DELTA_QK_WIDTH = N_DELTA_HEADS * DELTA_DK
DELTA_V_WIDTH = N_DELTA_HEADS * DELTA_DV
CONV_CH = 2 * DELTA_QK_WIDTH + DELTA_V_WIDTH
IN_SIZES = (ATTN_WIDTH, KV_WIDTH, KV_WIDTH, IDXQ_WIDTH, IDX_DIM, N_IDX_HEADS, CONV_CH,
            N_DELTA_HEADS, N_DELTA_HEADS, DELTA_V_WIDTH, 2 * D_MODEL)
IN_SPLITS = tuple(int(s) for s in np.cumsum(IN_SIZES)[:-1])
IN_WIDTH = int(sum(IN_SIZES))

kernel_name = 'dsa_deltanet_gated_merge_moe_step'


def layer_norm(x, g, b):
    xf = x.astype(jnp.float32)
    xc = xf - jnp.mean(xf, -1, keepdims=True)
    var = jnp.mean(xc * xc, -1, keepdims=True)
    return (xc * lax.rsqrt(var + LN_EPS) * g.astype(jnp.float32) + b.astype(jnp.float32)).astype(x.dtype)


def l2norm(x):
    xf = x.astype(jnp.float32)
    return xf * lax.rsqrt(jnp.sum(xf * xf, -1, keepdims=True) + 1e-6)


def partial_rope(x, pos):
    rot = x.shape[-1] // 4
    half = rot // 2
    inv_freq = ROPE_THETA ** (-jnp.arange(half, dtype=jnp.float32) / half)
    ang = pos.astype(jnp.float32)[:, None] * inv_freq[None, :]
    cos = jnp.cos(ang)[:, None, :].astype(x.dtype)
    sin = jnp.sin(ang)[:, None, :].astype(x.dtype)
    x1, x2, rest = x[..., :half], x[..., half:rot], x[..., rot:]
    return jnp.concatenate([x1 * cos - x2 * sin, x2 * cos + x1 * sin, rest], axis=-1)


def project_inputs(x, w_in, pos):
    b, t, _ = x.shape
    q, k, v, iq, ik, iw, qkv, b_d, a_d, z, gates = jnp.split(x @ w_in, IN_SPLITS, axis=-1)
    q = partial_rope(q.reshape(b, t, N_HEADS, HEAD_DIM), pos)
    k = partial_rope(k.reshape(b, t, N_KV_HEADS, HEAD_DIM), pos)
    v = v.reshape(b, t, N_KV_HEADS, HEAD_DIM)
    iq = partial_rope(iq.reshape(b, t, N_IDX_HEADS, IDX_DIM), pos)
    ik = partial_rope(ik[:, :, None, :], pos)[:, :, 0, :]
    iw = iw * (N_IDX_HEADS ** -0.5 * IDX_DIM ** -0.5)
    return q, k, v, iq, ik, iw, qkv, b_d, a_d, z, gates


def indexer_scores(iq, iw, ik):
    s = jnp.einsum('bthd,bsd->bths', iq, ik, preferred_element_type=jnp.float32)
    return jnp.einsum('bths,bth->bts', jax.nn.relu(s), iw.astype(jnp.float32))


def prompt_sparse_attention(q, k, v, iq, iw, ik):
    b, s = q.shape[:2]
    n_sel = min(IDX_TOPK, s // 4)
    nb = s // Q_BLOCK
    grp = N_HEADS // N_KV_HEADS
    kpos = jnp.arange(s)
    bi = jnp.arange(b)[:, None, None]

    def to_blocks(a):
        return jnp.swapaxes(a.reshape((b, nb, Q_BLOCK) + a.shape[2:]), 0, 1)

    def block(args):
        qb, iqb, iwb, qpos = args
        sc = indexer_scores(iqb, iwb, ik)
        sc = jnp.where((kpos[None, :] <= qpos[:, None])[None], sc, -jnp.inf)
        _, idx = lax.top_k(sc, n_sel)
        valid = idx <= qpos[None, :, None]
        k_sel = k[bi, idx]
        v_sel = v[bi, idx]
        qg = qb.reshape(b, Q_BLOCK, N_KV_HEADS, grp, HEAD_DIM)
        sco = jnp.einsum('btngd,btjnd->btngj', qg, k_sel, preferred_element_type=jnp.float32) * HEAD_DIM ** -0.5
        sco = jnp.where(valid[:, :, None, None, :], sco, -jnp.inf)
        p = jax.nn.softmax(sco, axis=-1).astype(v.dtype)
        o = jnp.einsum('btngj,btjnd->btngd', p, v_sel)
        return o.reshape(b, Q_BLOCK, N_HEADS * HEAD_DIM)

    out = lax.map(block, (to_blocks(q), to_blocks(iq), to_blocks(iw), kpos.reshape(nb, Q_BLOCK)))
    return jnp.swapaxes(out, 0, 1).reshape(b, s, N_HEADS * HEAD_DIM)


def sample_sparse_attention(q, k_new, v_new, iq, iw, ik_new, cache_k, cache_v, cache_idx_k, page_table):
    db, t = q.shape[:2]
    page = cache_k.shape[1]
    past = page_table.shape[1] * page
    n_keys = past + t
    n_sel = min(IDX_TOPK, n_keys // 4)
    ik_past = cache_idx_k[page_table].reshape(db, past, IDX_DIM)
    sc = indexer_scores(iq, iw, jnp.concatenate([ik_past, ik_new], axis=1))
    qpos = past + jnp.arange(t)
    kpos = jnp.arange(n_keys)
    sc = jnp.where((kpos[None, :] <= qpos[:, None])[None], sc, -jnp.inf)
    _, idx = lax.top_k(sc, n_sel)
    is_past = idx < past
    pidx = jnp.where(is_past, idx, 0)
    bi = jnp.arange(db)[:, None, None]
    phys = page_table[bi, pidx // page]
    off = pidx % page
    k_p = cache_k[phys, off]
    v_p = cache_v[phys, off]
    new_sel = jnp.any(idx[..., None] == qpos[None, None, None, :], axis=2)
    new_sel = new_sel & jnp.tril(jnp.ones((t, t), bool))[None]
    grp = N_HEADS // N_KV_HEADS
    qg = q.reshape(db, t, N_KV_HEADS, grp, HEAD_DIM)
    scale = HEAD_DIM ** -0.5
    s_p = jnp.einsum('btngd,btjnd->btngj', qg, k_p, preferred_element_type=jnp.float32) * scale
    s_n = jnp.einsum('btngd,bsnd->btngs', qg, k_new, preferred_element_type=jnp.float32) * scale
    s_p = jnp.where(is_past[:, :, None, None, :], s_p, -jnp.inf)
    s_n = jnp.where(new_sel[:, :, None, None, :], s_n, -jnp.inf)
    p = jax.nn.softmax(jnp.concatenate([s_p, s_n], axis=-1), axis=-1).astype(v_new.dtype)
    o = (jnp.einsum('btngj,btjnd->btngd', p[..., :n_sel], v_p)
         + jnp.einsum('btngs,bsnd->btngd', p[..., n_sel:], v_new))
    return o.reshape(db, t, N_HEADS * HEAD_DIM)


def causal_conv(xpad, w_conv):
    c = xpad.shape[-1]
    return lax.conv_general_dilated(xpad, w_conv[:, None, :].astype(xpad.dtype), window_strides=(1,),
                                    padding='VALID', dimension_numbers=('NWC', 'WIO', 'NWC'),
                                    feature_group_count=c)


def chunk_gated_delta(q, k, v, beta, g, s0):
    b, t, h, dk = q.shape
    dv = v.shape[-1]
    c = min(DELTA_CHUNK, t)
    n = -(-t // c)
    pad = n * c - t

    def chunks(a):
        a = jnp.pad(a.astype(jnp.float32), [(0, 0), (0, pad)] + [(0, 0)] * (a.ndim - 2))
        a = a.reshape((b, n, c) + a.shape[2:])
        return jnp.moveaxis(a, (1, 3), (0, 2))

    qc, kc, vc, bc, gc = [chunks(a) for a in (q, k, v, beta, g)]
    gc = jnp.cumsum(gc, axis=-1)
    tri = jnp.tril(jnp.ones((c, c), bool))
    strict = jnp.tril(jnp.ones((c, c), bool), -1)
    decay = jnp.exp(jnp.where(tri, gc[..., :, None] - gc[..., None, :], -jnp.inf))
    kb = kc * bc[..., None]
    a_low = jnp.where(strict, jnp.einsum('nbhid,nbhjd->nbhij', kb, kc) * decay, 0.0)
    rhs = jnp.concatenate([vc * bc[..., None], kb * jnp.exp(gc)[..., None]], axis=-1)
    sol = lax.linalg.triangular_solve(a_low + jnp.eye(c, dtype=jnp.float32), rhs, left_side=True,
                                      lower=True, unit_diagonal=True)
    u, w = sol[..., :dv], sol[..., dv:]
    qk = jnp.where(tri, jnp.einsum('nbhid,nbhjd->nbhij', qc, kc) * decay, 0.0)

    def step(state, xs):
        q_i, k_i, u_i, w_i, g_i, qk_i = xs
        v_corr = u_i - jnp.einsum('bhck,bhkv->bhcv', w_i, state)
        o_i = (jnp.einsum('bhck,bhkv->bhcv', q_i * jnp.exp(g_i)[..., None], state)
               + jnp.einsum('bhij,bhjv->bhiv', qk_i, v_corr))
        g_last = g_i[..., -1:]
        state = (state * jnp.exp(g_last)[..., None]
                 + jnp.einsum('bhck,bhcv->bhkv', k_i * jnp.exp(g_last - g_i)[..., None], v_corr))
        return state, o_i

    s_final, o = lax.scan(step, s0.astype(jnp.float32), (qc, kc, u, w, gc, qk))
    o = jnp.moveaxis(o, (0, 2), (1, 3)).reshape(b, n * c, h, dv)[:, :t]
    return o, s_final


def gated_delta_branch(qkv, b_d, a_d, z, conv_buf, s0, w_conv, a_log, dt_bias, norm_g):
    b, t, _ = qkv.shape
    xpad = jnp.concatenate([conv_buf.astype(qkv.dtype), qkv], axis=1)
    new_buf = xpad[:, -(CONV_WIDTH - 1):]
    hc = jax.nn.silu(causal_conv(xpad, w_conv))
    q, k, v = jnp.split(hc, (DELTA_QK_WIDTH, 2 * DELTA_QK_WIDTH), axis=-1)
    q = l2norm(q.reshape(b, t, N_DELTA_HEADS, DELTA_DK)) * DELTA_DK ** -0.5
    k = l2norm(k.reshape(b, t, N_DELTA_HEADS, DELTA_DK))
    v = v.reshape(b, t, N_DELTA_HEADS, DELTA_DV)
    beta = jax.nn.sigmoid(b_d.astype(jnp.float32))
    g = -jnp.exp(a_log.astype(jnp.float32)) * jax.nn.softplus(a_d.astype(jnp.float32) + dt_bias.astype(jnp.float32))
    o, s_new = chunk_gated_delta(q, k, v, beta, g, s0)
    o = o * lax.rsqrt(jnp.mean(o * o, -1, keepdims=True) + 1e-6) * norm_g.astype(jnp.float32)
    o = o * jax.nn.silu(z.reshape(b, t, N_DELTA_HEADS, DELTA_DV).astype(jnp.float32))
    return o.reshape(b, t, DELTA_V_WIDTH).astype(qkv.dtype), s_new.astype(s0.dtype), new_buf


def routed_experts(xt, expert_idx, gate, w_gate, w_up, w_down):
    n, d = xt.shape
    k = expert_idx.shape[1]
    n_exp = w_gate.shape[0]
    blk = EXPERT_BLOCK
    flat_e = expert_idx.reshape(-1)
    order = jnp.argsort(flat_e)
    sorted_e = flat_e[order]
    counts = jnp.bincount(flat_e, length=n_exp)
    padded = (counts + blk - 1) // blk * blk
    pad_end = jnp.cumsum(padded)
    pad_start = pad_end - padded
    start = jnp.cumsum(counts) - counts
    dest = pad_start[sorted_e] + jnp.arange(n * k) - start[sorted_e]
    n_blocks = -(-(n * k) // blk) + n_exp
    rows = n_blocks * blk
    row_token = jnp.full((rows,), n, jnp.int32).at[dest].set((order // k).astype(jnp.int32))
    row_gate = jnp.zeros((rows,), xt.dtype).at[dest].set(gate.reshape(-1)[order])
    block_expert = jnp.minimum(jnp.searchsorted(pad_end, jnp.arange(n_blocks) * blk, side='right'), n_exp - 1)
    x_rows = jnp.concatenate([xt, jnp.zeros((1, d), xt.dtype)], axis=0)[row_token].reshape(n_blocks, blk, d)

    def expert_block(args):
        xb, e = args
        return (jax.nn.silu(xb @ w_gate[e]) * (xb @ w_up[e])) @ w_down[e]

    y_rows = lax.map(expert_block, (x_rows, block_expert)).reshape(rows, d)
    out = jnp.zeros((n + 1, d), xt.dtype).at[row_token].add(y_rows * row_gate[:, None])
    return out[:n]


def moe(x, w_router, router_bias, w_exp_gate, w_exp_up, w_exp_down, w_sh_gate, w_sh_up, w_sh_down):
    shape = x.shape
    xt = x.reshape(-1, shape[-1])
    n = xt.shape[0]
    scores = jax.nn.sigmoid(jnp.matmul(xt, w_router, preferred_element_type=jnp.float32))
    biased = scores + router_bias.astype(jnp.float32)
    per_group = N_EXPERTS // N_GROUPS
    group_score = lax.top_k(biased.reshape(n, N_GROUPS, per_group), 2)[0].sum(-1)
    _, top_groups = lax.top_k(group_score, TOPK_GROUPS)
    group_keep = jnp.any(top_groups[:, :, None] == jnp.arange(N_GROUPS)[None, None, :], axis=1)
    biased = jnp.where(jnp.repeat(group_keep, per_group, axis=1), biased, -jnp.inf)
    _, expert_idx = lax.top_k(biased, TOP_K)
    gate = jnp.take_along_axis(scores, expert_idx, axis=1)
    gate = gate / jnp.sum(gate, -1, keepdims=True) * ROUTED_SCALE
    routed = routed_experts(xt, expert_idx, gate.astype(x.dtype), w_exp_gate, w_exp_up, w_exp_down)
    shared = (jax.nn.silu(xt @ w_sh_gate) * (xt @ w_sh_up)) @ w_sh_down
    return (routed + shared).reshape(shape)


def setup_inputs(seed: int = 0) -> dict:
    key = jax.random.key(seed)
    ks = jax.random.split(key, 32)
    f32 = jnp.float32
    n_pages = PAST_LEN // PAGE_SIZE
    n_used = DEC_BATCH * n_pages
    n_pool = n_used + max(1, n_used // 4)

    def nrm(k, shape, s):
        return jax.random.normal(k, shape, f32) * s

    page_table = jax.random.permutation(ks[5], n_pool)[:n_used].reshape(DEC_BATCH, n_pages).astype(jnp.int32)
    dt = jnp.exp(jax.random.uniform(ks[11], (DEPTH, N_DELTA_HEADS), f32, math.log(1e-3), math.log(1e-1)))
    return {
        'x_prompt': nrm(ks[0], (BATCH, SEQ, D_MODEL), 1.0),
        'x_sample': nrm(ks[1], (DEC_BATCH, DEC_SEQ, D_MODEL), 1.0),
        'cache_k': nrm(ks[2], (DEPTH, n_pool, PAGE_SIZE, N_KV_HEADS, HEAD_DIM), 1.0),
        'cache_v': nrm(ks[3], (DEPTH, n_pool, PAGE_SIZE, N_KV_HEADS, HEAD_DIM), 1.0),
        'cache_idx_k': nrm(ks[4], (DEPTH, n_pool, PAGE_SIZE, IDX_DIM), 1.0),
        'page_table': page_table,
        'state_delta': nrm(ks[6], (DEPTH, DEC_BATCH, N_DELTA_HEADS, DELTA_DK, DELTA_DV), 0.1),
        'state_conv': nrm(ks[7], (DEPTH, DEC_BATCH, CONV_WIDTH - 1, CONV_CH), 1.0),
        'w_in': nrm(ks[8], (DEPTH, D_MODEL, IN_WIDTH), D_MODEL ** -0.5),
        'w_conv': nrm(ks[9], (DEPTH, CONV_WIDTH, CONV_CH), CONV_WIDTH ** -0.5),
        'a_log': jnp.log(jax.random.uniform(ks[10], (DEPTH, N_DELTA_HEADS), f32, 1.0, 16.0)),
        'dt_bias': dt + jnp.log(-jnp.expm1(-dt)),
        'delta_norm_g': 1.0 + nrm(ks[12], (DEPTH, DELTA_DV), 0.01),
        'w_branch_attn': nrm(ks[13], (DEPTH, ATTN_WIDTH, D_MODEL), ATTN_WIDTH ** -0.5),
        'w_branch_delta': nrm(ks[14], (DEPTH, DELTA_V_WIDTH, D_MODEL), DELTA_V_WIDTH ** -0.5),
        'w_out': nrm(ks[15], (DEPTH, D_MODEL, D_MODEL), D_MODEL ** -0.5 * BETA_INIT),
        'ln1_g': 1.0 + nrm(ks[16], (DEPTH, D_MODEL), 0.01),
        'ln1_b': nrm(ks[17], (DEPTH, D_MODEL), 0.01),
        'w_router': nrm(ks[18], (DEPTH, D_MODEL, N_EXPERTS), D_MODEL ** -0.5),
        'router_bias': nrm(ks[19], (DEPTH, N_EXPERTS), 0.01),
        'w_exp_gate': nrm(ks[20], (DEPTH, N_EXPERTS, D_MODEL, EXPERT_DIM), D_MODEL ** -0.5),
        'w_exp_up': nrm(ks[21], (DEPTH, N_EXPERTS, D_MODEL, EXPERT_DIM), D_MODEL ** -0.5),
        'w_exp_down': nrm(ks[22], (DEPTH, N_EXPERTS, EXPERT_DIM, D_MODEL), EXPERT_DIM ** -0.5 * BETA_INIT),
        'w_sh_gate': nrm(ks[23], (DEPTH, D_MODEL, SHARED_DIM), D_MODEL ** -0.5),
        'w_sh_up': nrm(ks[24], (DEPTH, D_MODEL, SHARED_DIM), D_MODEL ** -0.5),
        'w_sh_down': nrm(ks[25], (DEPTH, SHARED_DIM, D_MODEL), SHARED_DIM ** -0.5 * BETA_INIT),
        'ln2_g': 1.0 + nrm(ks[26], (DEPTH, D_MODEL), 0.01),
        'ln2_b': nrm(ks[27], (DEPTH, D_MODEL), 0.01),
    }


def reference(x_prompt, x_sample, cache_k, cache_v, cache_idx_k, page_table, state_delta, state_conv,
              w_in, w_conv, a_log, dt_bias, delta_norm_g, w_branch_attn, w_branch_delta, w_out,
              ln1_g, ln1_b, w_router, router_bias, w_exp_gate, w_exp_up, w_exp_down,
              w_sh_gate, w_sh_up, w_sh_down, ln2_g, ln2_b):

    def run_layer(x, pos, attend, conv_buf, s0, lw):
        (l_in, l_conv, l_alog, l_dtb, l_ng, l_ba, l_bd, l_out, l_g1, l_b1,
         l_r, l_rb, l_eg, l_eu, l_ed, l_sg, l_su, l_sd, l_g2, l_b2) = lw
        q, k, v, iq, ik, iw, qkv, b_d, a_d, z, gates = project_inputs(x, l_in, pos)
        a_out = attend(q, k, v, iq, iw, ik)
        d_out, s_new, conv_new = gated_delta_branch(qkv, b_d, a_d, z, conv_buf, s0, l_conv, l_alog, l_dtb, l_ng)
        g_a, g_d = jnp.split(jax.nn.sigmoid(gates.astype(jnp.float32)).astype(x.dtype), 2, axis=-1)
        merged = g_a * (a_out @ l_ba) + g_d * (d_out @ l_bd)
        h = layer_norm(ALPHA * x + merged @ l_out, l_g1, l_b1)
        y = layer_norm(ALPHA * h + moe(h, l_r, l_rb, l_eg, l_eu, l_ed, l_sg, l_su, l_sd), l_g2, l_b2)
        return y, k, v, ik, s_new, conv_new

    b, s = x_prompt.shape[:2]
    db, t = x_sample.shape[:2]
    past = page_table.shape[1] * cache_k.shape[2]
    pos_prompt = jnp.arange(s)
    pos_sample = past + jnp.arange(t)
    yp, ys = x_prompt, x_sample
    kp_l, vp_l, ikp_l, sdp_l, scp_l = [], [], [], [], []
    ks_l, vs_l, iks_l, sds_l, scs_l = [], [], [], [], []
    for l in range(DEPTH):
        lw = [a[l] for a in (w_in, w_conv, a_log, dt_bias, delta_norm_g, w_branch_attn, w_branch_delta, w_out,
                             ln1_g, ln1_b, w_router, router_bias, w_exp_gate, w_exp_up, w_exp_down,
                             w_sh_gate, w_sh_up, w_sh_down, ln2_g, ln2_b)]
        ck, cv, cik = cache_k[l], cache_v[l], cache_idx_k[l]

        def sample_attend(q, k, v, iq, iw, ik, ck=ck, cv=cv, cik=cik):
            return sample_sparse_attention(q, k, v, iq, iw, ik, ck, cv, cik, page_table)

        yp, kp, vp, ikp, sdp, scp = run_layer(
            yp, pos_prompt, prompt_sparse_attention,
            jnp.zeros((b, CONV_WIDTH - 1, CONV_CH), x_prompt.dtype),
            jnp.zeros((b, N_DELTA_HEADS, DELTA_DK, DELTA_DV), state_delta.dtype), lw)
        ys, ksn, vsn, iksn, sds, scs = run_layer(ys, pos_sample, sample_attend, state_conv[l], state_delta[l], lw)
        kp_l.append(kp); vp_l.append(vp); ikp_l.append(ikp); sdp_l.append(sdp); scp_l.append(scp)
        ks_l.append(ksn); vs_l.append(vsn); iks_l.append(iksn); sds_l.append(sds); scs_l.append(scs)

    return (yp, ys,
            jnp.stack(kp_l), jnp.stack(vp_l), jnp.stack(ikp_l), jnp.stack(sdp_l), jnp.stack(scp_l),
            jnp.stack(ks_l), jnp.stack(vs_l), jnp.stack(iks_l), jnp.stack(sds_l), jnp.stack(scs_l))
```

```python
import functools
import math

import jax
import jax.numpy as jnp
import numpy as np
from jax import lax
from jax.experimental import pallas as pl
from jax.experimental.pallas import tpu as pltpu

D_MODEL = 4096
DEPTH = 1
N_HEADS = 16
N_KV_HEADS = 4
HEAD_DIM = 128
ROPE_THETA = 500000.0
N_IDX_HEADS = 16
IDX_DIM = 128
IDX_TOPK = 256
Q_BLOCK = 128
N_DELTA_HEADS = 16
DELTA_DK = 128
DELTA_DV = 128
CONV_WIDTH = 4
DELTA_CHUNK = 64
N_EXPERTS = 64
TOP_K = 8
N_GROUPS = 8
TOPK_GROUPS = 4
EXPERT_DIM = 1024
ROUTED_SCALE = 2.5
EXPERT_BLOCK = 256
ALPHA = (2 * DEPTH) ** 0.25
LN_EPS = 1e-5
ATTN_WIDTH = N_HEADS * HEAD_DIM
KV_WIDTH = N_KV_HEADS * HEAD_DIM
IDXQ_WIDTH = N_IDX_HEADS * IDX_DIM
DELTA_QK_WIDTH = N_DELTA_HEADS * DELTA_DK
DELTA_V_WIDTH = N_DELTA_HEADS * DELTA_DV
CONV_CH = 2 * DELTA_QK_WIDTH + DELTA_V_WIDTH
IN_SIZES = (ATTN_WIDTH, KV_WIDTH, KV_WIDTH, IDXQ_WIDTH, IDX_DIM, N_IDX_HEADS, CONV_CH,
            N_DELTA_HEADS, N_DELTA_HEADS, DELTA_V_WIDTH, 2 * D_MODEL)
IN_SPLITS = tuple(int(s) for s in np.cumsum(IN_SIZES)[:-1])

V7X_VMEM_LIMIT = 56 * 1024 * 1024
BF16 = jnp.bfloat16
F32 = jnp.float32


def _mm_kernel(x_ref, w_ref, o_ref, wbf_ref):
    @pl.when(pl.program_id(1) == 0)
    def _():
        wbf_ref[...] = w_ref[...].astype(BF16)

    o_ref[...] = jnp.dot(x_ref[...], wbf_ref[...], preferred_element_type=F32).astype(o_ref.dtype)


def matmul(x, w, out_dtype=F32, tn=512):
    m, k = x.shape
    n = w.shape[1]
    tm = min(m, 1024)
    tn = min(tn, n)
    assert m % tm == 0
    return pl.pallas_call(
        _mm_kernel,
        grid=(pl.cdiv(n, tn), m // tm),
        in_specs=[pl.BlockSpec((tm, k), lambda j, i: (i, 0)),
                  pl.BlockSpec((k, tn), lambda j, i: (0, j))],
        out_specs=pl.BlockSpec((tm, tn), lambda j, i: (i, j)),
        out_shape=jax.ShapeDtypeStruct((m, n), out_dtype),
        scratch_shapes=[pltpu.VMEM((k, tn), BF16)],
        compiler_params=pltpu.CompilerParams(dimension_semantics=("arbitrary", "arbitrary"),
                                             vmem_limit_bytes=V7X_VMEM_LIMIT),
        name="dense_matmul",
    )(x, w)


def _expert_changed(be_ref, i):
    prev = be_ref[jnp.maximum(i - 1, 0)]
    return jnp.logical_or(i == 0, be_ref[i] != prev)


def _moe_up_kernel(be_ref, x_ref, wg_ref, wu_ref, h_ref, wg_bf, wu_bf):
    i = pl.program_id(1)

    @pl.when(_expert_changed(be_ref, i))
    def _():
        wg_bf[...] = wg_ref[...].astype(BF16)
        wu_bf[...] = wu_ref[...].astype(BF16)

    x = x_ref[...]
    g = jnp.dot(x, wg_bf[...], preferred_element_type=F32)
    u = jnp.dot(x, wu_bf[...], preferred_element_type=F32)
    h_ref[...] = (g * jax.nn.sigmoid(g) * u).astype(h_ref.dtype)


def _moe_down_kernel(be_ref, h_ref, wd_ref, y_ref, wd_bf):
    i = pl.program_id(1)

    @pl.when(_expert_changed(be_ref, i))
    def _():
        wd_bf[...] = wd_ref[...].astype(BF16)

    y_ref[...] = jnp.dot(h_ref[...], wd_bf[...], preferred_element_type=F32).astype(y_ref.dtype)


def grouped_swiglu_up(x_rows, block_expert, w_gate, w_up, blk, tf=256):
    rows, d = x_rows.shape
    f = w_gate.shape[2]
    n_blocks = rows // blk
    grid_spec = pltpu.PrefetchScalarGridSpec(
        num_scalar_prefetch=1,
        grid=(f // tf, n_blocks),
        in_specs=[pl.BlockSpec((blk, d), lambda j, i, be: (i, 0)),
                  pl.BlockSpec((None, d, tf), lambda j, i, be: (be[i], 0, j)),
                  pl.BlockSpec((None, d, tf), lambda j, i, be: (be[i], 0, j))],
        out_specs=pl.BlockSpec((blk, tf), lambda j, i, be: (i, j)),
        scratch_shapes=[pltpu.VMEM((d, tf), BF16), pltpu.VMEM((d, tf), BF16)],
    )
    return pl.pallas_call(
        _moe_up_kernel, grid_spec=grid_spec,
        out_shape=jax.ShapeDtypeStruct((rows, f), BF16),
        compiler_params=pltpu.CompilerParams(dimension_semantics=("arbitrary", "arbitrary"),
                                             vmem_limit_bytes=V7X_VMEM_LIMIT),
        name="moe_up",
    )(block_expert, x_rows, w_gate, w_up)


def grouped_down(h_rows, block_expert, w_down, blk, tn=1024):
    rows, f = h_rows.shape
    d = w_down.shape[2]
    n_blocks = rows // blk
    grid_spec = pltpu.PrefetchScalarGridSpec(
        num_scalar_prefetch=1,
        grid=(d // tn, n_blocks),
        in_specs=[pl.BlockSpec((blk, f), lambda j, i, be: (i, 0)),
                  pl.BlockSpec((None, f, tn), lambda j, i, be: (be[i], 0, j))],
        out_specs=pl.BlockSpec((blk, tn), lambda j, i, be: (i, j)),
        scratch_shapes=[pltpu.VMEM((f, tn), BF16)],
    )
    return pl.pallas_call(
        _moe_down_kernel, grid_spec=grid_spec,
        out_shape=jax.ShapeDtypeStruct((rows, d), F32),
        compiler_params=pltpu.CompilerParams(dimension_semantics=("arbitrary", "arbitrary"),
                                             vmem_limit_bytes=V7X_VMEM_LIMIT),
        name="moe_down",
    )(block_expert, h_rows, w_down)


def layer_norm(x, g, b):
    xc = x - jnp.mean(x, -1, keepdims=True)
    var = jnp.mean(xc * xc, -1, keepdims=True)
    return xc * lax.rsqrt(var + LN_EPS) * g + b


def l2norm(x):
    return x * lax.rsqrt(jnp.sum(x * x, -1, keepdims=True) + 1e-6)


def partial_rope(x, pos):
    rot = x.shape[-1] // 4
    half = rot // 2
    inv_freq = ROPE_THETA ** (-jnp.arange(half, dtype=F32) / half)
    ang = pos.astype(F32)[:, None] * inv_freq[None, :]
    cos = jnp.cos(ang)[:, None, :]
    sin = jnp.sin(ang)[:, None, :]
    x1, x2, rest = x[..., :half], x[..., half:rot], x[..., rot:]
    return jnp.concatenate([x1 * cos - x2 * sin, x2 * cos + x1 * sin, rest], axis=-1)


def project_inputs(x, w_in, pos):
    b, t, d = x.shape
    p = matmul(x.reshape(b * t, d).astype(BF16), w_in).reshape(b, t, -1)
    q, k, v, iq, ik, iw, qkv, b_d, a_d, z, gates = jnp.split(p, IN_SPLITS, axis=-1)
    q = partial_rope(q.reshape(b, t, N_HEADS, HEAD_DIM), pos)
    k = partial_rope(k.reshape(b, t, N_KV_HEADS, HEAD_DIM), pos)
    v = v.reshape(b, t, N_KV_HEADS, HEAD_DIM)
    iq = partial_rope(iq.reshape(b, t, N_IDX_HEADS, IDX_DIM), pos)
    ik = partial_rope(ik[:, :, None, :], pos)[:, :, 0, :]
    iw = iw * (N_IDX_HEADS ** -0.5 * IDX_DIM ** -0.5)
    return q, k, v, iq, ik, iw, qkv, b_d, a_d, z, gates


def indexer_scores(iq, iw, ik):
    s = jnp.einsum('bthd,bsd->bths', iq, ik, preferred_element_type=F32)
    return jnp.einsum('bths,bth->bts', jax.nn.relu(s), iw)


def prompt_sparse_attention(q, k, v, iq, iw, ik):
    b, s = q.shape[:2]
    n_sel = min(IDX_TOPK, s // 4)
    nb = s // Q_BLOCK
    grp = N_HEADS // N_KV_HEADS
    kpos = jnp.arange(s)
    bi = jnp.arange(b)[:, None, None]

    def to_blocks(a):
        return jnp.swapaxes(a.reshape((b, nb, Q_BLOCK) + a.shape[2:]), 0, 1)

    def block(args):
        qb, iqb, iwb, qpos = args
        sc = indexer_scores(iqb, iwb, ik)
        sc = jnp.where((kpos[None, :] <= qpos[:, None])[None], sc, -jnp.inf)
        _, idx = lax.top_k(sc, n_sel)
        valid = idx <= qpos[None, :, None]
        k_sel = k[bi, idx]
        v_sel = v[bi, idx]
        qg = qb.reshape(b, Q_BLOCK, N_KV_HEADS, grp, HEAD_DIM)
        sco = jnp.einsum('btngd,btjnd->btngj', qg, k_sel, preferred_element_type=F32) * HEAD_DIM ** -0.5
        sco = jnp.where(valid[:, :, None, None, :], sco, -jnp.inf)
        p = jax.nn.softmax(sco, axis=-1)
        o = jnp.einsum('btngj,btjnd->btngd', p, v_sel)
        return o.reshape(b, Q_BLOCK, N_HEADS * HEAD_DIM)

    out = lax.map(block, (to_blocks(q), to_blocks(iq), to_blocks(iw), kpos.reshape(nb, Q_BLOCK)))
    return jnp.swapaxes(out, 0, 1).reshape(b, s, N_HEADS * HEAD_DIM)


def sample_sparse_attention(q, k_new, v_new, iq, iw, ik_new, cache_k, cache_v, cache_idx_k, page_table):
    db, t = q.shape[:2]
    page = cache_k.shape[1]
    past = page_table.shape[1] * page
    n_keys = past + t
    n_sel = min(IDX_TOPK, n_keys // 4)
    ik_past = cache_idx_k[page_table].reshape(db, past, IDX_DIM)
    sc = indexer_scores(iq, iw, jnp.concatenate([ik_past, ik_new], axis=1))
    qpos = past + jnp.arange(t)
    kpos = jnp.arange(n_keys)
    sc = jnp.where((kpos[None, :] <= qpos[:, None])[None], sc, -jnp.inf)
    _, idx = lax.top_k(sc, n_sel)
    is_past = idx < past
    pidx = jnp.where(is_past, idx, 0)
    bi = jnp.arange(db)[:, None, None]
    phys = page_table[bi, pidx // page]
    off = pidx % page
    k_p = cache_k[phys, off]
    v_p = cache_v[phys, off]
    new_sel = jnp.any(idx[..., None] == qpos[None, None, None, :], axis=2)
    new_sel = new_sel & jnp.tril(jnp.ones((t, t), bool))[None]
    grp = N_HEADS // N_KV_HEADS
    qg = q.reshape(db, t, N_KV_HEADS, grp, HEAD_DIM)
    scale = HEAD_DIM ** -0.5
    s_p = jnp.einsum('btngd,btjnd->btngj', qg, k_p, preferred_element_type=F32) * scale
    s_n = jnp.einsum('btngd,bsnd->btngs', qg, k_new, preferred_element_type=F32) * scale
    s_p = jnp.where(is_past[:, :, None, None, :], s_p, -jnp.inf)
    s_n = jnp.where(new_sel[:, :, None, None, :], s_n, -jnp.inf)
    p = jax.nn.softmax(jnp.concatenate([s_p, s_n], axis=-1), axis=-1)
    o = (jnp.einsum('btngj,btjnd->btngd', p[..., :n_sel], v_p)
         + jnp.einsum('btngs,bsnd->btngd', p[..., n_sel:], v_new))
    return o.reshape(db, t, N_HEADS * HEAD_DIM)


def causal_conv(xpad, w_conv):
    c = xpad.shape[-1]
    return lax.conv_general_dilated(xpad, w_conv[:, None, :], window_strides=(1,),
                                    padding='VALID', dimension_numbers=('NWC', 'WIO', 'NWC'),
                                    feature_group_count=c)


def chunk_gated_delta(q, k, v, beta, g, s0):
    b, t, h, dk = q.shape
    dv = v.shape[-1]
    c = min(DELTA_CHUNK, t)
    n = -(-t // c)
    pad = n * c - t

    def chunks(a):
        a = jnp.pad(a, [(0, 0), (0, pad)] + [(0, 0)] * (a.ndim - 2))
        a = a.reshape((b, n, c) + a.shape[2:])
        return jnp.moveaxis(a, (1, 3), (0, 2))

    qc, kc, vc, bc, gc = [chunks(a) for a in (q, k, v, beta, g)]
    gc = jnp.cumsum(gc, axis=-1)
    tri = jnp.tril(jnp.ones((c, c), bool))
    strict = jnp.tril(jnp.ones((c, c), bool), -1)
    decay = jnp.exp(jnp.where(tri, gc[..., :, None] - gc[..., None, :], -jnp.inf))
    kb = kc * bc[..., None]
    a_low = jnp.where(strict, jnp.einsum('nbhid,nbhjd->nbhij', kb, kc) * decay, 0.0)
    rhs = jnp.concatenate([vc * bc[..., None], kb * jnp.exp(gc)[..., None]], axis=-1)
    sol = lax.linalg.triangular_solve(a_low + jnp.eye(c, dtype=F32), rhs, left_side=True,
                                      lower=True, unit_diagonal=True)
    u, w = sol[..., :dv], sol[..., dv:]
    qk = jnp.where(tri, jnp.einsum('nbhid,nbhjd->nbhij', qc, kc) * decay, 0.0)

    def step(state, xs):
        q_i, k_i, u_i, w_i, g_i, qk_i = xs
        v_corr = u_i - jnp.einsum('bhck,bhkv->bhcv', w_i, state)
        o_i = (jnp.einsum('bhck,bhkv->bhcv', q_i * jnp.exp(g_i)[..., None], state)
               + jnp.einsum('bhij,bhjv->bhiv', qk_i, v_corr))
        g_last = g_i[..., -1:]
        state = (state * jnp.exp(g_last)[..., None]
                 + jnp.einsum('bhck,bhcv->bhkv', k_i * jnp.exp(g_last - g_i)[..., None], v_corr))
        return state, o_i

    s_final, o = lax.scan(step, s0, (qc, kc, u, w, gc, qk))
    o = jnp.moveaxis(o, (0, 2), (1, 3)).reshape(b, n * c, h, dv)[:, :t]
    return o, s_final


def gated_delta_branch(qkv, b_d, a_d, z, conv_buf, s0, w_conv, a_log, dt_bias, norm_g):
    b, t, _ = qkv.shape
    xpad = jnp.concatenate([conv_buf, qkv], axis=1)
    new_buf = xpad[:, -(CONV_WIDTH - 1):]
    hc = jax.nn.silu(causal_conv(xpad, w_conv))
    q, k, v = jnp.split(hc, (DELTA_QK_WIDTH, 2 * DELTA_QK_WIDTH), axis=-1)
    q = l2norm(q.reshape(b, t, N_DELTA_HEADS, DELTA_DK)) * DELTA_DK ** -0.5
    k = l2norm(k.reshape(b, t, N_DELTA_HEADS, DELTA_DK))
    v = v.reshape(b, t, N_DELTA_HEADS, DELTA_DV)
    beta = jax.nn.sigmoid(b_d)
    g = -jnp.exp(a_log) * jax.nn.softplus(a_d + dt_bias)
    o, s_new = chunk_gated_delta(q, k, v, beta, g, s0)
    o = o * lax.rsqrt(jnp.mean(o * o, -1, keepdims=True) + 1e-6) * norm_g
    o = o * jax.nn.silu(z.reshape(b, t, N_DELTA_HEADS, DELTA_DV))
    return o.reshape(b, t, DELTA_V_WIDTH), s_new, new_buf


def routed_experts(xt, expert_idx, gate, w_gate, w_up, w_down):
    n, d = xt.shape
    k = expert_idx.shape[1]
    n_exp = w_gate.shape[0]
    blk = EXPERT_BLOCK
    flat_e = expert_idx.reshape(-1)
    order = jnp.argsort(flat_e)
    sorted_e = flat_e[order]
    counts = jnp.bincount(flat_e, length=n_exp)
    padded = (counts + blk - 1) // blk * blk
    pad_end = jnp.cumsum(padded)
    pad_start = pad_end - padded
    start = jnp.cumsum(counts) - counts
    dest = pad_start[sorted_e] + jnp.arange(n * k) - start[sorted_e]
    n_blocks = -(-(n * k) // blk) + n_exp
    rows = n_blocks * blk
    row_token = jnp.full((rows,), n, jnp.int32).at[dest].set((order // k).astype(jnp.int32))
    row_gate = jnp.zeros((rows,), xt.dtype).at[dest].set(gate.reshape(-1)[order])
    block_expert = jnp.minimum(jnp.searchsorted(pad_end, jnp.arange(n_blocks) * blk, side='right'),
                               n_exp - 1).astype(jnp.int32)
    x_rows = jnp.concatenate([xt.astype(BF16), jnp.zeros((1, d), BF16)], axis=0)[row_token]
    h_rows = grouped_swiglu_up(x_rows, block_expert, w_gate, w_up, blk)
    y_rows = grouped_down(h_rows, block_expert, w_down, blk)
    out = jnp.zeros((n + 1, d), xt.dtype).at[row_token].add(y_rows * row_gate[:, None])
    return out[:n]


def moe(x, w_router, router_bias, w_exp_gate, w_exp_up, w_exp_down, w_sh_gate, w_sh_up, w_sh_down):
    shape = x.shape
    xt = x.reshape(-1, shape[-1])
    n = xt.shape[0]
    scores = jax.nn.sigmoid(jnp.matmul(xt, w_router, preferred_element_type=F32))
    biased = scores + router_bias
    per_group = N_EXPERTS // N_GROUPS
    group_score = lax.top_k(biased.reshape(n, N_GROUPS, per_group), 2)[0].sum(-1)
    _, top_groups = lax.top_k(group_score, TOPK_GROUPS)
    group_keep = jnp.any(top_groups[:, :, None] == jnp.arange(N_GROUPS)[None, None, :], axis=1)
    biased = jnp.where(jnp.repeat(group_keep, per_group, axis=1), biased, -jnp.inf)
    _, expert_idx = lax.top_k(biased, TOP_K)
    gate = jnp.take_along_axis(scores, expert_idx, axis=1)
    gate = gate / jnp.sum(gate, -1, keepdims=True) * ROUTED_SCALE
    routed = routed_experts(xt, expert_idx, gate, w_exp_gate, w_exp_up, w_exp_down)
    xb = xt.astype(BF16)
    sg = matmul(xb, w_sh_gate)
    su = matmul(xb, w_sh_up)
    shared = matmul((jax.nn.silu(sg) * su).astype(BF16), w_sh_down)
    return (routed + shared).reshape(shape)


def run_layer(x, pos, attend, conv_buf, s0, lw):
    (l_in, l_conv, l_alog, l_dtb, l_ng, l_ba, l_bd, l_out, l_g1, l_b1,
     l_r, l_rb, l_eg, l_eu, l_ed, l_sg, l_su, l_sd, l_g2, l_b2) = lw
    b, t, d = x.shape
    q, k, v, iq, ik, iw, qkv, b_d, a_d, z, gates = project_inputs(x, l_in, pos)
    a_out = attend(q, k, v, iq, iw, ik)
    d_out, s_new, conv_new = gated_delta_branch(qkv, b_d, a_d, z, conv_buf, s0, l_conv, l_alog, l_dtb, l_ng)
    g_a, g_d = jnp.split(jax.nn.sigmoid(gates), 2, axis=-1)
    pa = matmul(a_out.reshape(b * t, -1).astype(BF16), l_ba).reshape(b, t, d)
    pd = matmul(d_out.reshape(b * t, -1).astype(BF16), l_bd).reshape(b, t, d)
    merged = g_a * pa + g_d * pd
    mo = matmul(merged.reshape(b * t, d).astype(BF16), l_out).reshape(b, t, d)
    h = layer_norm(ALPHA * x + mo, l_g1, l_b1)
    y = layer_norm(ALPHA * h + moe(h, l_r, l_rb, l_eg, l_eu, l_ed, l_sg, l_su, l_sd), l_g2, l_b2)
    return y, k, v, ik, s_new, conv_new


def kernel(x_prompt, x_sample, cache_k, cache_v, cache_idx_k, page_table, state_delta, state_conv, w_in, w_conv, a_log, dt_bias, delta_norm_g, w_branch_attn, w_branch_delta, w_out, ln1_g, ln1_b, w_router, router_bias, w_exp_gate, w_exp_up, w_exp_down, w_sh_gate, w_sh_up, w_sh_down, ln2_g, ln2_b):
    b, s = x_prompt.shape[:2]
    db, t = x_sample.shape[:2]
    past = page_table.shape[1] * cache_k.shape[2]
    pos_prompt = jnp.arange(s)
    pos_sample = past + jnp.arange(t)
    lw = [a[0] for a in (w_in, w_conv, a_log, dt_bias, delta_norm_g, w_branch_attn, w_branch_delta, w_out,
                         ln1_g, ln1_b, w_router, router_bias, w_exp_gate, w_exp_up, w_exp_down,
                         w_sh_gate, w_sh_up, w_sh_down, ln2_g, ln2_b)]
    ck, cv, cik = cache_k[0], cache_v[0], cache_idx_k[0]
    sample_attend = functools.partial(sample_sparse_attention, cache_k=ck, cache_v=cv, cache_idx_k=cik,
                                      page_table=page_table)
    yp, kp, vp, ikp, sdp, scp = run_layer(
        x_prompt, pos_prompt, prompt_sparse_attention,
        jnp.zeros((b, CONV_WIDTH - 1, CONV_CH), x_prompt.dtype),
        jnp.zeros((b, N_DELTA_HEADS, DELTA_DK, DELTA_DV), state_delta.dtype), lw)
    ys, ksn, vsn, iksn, sds, scs = run_layer(x_sample, pos_sample, sample_attend, state_conv[0], state_delta[0], lw)
    return (yp, ys, kp[None], vp[None], ikp[None], sdp[None], scp[None],
            ksn[None], vsn[None], iksn[None], sds[None], scs[None])
```

```python
import collections
import functools

import jax
import jax.numpy as jnp
import numpy as np
from jax import lax
from jax.experimental import pallas as pl
from jax.experimental.pallas import tpu as pltpu

D_MODEL = 4096
DEPTH = 1
N_HEADS = 16
N_KV_HEADS = 4
HEAD_DIM = 128
ROPE_THETA = 500000.0
N_IDX_HEADS = 16
IDX_DIM = 128
IDX_TOPK = 256
Q_BLOCK = 128
N_DELTA_HEADS = 16
DELTA_DK = 128
DELTA_DV = 128
CONV_WIDTH = 4
DELTA_CHUNK = 64
N_EXPERTS = 64
TOP_K = 8
N_GROUPS = 8
TOPK_GROUPS = 4
EXPERT_DIM = 1024
ROUTED_SCALE = 2.5
EXPERT_BLOCK = 256
ALPHA = (2 * DEPTH) ** 0.25
LN_EPS = 1e-5
ATTN_WIDTH = N_HEADS * HEAD_DIM
KV_WIDTH = N_KV_HEADS * HEAD_DIM
IDXQ_WIDTH = N_IDX_HEADS * IDX_DIM
DELTA_QK_WIDTH = N_DELTA_HEADS * DELTA_DK
DELTA_V_WIDTH = N_DELTA_HEADS * DELTA_DV
CONV_CH = 2 * DELTA_QK_WIDTH + DELTA_V_WIDTH
IN_SIZES = (ATTN_WIDTH, KV_WIDTH, KV_WIDTH, IDXQ_WIDTH, IDX_DIM, N_IDX_HEADS, CONV_CH,
            N_DELTA_HEADS, N_DELTA_HEADS, DELTA_V_WIDTH, 2 * D_MODEL)
IN_SPLITS = tuple(int(s) for s in np.cumsum(IN_SIZES)[:-1])

V7X_VMEM_LIMIT = 56 * 1024 * 1024
BF16 = jnp.bfloat16
F32 = jnp.float32


def _mm_kernel(x_ref, w_ref, o_ref, wbf_ref):
    @pl.when(pl.program_id(1) == 0)
    def _():
        wbf_ref[...] = w_ref[...].astype(BF16)

    o_ref[...] = jnp.dot(x_ref[...], wbf_ref[...], preferred_element_type=F32).astype(o_ref.dtype)


def _row_tile(m):
    for tm in (1024, 1088, 512, 256, 128):
        if m % tm == 0:
            return tm
    raise ValueError(f"no row tile for {m} rows")


def matmul(x, w, out_dtype=F32, tn=512):
    m, k = x.shape
    n = w.shape[1]
    tm = _row_tile(m)
    tn = min(tn, n)
    return pl.pallas_call(
        _mm_kernel,
        grid=(pl.cdiv(n, tn), m // tm),
        in_specs=[pl.BlockSpec((tm, k), lambda j, i: (i, 0)),
                  pl.BlockSpec((k, tn), lambda j, i: (0, j))],
        out_specs=pl.BlockSpec((tm, tn), lambda j, i: (i, j)),
        out_shape=jax.ShapeDtypeStruct((m, n), out_dtype),
        scratch_shapes=[pltpu.VMEM((k, tn), BF16)],
        compiler_params=pltpu.CompilerParams(dimension_semantics=("arbitrary", "arbitrary"),
                                             vmem_limit_bytes=V7X_VMEM_LIMIT),
        name="dense_matmul",
    )(x, w)


LANES = 128


def _mm_cols_kernel(x_ref, w_ref, o_ref, wbf_ref, *, off):
    @pl.when(pl.program_id(1) == 0)
    def _():
        k, tn = wbf_ref.shape
        rows = min(k, 512)
        for r in range(0, k, rows):
            wbf_ref[r:r + rows, :] = w_ref[r:r + rows, off:off + tn].astype(BF16)

    o_ref[...] = jnp.dot(x_ref[...], wbf_ref[...], preferred_element_type=F32).astype(o_ref.dtype)


def matmul_cols(x, w, col0, ncols, out_dtype=F32, tn=512):
    m, k = x.shape
    tm = _row_tile(m)
    tn = min(tn, ncols)
    off = col0 % LANES
    base = col0 - off
    assert k % min(k, 512) == 0 and tn % LANES == 0
    w_spec = pl.BlockSpec((pl.Element(k), pl.Element(tn + LANES, (0, tn + LANES))),
                          lambda j, i: (0, pl.multiple_of(base + j * tn, LANES)))
    return pl.pallas_call(
        functools.partial(_mm_cols_kernel, off=off),
        grid=(pl.cdiv(ncols, tn), m // tm),
        in_specs=[pl.BlockSpec((tm, k), lambda j, i: (i, 0)), w_spec],
        out_specs=pl.BlockSpec((tm, tn), lambda j, i: (i, j)),
        out_shape=jax.ShapeDtypeStruct((m, ncols), out_dtype),
        scratch_shapes=[pltpu.VMEM((k, tn), BF16)],
        compiler_params=pltpu.CompilerParams(dimension_semantics=("arbitrary", "arbitrary"),
                                             vmem_limit_bytes=V7X_VMEM_LIMIT),
        name="dense_matmul_cols",
    )(x, w)


def _expert_changed(be_ref, i):
    prev = be_ref[jnp.maximum(i - 1, 0)]
    return jnp.logical_or(i == 0, be_ref[i] != prev)


def _moe_up_kernel(be_ref, nu_ref, x_ref, wg_ref, wu_ref, h_ref, wg_bf, wu_bf):
    i = pl.program_id(1)

    @pl.when(i < nu_ref[0])
    def _():
        @pl.when(_expert_changed(be_ref, i))
        def _():
            wg_bf[...] = wg_ref[...].astype(BF16)
            wu_bf[...] = wu_ref[...].astype(BF16)

        x = x_ref[...]
        g = jnp.dot(x, wg_bf[...], preferred_element_type=F32)
        u = jnp.dot(x, wu_bf[...], preferred_element_type=F32)
        h_ref[...] = (g * jax.nn.sigmoid(g) * u).astype(h_ref.dtype)


def _moe_down_kernel(be_ref, nu_ref, h_ref, gate_ref, wd_ref, y_ref, wd_bf):
    i = pl.program_id(1)

    @pl.when(i < nu_ref[0])
    def _():
        @pl.when(_expert_changed(be_ref, i))
        def _():
            wd_bf[...] = wd_ref[...].astype(BF16)

        y = jnp.dot(h_ref[...], wd_bf[...], preferred_element_type=F32)
        y_ref[...] = (y * gate_ref[...]).astype(y_ref.dtype)


def grouped_swiglu_up(x_rows, block_expert, n_used, w_gate, w_up, blk, tf=512):
    rows, d = x_rows.shape
    f = w_gate.shape[2]
    tf = min(tf, f)
    n_blocks = rows // blk
    grid_spec = pltpu.PrefetchScalarGridSpec(
        num_scalar_prefetch=2,
        grid=(f // tf, n_blocks),
        in_specs=[pl.BlockSpec((blk, d), lambda j, i, be, nu: (i, 0)),
                  pl.BlockSpec((None, d, tf), lambda j, i, be, nu: (be[i], 0, j)),
                  pl.BlockSpec((None, d, tf), lambda j, i, be, nu: (be[i], 0, j))],
        out_specs=pl.BlockSpec((blk, tf), lambda j, i, be, nu: (i, j)),
        scratch_shapes=[pltpu.VMEM((d, tf), BF16), pltpu.VMEM((d, tf), BF16)],
    )
    return pl.pallas_call(
        _moe_up_kernel, grid_spec=grid_spec,
        out_shape=jax.ShapeDtypeStruct((rows, f), BF16),
        compiler_params=pltpu.CompilerParams(dimension_semantics=("arbitrary", "arbitrary"),
                                             vmem_limit_bytes=V7X_VMEM_LIMIT),
        name="moe_up",
    )(block_expert, n_used, x_rows, w_gate, w_up)


def grouped_down(h_rows, row_gate, block_expert, n_used, w_down, blk, tn=2048):
    rows, f = h_rows.shape
    d = w_down.shape[2]
    tn = min(tn, d)
    n_blocks = rows // blk
    grid_spec = pltpu.PrefetchScalarGridSpec(
        num_scalar_prefetch=2,
        grid=(d // tn, n_blocks),
        in_specs=[pl.BlockSpec((blk, f), lambda j, i, be, nu: (i, 0)),
                  pl.BlockSpec((blk, 1), lambda j, i, be, nu: (i, 0)),
                  pl.BlockSpec((None, f, tn), lambda j, i, be, nu: (be[i], 0, j))],
        out_specs=pl.BlockSpec((blk, tn), lambda j, i, be, nu: (i, j)),
        scratch_shapes=[pltpu.VMEM((f, tn), BF16)],
    )
    return pl.pallas_call(
        _moe_down_kernel, grid_spec=grid_spec,
        out_shape=jax.ShapeDtypeStruct((rows, d), F32),
        compiler_params=pltpu.CompilerParams(dimension_semantics=("arbitrary", "arbitrary"),
                                             vmem_limit_bytes=V7X_VMEM_LIMIT),
        name="moe_down",
    )(block_expert, n_used, h_rows, row_gate, w_down)


def layer_norm(x, g, b):
    xc = x - jnp.mean(x, -1, keepdims=True)
    var = jnp.mean(xc * xc, -1, keepdims=True)
    return xc * lax.rsqrt(var + LN_EPS) * g + b


def l2norm(x):
    return x * lax.rsqrt(jnp.sum(x * x, -1, keepdims=True) + 1e-6)


def partial_rope(x, pos):
    rot = x.shape[-1] // 4
    half = rot // 2
    inv_freq = ROPE_THETA ** (-jnp.arange(half, dtype=F32) / half)
    ang = pos.astype(F32)[:, None] * inv_freq[None, :]
    cos = jnp.cos(ang)[:, None, :]
    sin = jnp.sin(ang)[:, None, :]
    x1, x2, rest = x[..., :half], x[..., half:rot], x[..., rot:]
    return jnp.concatenate([x1 * cos - x2 * sin, x2 * cos + x1 * sin, rest], axis=-1)


ATTN_COLS = IN_SPLITS[5]
DELTA_COLS = IN_SPLITS[8] - IN_SPLITS[5]
GATE_COLS = int(sum(IN_SIZES)) - IN_SPLITS[8]
AttnInputs = collections.namedtuple("AttnInputs", "q k v iq ik iw")


def attention_inputs(p, pos):
    b, t, _ = p.shape
    q, k, v, iq, ik, iw = jnp.split(p, IN_SPLITS[:5], axis=-1)
    q = partial_rope(q.reshape(b, t, N_HEADS, HEAD_DIM), pos)
    k = partial_rope(k.reshape(b, t, N_KV_HEADS, HEAD_DIM), pos)
    v = v.reshape(b, t, N_KV_HEADS, HEAD_DIM)
    iq = partial_rope(iq.reshape(b, t, N_IDX_HEADS, IDX_DIM), pos)
    ik = partial_rope(ik[:, :, None, :], pos)[:, :, 0, :]
    iw = iw * (N_IDX_HEADS ** -0.5 * IDX_DIM ** -0.5)
    return AttnInputs(q, k, v, iq, ik, iw)


KEY_CHUNK = 512
MASKED = -1e30
INT32_MIN = -2 ** 31
INT32_MAX = 2 ** 31 - 1
NEG_INF_KEY = int(np.int32(np.uint32(0xFF800000) ^ np.uint32(0x7FFFFFFF)))


def _sort_key(x):
    bits = lax.bitcast_convert_type(x, jnp.int32)
    return bits ^ (jnp.right_shift(bits, 31) & jnp.int32(INT32_MAX))


def _lane_tile(x, n):
    return x if n == 1 else jnp.concatenate([x] * n, axis=1)


def _count_rows(keys_ref, nkc, preds):
    _, rows, kc = keys_ref.shape

    def body(c, parts):
        keys = keys_ref[c]
        out = []
        for pred, part in zip(preds, parts):
            hit = pred(keys, c)
            for j in range(kc // 128):
                part = part + hit[:, j * 128:(j + 1) * 128]
            out.append(part)
        return tuple(out)

    parts = lax.fori_loop(0, nkc, body, tuple(jnp.zeros((rows, 128), F32) for _ in preds))
    return [jnp.broadcast_to(jnp.sum(part, axis=1, keepdims=True), (rows, 128)) for part in parts]


def _topk_selection(keys_ref, nkc, n_sel, radix_bits):
    _, rows, kc = keys_ref.shape
    reps = kc // 128
    col = lax.broadcasted_iota(jnp.int32, (rows, kc), 1)

    def thr_digit(it, carry):
        thr, cnt_thr = carry
        shift = 32 - radix_bits * (it + 1)
        cands = [thr + jnp.left_shift(jnp.int32(j), shift) for j in range(1, 2 ** radix_bits)]
        cands_w = [_lane_tile(cand, reps) for cand in cands]
        cnts = _count_rows(keys_ref, nkc, [lambda keys, c, cw=cw: jnp.where(keys >= cw, 1.0, 0.0) for cw in cands_w])
        for cand, cnt in zip(cands, cnts):
            ok = cnt >= n_sel
            thr = jnp.where(ok, cand, thr)
            cnt_thr = jnp.where(ok, cnt, cnt_thr)
        return thr, cnt_thr

    thr0 = jnp.full((rows, 128), INT32_MIN, jnp.int32)
    cnt0 = jnp.broadcast_to(jnp.asarray(nkc * kc).astype(F32), (rows, 128))
    thr, cnt_thr = lax.fori_loop(0, 32 // radix_bits, thr_digit, (thr0, cnt0))

    short = thr == NEG_INF_KEY
    thr_w = _lane_tile(thr, reps)
    cnt_gt, = _count_rows(keys_ref, nkc, [lambda keys, c: jnp.where(keys > thr_w, 1.0, 0.0)])
    need = n_sel - cnt_gt
    tied = jnp.logical_and(cnt_thr > n_sel, jnp.logical_not(short))

    def tie_limit():
        def idx_bit(it, x):
            cand = x + jnp.left_shift(jnp.int32(1), 30 - it)
            cand_w = _lane_tile(cand, reps)
            cnt, = _count_rows(keys_ref, nkc, [lambda keys, c: jnp.where(
                keys == thr_w, jnp.where(c * kc + col < cand_w, 1.0, 0.0), 0.0)])
            return jnp.where(cnt < need, cand, x)
        return lax.fori_loop(0, 31, idx_bit, jnp.zeros((rows, 128), jnp.int32))

    any_tied = jnp.max(jnp.where(tied, 1.0, 0.0)) > 0.0
    lim = lax.cond(any_tied, tie_limit, lambda: jnp.full((rows, 128), INT32_MAX, jnp.int32))
    lim = jnp.where(short, -1, jnp.where(tied, lim, INT32_MAX))
    return thr, lim


def _selected(keys, index, thr_w, lim_w, yes, no):
    keep_tie = jnp.where(index <= lim_w, yes, no)
    return jnp.where(keys > thr_w, yes, jnp.where(keys == thr_w, keep_tie, no))


def _dsa_prompt_kernel(iq_ref, iw_ref, ikt_ref, q_ref, kt_ref, v_ref, o_ref,
                       keys_ref, m_ref, l_ref, acc_ref, *, n_sel):
    qb, kc = Q_BLOCK, KEY_CHUNK
    reps = kc // 128
    grp = N_HEADS // N_KV_HEADS
    i = pl.program_id(0)
    nkc = (i * qb + qb + kc - 1) // kc
    qpos = i * qb + lax.broadcasted_iota(jnp.int32, (qb, kc), 0)
    col = lax.broadcasted_iota(jnp.int32, (qb, kc), 1)

    def score_chunk(c, carry):
        ikc = ikt_ref[c]
        acc = jnp.zeros((qb, kc), F32)
        for h in range(N_IDX_HEADS):
            s = jnp.dot(iq_ref[h], ikc, preferred_element_type=F32)
            acc = acc + jnp.maximum(s, 0.0) * iw_ref[:, h:h + 1]
        acc = jnp.where(c * kc + col <= qpos, acc, -jnp.inf)
        keys_ref[c] = _sort_key(acc)
        return carry

    lax.fori_loop(0, nkc, score_chunk, 0)

    thr, lim = _topk_selection(keys_ref, nkc, n_sel, radix_bits=1)
    thr_w, lim_w = _lane_tile(thr, reps), _lane_tile(lim, reps)

    m_ref[...] = jnp.full(m_ref.shape, MASKED, F32)
    l_ref[...] = jnp.zeros(l_ref.shape, F32)
    acc_ref[...] = jnp.zeros(acc_ref.shape, F32)
    rows = grp * qb

    def attend_chunk(c, carry):
        bias = _selected(keys_ref[c], c * kc + col, thr_w, lim_w, 0.0, MASKED)
        for n in range(N_KV_HEADS):
            r = pl.ds(n * rows, rows)
            qn = q_ref[n * grp:(n + 1) * grp].reshape(rows, HEAD_DIM)
            s = jnp.dot(qn, kt_ref[n, c], preferred_element_type=F32) * (HEAD_DIM ** -0.5)
            s = (s.reshape(grp, qb, kc) + bias[None]).reshape(rows, kc)
            m_prev = m_ref[r, :]
            m_new = jnp.maximum(m_prev, jnp.max(s, axis=1, keepdims=True))
            alpha = jnp.exp(m_prev - m_new)
            p = jnp.exp(s - _lane_tile(m_new, reps))
            l_ref[r, :] = alpha * l_ref[r, :] + jnp.sum(p, axis=1, keepdims=True)
            acc_ref[r, :] = alpha * acc_ref[r, :] + jnp.dot(p.astype(BF16), v_ref[n, c],
                                                           preferred_element_type=F32)
            m_ref[r, :] = m_new
        return carry

    lax.fori_loop(0, nkc, attend_chunk, 0)
    for h in range(N_HEADS):
        r = pl.ds(h * qb, qb)
        o_ref[:, h * HEAD_DIM:(h + 1) * HEAD_DIM] = (acc_ref[r, :] / l_ref[r, :]).astype(o_ref.dtype)


def prompt_sparse_attention(q, k, v, iq, iw, ik):
    b, s = q.shape[:2]
    assert b == 1 and s % KEY_CHUNK == 0 and KEY_CHUNK % Q_BLOCK == 0 and KEY_CHUNK >= IDX_TOPK
    n_sel = min(IDX_TOPK, s // 4)
    nb, nc, kc = s // Q_BLOCK, s // KEY_CHUNK, KEY_CHUNK

    def head_major(a):
        return a.astype(BF16).reshape(nb, Q_BLOCK, a.shape[2], a.shape[3]).transpose(0, 2, 1, 3)

    ikt = ik.astype(BF16).reshape(nc, kc, IDX_DIM).transpose(0, 2, 1)
    kt = k.astype(BF16).reshape(nc, kc, N_KV_HEADS, HEAD_DIM).transpose(2, 0, 3, 1)
    vc = v.astype(BF16).reshape(nc, kc, N_KV_HEADS, HEAD_DIM).transpose(2, 0, 1, 3)
    resident = dict(pipeline_mode=pl.Buffered(1))
    out = pl.pallas_call(
        functools.partial(_dsa_prompt_kernel, n_sel=n_sel),
        grid=(nb,),
        in_specs=[pl.BlockSpec((None, N_IDX_HEADS, Q_BLOCK, IDX_DIM), lambda i: (i, 0, 0, 0)),
                  pl.BlockSpec((Q_BLOCK, N_IDX_HEADS), lambda i: (i, 0)),
                  pl.BlockSpec((nc, IDX_DIM, kc), lambda i: (0, 0, 0), **resident),
                  pl.BlockSpec((None, N_HEADS, Q_BLOCK, HEAD_DIM), lambda i: (i, 0, 0, 0)),
                  pl.BlockSpec((N_KV_HEADS, nc, HEAD_DIM, kc), lambda i: (0, 0, 0, 0), **resident),
                  pl.BlockSpec((N_KV_HEADS, nc, kc, HEAD_DIM), lambda i: (0, 0, 0, 0), **resident)],
        out_specs=pl.BlockSpec((Q_BLOCK, N_HEADS * HEAD_DIM), lambda i: (i, 0)),
        out_shape=jax.ShapeDtypeStruct((s, N_HEADS * HEAD_DIM), BF16),
        scratch_shapes=[pltpu.VMEM((nc, Q_BLOCK, kc), jnp.int32),
                        pltpu.VMEM((N_HEADS * Q_BLOCK, 128), F32),
                        pltpu.VMEM((N_HEADS * Q_BLOCK, 128), F32),
                        pltpu.VMEM((N_HEADS * Q_BLOCK, HEAD_DIM), F32)],
        compiler_params=pltpu.CompilerParams(dimension_semantics=("arbitrary",),
                                             vmem_limit_bytes=V7X_VMEM_LIMIT),
        name="dsa_prompt",
    )(head_major(iq), iw.reshape(s, N_IDX_HEADS), ikt, head_major(q), kt, vc)
    return out.reshape(b, s, N_HEADS * HEAD_DIM)


T_PAD = 8


def _dsa_sample_kernel(pt_ref, iq_ref, iw_ref, q_ref, iknew_ref, knew_ref, vnew_ref, *rest, n_sel, n_pages):
    del pt_ref
    ik_pages, k_pages, v_pages = rest[:n_pages], rest[n_pages:2 * n_pages], rest[2 * n_pages:3 * n_pages]
    o_ref, keys_ref, s_ref = rest[3 * n_pages:]
    page = ik_pages[0].shape[0]
    kv_rows = k_pages[0].shape[0]
    rows = N_HEADS * T_PAD
    nkc = n_pages + 1
    nt = (((1,), (1,)), ((), ()))
    row_t = lax.broadcasted_iota(jnp.int32, (T_PAD, page), 0)
    col = lax.broadcasted_iota(jnp.int32, (T_PAD, page), 1)

    for c in range(nkc):
        ikc = ik_pages[c][...].astype(BF16) if c < n_pages else iknew_ref[...]
        s = lax.dot_general(iq_ref[...], ikc, nt, preferred_element_type=F32)
        x = jnp.maximum(s, 0.0) * iw_ref[...]
        sc = jnp.sum(x.reshape(T_PAD, N_IDX_HEADS, page), axis=1)
        if c == n_pages:
            sc = jnp.where(col <= row_t, sc, -jnp.inf)
        keys_ref[c] = _sort_key(sc)

    thr, lim = _topk_selection(keys_ref, nkc, n_sel, radix_bits=4)

    scale = HEAD_DIM ** -0.5
    spread = (lax.broadcasted_iota(jnp.int32, (page, kv_rows), 1) // N_KV_HEADS
              == lax.broadcasted_iota(jnp.int32, (page, kv_rows), 0)).astype(BF16)
    own_head = (lax.broadcasted_iota(jnp.int32, (rows, kv_rows), 0) // (rows // N_KV_HEADS)
                == lax.broadcasted_iota(jnp.int32, (rows, kv_rows), 1) % N_KV_HEADS).astype(F32)
    for c in range(nkc):
        kc_ = k_pages[c][...].astype(BF16) if c < n_pages else knew_ref[...]
        s = lax.dot_general(q_ref[...], kc_, nt, preferred_element_type=F32) * scale
        sel = _selected(keys_ref[c], c * page + col, thr, lim, 1.0, 0.0).astype(BF16)
        sel = jnp.dot(sel, spread, preferred_element_type=F32)
        keep = jnp.broadcast_to(sel[None], (N_HEADS, T_PAD, kv_rows)).reshape(rows, kv_rows) * own_head
        s_ref[:, c * kv_rows:(c + 1) * kv_rows] = jnp.where(keep > 0.5, s, MASKED)
    s = s_ref[...]
    p = jnp.exp(s - jnp.max(s, axis=1, keepdims=True))
    inv_l = 1.0 / jnp.sum(p, axis=1, keepdims=True)
    p = p.astype(BF16)
    o = jnp.zeros((rows, HEAD_DIM), F32)
    for c in range(nkc):
        vc_ = v_pages[c][...].astype(BF16) if c < n_pages else vnew_ref[...]
        o = o + jnp.dot(p[:, c * kv_rows:(c + 1) * kv_rows], vc_, preferred_element_type=F32)
    o_ref[...] = o * inv_l


def sample_sparse_attention(q, k_new, v_new, iq, iw, ik_new, cache_k, cache_v, cache_idx_k, page_table):
    db, t = q.shape[:2]
    n_pool, page = cache_k.shape[:2]
    n_pages = page_table.shape[1]
    n_keys = n_pages * page + t
    n_sel = min(IDX_TOPK, n_keys // 4)
    grp = N_HEADS // N_KV_HEADS
    assert t <= T_PAD <= page and page == 128 and HEAD_DIM == 128 and IDX_DIM == 128

    def pad_t(a, to):
        return jnp.pad(a, [(0, 0), (0, to - a.shape[1])] + [(0, 0)] * (a.ndim - 2))

    kv_rows = page * N_KV_HEADS
    rows = N_HEADS * T_PAD
    iq_rows = pad_t(iq, T_PAD).astype(BF16).reshape(db, T_PAD * N_IDX_HEADS, IDX_DIM)
    iw_rows = pad_t(iw, T_PAD).reshape(db, T_PAD * N_IDX_HEADS, 1)
    q_rows = pad_t(q, T_PAD).astype(BF16).transpose(0, 2, 1, 3).reshape(db, rows, HEAD_DIM)
    iknew = pad_t(ik_new, page).astype(BF16)
    knew = pad_t(k_new, page).astype(BF16).reshape(db, kv_rows, HEAD_DIM)
    vnew = pad_t(v_new, page).astype(BF16).reshape(db, kv_rows, HEAD_DIM)
    ck = cache_k.reshape(n_pool, kv_rows, HEAD_DIM)
    cv = cache_v.reshape(n_pool, kv_rows, HEAD_DIM)

    def per_batch(*blk):
        return pl.BlockSpec((None,) + blk, lambda b, pt: (b,) + (0,) * len(blk))

    def paged(nrows):
        return [pl.BlockSpec((None, nrows, HEAD_DIM), lambda b, pt, j=j: (pt[b, j], 0, 0)) for j in range(n_pages)]

    grid_spec = pltpu.PrefetchScalarGridSpec(
        num_scalar_prefetch=1,
        grid=(db,),
        in_specs=[per_batch(T_PAD * N_IDX_HEADS, IDX_DIM), per_batch(T_PAD * N_IDX_HEADS, 1),
                  per_batch(rows, HEAD_DIM), per_batch(page, IDX_DIM), per_batch(kv_rows, HEAD_DIM),
                  per_batch(kv_rows, HEAD_DIM)]
        + paged(page) + paged(kv_rows) + paged(kv_rows),
        out_specs=per_batch(rows, HEAD_DIM),
        scratch_shapes=[pltpu.VMEM((n_pages + 1, T_PAD, page), jnp.int32),
                        pltpu.VMEM((rows, (n_pages + 1) * kv_rows), F32)],
    )
    o = pl.pallas_call(
        functools.partial(_dsa_sample_kernel, n_sel=n_sel, n_pages=n_pages),
        grid_spec=grid_spec,
        out_shape=jax.ShapeDtypeStruct((db, rows, HEAD_DIM), F32),
        compiler_params=pltpu.CompilerParams(dimension_semantics=("arbitrary",),
                                             vmem_limit_bytes=V7X_VMEM_LIMIT),
        name="dsa_sample",
    )(page_table, iq_rows, iw_rows, q_rows, iknew, knew, vnew,
      *([cache_idx_k] * n_pages), *([ck] * n_pages), *([cv] * n_pages))
    o = o.reshape(db, N_KV_HEADS, grp, T_PAD, HEAD_DIM)[:, :, :, :t]
    return o.transpose(0, 3, 1, 2, 4).reshape(db, t, N_HEADS * HEAD_DIM)


def causal_conv(xpad, w_conv):
    c = xpad.shape[-1]
    return lax.conv_general_dilated(xpad, w_conv[:, None, :], window_strides=(1,),
                                    padding='VALID', dimension_numbers=('NWC', 'WIO', 'NWC'),
                                    feature_group_count=c)


def chunk_gated_delta(q, k, v, beta, g, s0):
    b, t, h, dk = q.shape
    dv = v.shape[-1]
    c = min(DELTA_CHUNK, t)
    n = -(-t // c)
    pad = n * c - t

    def chunks(a):
        a = jnp.pad(a, [(0, 0), (0, pad)] + [(0, 0)] * (a.ndim - 2))
        a = a.reshape((b, n, c) + a.shape[2:])
        return jnp.moveaxis(a, (1, 3), (0, 2))

    qc, kc, vc, bc, gc = [chunks(a) for a in (q, k, v, beta, g)]
    gc = jnp.cumsum(gc, axis=-1)
    tri = jnp.tril(jnp.ones((c, c), bool))
    strict = jnp.tril(jnp.ones((c, c), bool), -1)
    decay = jnp.exp(jnp.where(tri, gc[..., :, None] - gc[..., None, :], -jnp.inf))
    kb = kc * bc[..., None]
    a_low = jnp.where(strict, jnp.einsum('nbhid,nbhjd->nbhij', kb, kc) * decay, 0.0)
    rhs = jnp.concatenate([vc * bc[..., None], kb * jnp.exp(gc)[..., None]], axis=-1)
    sol = lax.linalg.triangular_solve(a_low + jnp.eye(c, dtype=F32), rhs, left_side=True,
                                      lower=True, unit_diagonal=True)
    u, w = sol[..., :dv], sol[..., dv:]
    qk = jnp.where(tri, jnp.einsum('nbhid,nbhjd->nbhij', qc, kc) * decay, 0.0)

    def step(state, xs):
        q_i, k_i, u_i, w_i, g_i, qk_i = xs
        v_corr = u_i - jnp.einsum('bhck,bhkv->bhcv', w_i, state)
        o_i = (jnp.einsum('bhck,bhkv->bhcv', q_i * jnp.exp(g_i)[..., None], state)
               + jnp.einsum('bhij,bhjv->bhiv', qk_i, v_corr))
        g_last = g_i[..., -1:]
        state = (state * jnp.exp(g_last)[..., None]
                 + jnp.einsum('bhck,bhcv->bhkv', k_i * jnp.exp(g_last - g_i)[..., None], v_corr))
        return state, o_i

    s_final, o = lax.scan(step, s0, (qc, kc, u, w, gc, qk))
    o = jnp.moveaxis(o, (0, 2), (1, 3)).reshape(b, n * c, h, dv)[:, :t]
    return o, s_final


def gated_delta_branch(qkv, b_d, a_d, z, conv_buf, s0, w_conv, a_log, dt_bias, norm_g):
    b, t, _ = qkv.shape
    xpad = jnp.concatenate([conv_buf, qkv], axis=1)
    new_buf = xpad[:, -(CONV_WIDTH - 1):]
    hc = jax.nn.silu(causal_conv(xpad, w_conv))
    q, k, v = jnp.split(hc, (DELTA_QK_WIDTH, 2 * DELTA_QK_WIDTH), axis=-1)
    q = l2norm(q.reshape(b, t, N_DELTA_HEADS, DELTA_DK)) * DELTA_DK ** -0.5
    k = l2norm(k.reshape(b, t, N_DELTA_HEADS, DELTA_DK))
    v = v.reshape(b, t, N_DELTA_HEADS, DELTA_DV)
    beta = jax.nn.sigmoid(b_d)
    g = -jnp.exp(a_log) * jax.nn.softplus(a_d + dt_bias)
    o, s_new = chunk_gated_delta(q, k, v, beta, g, s0)
    o = o * lax.rsqrt(jnp.mean(o * o, -1, keepdims=True) + 1e-6) * norm_g
    o = o * jax.nn.silu(z.reshape(b, t, N_DELTA_HEADS, DELTA_DV))
    return o.reshape(b, t, DELTA_V_WIDTH), s_new, new_buf


DELTA_TOKEN_BLOCK = 256
SUBLANES = 8


def _delta_pre_kernel(x_ref, halo_ref, w_ref, q_ref, k_ref, v_ref, xe_ref):
    i = pl.program_id(0)
    tb = x_ref.shape[0]
    xe_ref[0:SUBLANES, :] = jnp.where(i == 0, 0.0, halo_ref[...])
    xe_ref[SUBLANES:SUBLANES + tb, :] = x_ref[...]
    outs = (q_ref, k_ref, v_ref)
    for col in range(3 * N_DELTA_HEADS):
        cs = slice(col * DELTA_DK, (col + 1) * DELTA_DK)
        y = sum(xe_ref[SUBLANES - (CONV_WIDTH - 1) + tap:SUBLANES - (CONV_WIDTH - 1) + tap + tb, cs] * w_ref[tap:tap + 1, cs]
                for tap in range(CONV_WIDTH))
        y = y * jax.nn.sigmoid(y)
        if col < 2 * N_DELTA_HEADS:
            y = y * lax.rsqrt(jnp.sum(y * y, axis=1, keepdims=True) + 1e-6)
        if col < N_DELTA_HEADS:
            y = y * DELTA_DK ** -0.5
        for cc in range(tb // DELTA_CHUNK):
            outs[col // N_DELTA_HEADS][cc, col % N_DELTA_HEADS] = y[cc * DELTA_CHUNK:(cc + 1) * DELTA_CHUNK]


def delta_pre(qkv, t, w_conv):
    c = w_conv.shape[1]
    tb = DELTA_TOKEN_BLOCK
    assert t % tb == 0 and tb % DELTA_CHUNK == 0 and DELTA_DK == DELTA_DV
    per_blk = tb // DELTA_CHUNK
    out_spec = pl.BlockSpec((per_blk, N_DELTA_HEADS, DELTA_CHUNK, DELTA_DK), lambda i: (i, 0, 0, 0))
    out_shape = jax.ShapeDtypeStruct((t // DELTA_CHUNK, N_DELTA_HEADS, DELTA_CHUNK, DELTA_DK), F32)
    return pl.pallas_call(
        _delta_pre_kernel,
        grid=(t // tb,),
        in_specs=[pl.BlockSpec((tb, c), lambda i: (i, 0)),
                  pl.BlockSpec((SUBLANES, c), lambda i: (jnp.maximum(i * (tb // SUBLANES) - 1, 0), 0)),
                  pl.BlockSpec((CONV_WIDTH, c), lambda i: (0, 0))],
        out_specs=[out_spec] * 3,
        out_shape=[out_shape] * 3,
        scratch_shapes=[pltpu.VMEM((tb + SUBLANES, c), F32)],
        compiler_params=pltpu.CompilerParams(dimension_semantics=("arbitrary",),
                                             vmem_limit_bytes=V7X_VMEM_LIMIT),
        name="delta_pre",
    )(qkv, qkv, w_conv)


def _delta_scan_kernel(wq_ref, kt_ref, u_ref, qk_ref, el_ref, z_ref, ng_ref, o_ref, sfin_ref, s_ref):
    n = pl.program_id(0)
    c = DELTA_CHUNK

    @pl.when(n == 0)
    def _():
        s_ref[...] = jnp.zeros(s_ref.shape, F32)

    for h in range(N_DELTA_HEADS):
        s = s_ref[h]
        ws = jnp.dot(wq_ref[h], s.astype(BF16), preferred_element_type=F32)
        v_corr = (u_ref[h] - ws[:c]).astype(BF16)
        o = ws[c:] + jnp.dot(qk_ref[h], v_corr, preferred_element_type=F32)
        s_ref[h] = s * el_ref[h] + jnp.dot(kt_ref[h], v_corr, preferred_element_type=F32)
        o = o * lax.rsqrt(jnp.mean(o * o, axis=1, keepdims=True) + 1e-6) * ng_ref[...]
        zz = z_ref[:, h * DELTA_DV:(h + 1) * DELTA_DV]
        o_ref[:, h * DELTA_DV:(h + 1) * DELTA_DV] = (o * (zz * jax.nn.sigmoid(zz))).astype(o_ref.dtype)

    @pl.when(n == pl.num_programs(0) - 1)
    def _():
        sfin_ref[...] = s_ref[...]


def prompt_delta_branch(p_delta, p_gate, t, w_conv, a_log, dt_bias, norm_g):
    assert t >= CONV_WIDTH - 1
    c, h = DELTA_CHUNK, N_DELTA_HEADS
    n = t // c
    new_buf = p_delta[t - (CONV_WIDTH - 1):t, :CONV_CH]
    qc, kc, vc = delta_pre(p_delta, t, w_conv)
    beta = jax.nn.sigmoid(p_delta[:t, CONV_CH:CONV_CH + h])
    g = -jnp.exp(a_log) * jax.nn.softplus(p_delta[:t, CONV_CH + h:CONV_CH + 2 * h] + dt_bias)
    bc = beta.reshape(n, c, h).transpose(0, 2, 1)
    gc = jnp.cumsum(g.reshape(n, c, h).transpose(0, 2, 1), axis=-1)
    tri = jnp.tril(jnp.ones((c, c), bool))
    strict = jnp.tril(jnp.ones((c, c), bool), -1)
    decay = jnp.exp(jnp.where(tri, gc[..., :, None] - gc[..., None, :], -jnp.inf))
    kb = kc * bc[..., None]
    a_low = jnp.where(strict, jnp.einsum('nhid,nhjd->nhij', kb, kc) * decay, 0.0)
    rhs = jnp.concatenate([vc * bc[..., None], kb * jnp.exp(gc)[..., None]], axis=-1)
    sol = lax.linalg.triangular_solve(a_low + jnp.eye(c, dtype=F32), rhs, left_side=True,
                                      lower=True, unit_diagonal=True)
    u, w = sol[..., :DELTA_DV], sol[..., DELTA_DV:]
    qk = jnp.where(tri, jnp.einsum('nhid,nhjd->nhij', qc, kc) * decay, 0.0)
    g_last = gc[..., -1:]
    wq = jnp.concatenate([w, qc * jnp.exp(gc)[..., None]], axis=-2).astype(BF16)
    kt = jnp.swapaxes(kc * jnp.exp(g_last - gc)[..., None], -1, -2).astype(BF16)
    e_last = jnp.broadcast_to(jnp.exp(g_last)[..., None], (n, h, 1, DELTA_DV))

    def per_chunk(*blk):
        return pl.BlockSpec((None,) + blk, lambda i: (i,) + (0,) * len(blk))

    d_out, s_fin = pl.pallas_call(
        _delta_scan_kernel,
        grid=(n,),
        in_specs=[per_chunk(h, 2 * c, DELTA_DK), per_chunk(h, DELTA_DK, c), per_chunk(h, c, DELTA_DV),
                  per_chunk(h, c, c), per_chunk(h, 1, DELTA_DV),
                  pl.BlockSpec((c, h * DELTA_DV), lambda i: (i, 0)),
                  pl.BlockSpec((1, DELTA_DV), lambda i: (0, 0))],
        out_specs=[pl.BlockSpec((c, h * DELTA_DV), lambda i: (i, 0)),
                   pl.BlockSpec((h, DELTA_DK, DELTA_DV), lambda i: (0, 0, 0))],
        out_shape=[jax.ShapeDtypeStruct((t, h * DELTA_DV), BF16),
                   jax.ShapeDtypeStruct((h, DELTA_DK, DELTA_DV), F32)],
        scratch_shapes=[pltpu.VMEM((h, DELTA_DK, DELTA_DV), F32)],
        compiler_params=pltpu.CompilerParams(dimension_semantics=("arbitrary",),
                                             vmem_limit_bytes=V7X_VMEM_LIMIT),
        name="delta_scan",
    )(wq, kt, u, qk.astype(BF16), e_last, p_gate, norm_g.reshape(1, DELTA_DV))
    return d_out, s_fin, new_buf


def routed_experts(xt, expert_idx, gate, w_gate, w_up, w_down):
    n, d = xt.shape
    k = expert_idx.shape[1]
    n_exp = w_gate.shape[0]
    blk = EXPERT_BLOCK
    flat_e = expert_idx.reshape(-1)
    order = jnp.argsort(flat_e)
    sorted_e = flat_e[order]
    counts = jnp.bincount(flat_e, length=n_exp)
    padded = (counts + blk - 1) // blk * blk
    pad_end = jnp.cumsum(padded)
    pad_start = pad_end - padded
    start = jnp.cumsum(counts) - counts
    dest = (pad_start[sorted_e] + jnp.arange(n * k) - start[sorted_e]).astype(jnp.int32)
    n_blocks = -(-(n * k) // blk) + n_exp
    rows = n_blocks * blk
    row_token = jnp.full((rows,), n, jnp.int32).at[dest].set((order // k).astype(jnp.int32))
    row_gate = jnp.zeros((rows,), F32).at[dest].set(gate.reshape(-1)[order])
    block_expert = jnp.minimum(jnp.searchsorted(pad_end, jnp.arange(n_blocks) * blk, side='right'),
                               n_exp - 1).astype(jnp.int32)
    n_used = (pad_end[-1:] // blk).astype(jnp.int32)
    x_rows = jnp.concatenate([xt.astype(BF16), jnp.zeros((1, d), BF16)], axis=0)[row_token]
    h_rows = grouped_swiglu_up(x_rows, block_expert, n_used, w_gate, w_up, blk)
    y_rows = grouped_down(h_rows, row_gate[:, None], block_expert, n_used, w_down, blk)
    pos = jnp.zeros((n * k,), jnp.int32).at[order].set(dest)
    return jnp.sum(y_rows[pos].reshape(n, k, d), axis=1)


def moe(xt, w_router, router_bias, w_exp_gate, w_exp_up, w_exp_down, w_sh_gate, w_sh_up, w_sh_down):
    n = xt.shape[0]
    scores = jax.nn.sigmoid(jnp.matmul(xt, w_router, preferred_element_type=F32))
    biased = scores + router_bias
    per_group = N_EXPERTS // N_GROUPS
    group_score = lax.top_k(biased.reshape(n, N_GROUPS, per_group), 2)[0].sum(-1)
    _, top_groups = lax.top_k(group_score, TOPK_GROUPS)
    group_keep = jnp.any(top_groups[:, :, None] == jnp.arange(N_GROUPS)[None, None, :], axis=1)
    biased = jnp.where(jnp.repeat(group_keep, per_group, axis=1), biased, -jnp.inf)
    _, expert_idx = lax.top_k(biased, TOP_K)
    gate = jnp.take_along_axis(scores, expert_idx, axis=1)
    gate = gate / jnp.sum(gate, -1, keepdims=True) * ROUTED_SCALE
    routed = routed_experts(xt, expert_idx, gate, w_exp_gate, w_exp_up, w_exp_down)
    xb = xt.astype(BF16)
    sg = matmul(xb, w_sh_gate)
    su = matmul(xb, w_sh_up)
    shared = matmul((jax.nn.silu(sg) * su).astype(BF16), w_sh_down)
    return routed + shared


def kernel(x_prompt, x_sample, cache_k, cache_v, cache_idx_k, page_table, state_delta, state_conv, w_in, w_conv, a_log, dt_bias, delta_norm_g, w_branch_attn, w_branch_delta, w_out, ln1_g, ln1_b, w_router, router_bias, w_exp_gate, w_exp_up, w_exp_down, w_sh_gate, w_sh_up, w_sh_down, ln2_g, ln2_b):
    b, s, d = x_prompt.shape
    db, t = x_sample.shape[:2]
    assert DEPTH == 1 and b == 1
    past = page_table.shape[1] * cache_k.shape[2]
    n_p, n_s = b * s, db * t
    (l_in, l_conv, l_alog, l_dtb, l_ng, l_ba, l_bd, l_out, l_g1, l_b1,
     l_r, l_rb, l_eg, l_eu, l_ed, l_sg, l_su, l_sd, l_g2, l_b2) = [a[0] for a in (
         w_in, w_conv, a_log, dt_bias, delta_norm_g, w_branch_attn, w_branch_delta, w_out,
         ln1_g, ln1_b, w_router, router_bias, w_exp_gate, w_exp_up, w_exp_down,
         w_sh_gate, w_sh_up, w_sh_down, ln2_g, ln2_b)]

    x_all = jnp.concatenate([x_prompt.reshape(n_p, d), x_sample.reshape(n_s, d)], axis=0)
    xb = x_all.astype(BF16)
    p_attn = matmul_cols(xb, l_in, 0, ATTN_COLS)
    p_delta = matmul_cols(xb, l_in, IN_SPLITS[5], DELTA_COLS)
    p_gate = matmul_cols(xb, l_in, IN_SPLITS[8], GATE_COLS)
    ap = attention_inputs(p_attn[:n_p].reshape(b, s, -1), jnp.arange(s))
    at = attention_inputs(p_attn[n_p:].reshape(db, t, -1), past + jnp.arange(t))

    a_p = prompt_sparse_attention(ap.q, ap.k, ap.v, ap.iq, ap.iw, ap.ik)
    a_s = sample_sparse_attention(at.q, at.k, at.v, at.iq, at.iw, at.ik,
                                  cache_k[0], cache_v[0], cache_idx_k[0], page_table)
    d_p, sd_p, sc_p = prompt_delta_branch(p_delta, p_gate, n_p, l_conv, l_alog, l_dtb, l_ng)
    pd_s = p_delta[n_p:].reshape(db, t, -1)
    nh = N_DELTA_HEADS
    d_s, sd_s, sc_s = gated_delta_branch(pd_s[..., :CONV_CH], pd_s[..., CONV_CH:CONV_CH + nh], pd_s[..., CONV_CH + nh:],
                                         p_gate[n_p:, :DELTA_V_WIDTH].reshape(db, t, -1), state_conv[0], state_delta[0],
                                         l_conv, l_alog, l_dtb, l_ng)

    a_all = jnp.concatenate([a_p.reshape(n_p, -1).astype(BF16), a_s.reshape(n_s, -1).astype(BF16)], axis=0)
    d_all = jnp.concatenate([d_p, d_s.reshape(n_s, -1).astype(BF16)], axis=0)
    g_a, g_d = jnp.split(jax.nn.sigmoid(p_gate[:, DELTA_V_WIDTH:]), 2, axis=-1)
    merged = g_a * matmul(a_all, l_ba) + g_d * matmul(d_all, l_bd)
    h = layer_norm(ALPHA * x_all + matmul(merged.astype(BF16), l_out), l_g1, l_b1)
    y = layer_norm(ALPHA * h + moe(h, l_r, l_rb, l_eg, l_eu, l_ed, l_sg, l_su, l_sd), l_g2, l_b2)
    return (y[:n_p].reshape(b, s, d), y[n_p:].reshape(db, t, d),
            ap.k[None], ap.v[None], ap.ik[None], sd_p[None, None], sc_p[None, None],
            at.k[None], at.v[None], at.ik[None], sd_s[None], sc_s[None])
```

```python
import collections
import functools

import jax
import jax.numpy as jnp
import numpy as np
from jax import lax
from jax.experimental import pallas as pl
from jax.experimental.pallas import tpu as pltpu

D_MODEL = 4096
DEPTH = 1
N_HEADS = 16
N_KV_HEADS = 4
HEAD_DIM = 128
ROPE_THETA = 500000.0
N_IDX_HEADS = 16
IDX_DIM = 128
IDX_TOPK = 256
Q_BLOCK = 128
N_DELTA_HEADS = 16
DELTA_DK = 128
DELTA_DV = 128
CONV_WIDTH = 4
DELTA_CHUNK = 64
N_EXPERTS = 64
TOP_K = 8
N_GROUPS = 8
TOPK_GROUPS = 4
EXPERT_DIM = 1024
ROUTED_SCALE = 2.5
EXPERT_BLOCK = 256
ROW_GROUPS = 2
ALPHA = (2 * DEPTH) ** 0.25
LN_EPS = 1e-5
ATTN_WIDTH = N_HEADS * HEAD_DIM
KV_WIDTH = N_KV_HEADS * HEAD_DIM
IDXQ_WIDTH = N_IDX_HEADS * IDX_DIM
DELTA_QK_WIDTH = N_DELTA_HEADS * DELTA_DK
DELTA_V_WIDTH = N_DELTA_HEADS * DELTA_DV
CONV_CH = 2 * DELTA_QK_WIDTH + DELTA_V_WIDTH
IN_SIZES = (ATTN_WIDTH, KV_WIDTH, KV_WIDTH, IDXQ_WIDTH, IDX_DIM, N_IDX_HEADS, CONV_CH,
            N_DELTA_HEADS, N_DELTA_HEADS, DELTA_V_WIDTH, 2 * D_MODEL)
IN_SPLITS = tuple(int(s) for s in np.cumsum(IN_SIZES)[:-1])

V7X_VMEM_LIMIT = 56 * 1024 * 1024
BF16 = jnp.bfloat16
F32 = jnp.float32


def _mm_kernel(x_ref, w_ref, o_ref, wbf_ref):
    @pl.when(pl.program_id(1) == 0)
    def _():
        wbf_ref[...] = w_ref[...].astype(BF16)

    o_ref[...] = jnp.dot(x_ref[...], wbf_ref[...], preferred_element_type=F32).astype(o_ref.dtype)


def _row_tile(m):
    for tm in (1024, 1088, 512, 256, 128):
        if m % tm == 0:
            return tm
    raise ValueError(f"no row tile for {m} rows")


def matmul(x, w, out_dtype=F32, tn=512):
    m, k = x.shape
    n = w.shape[1]
    tm = _row_tile(m)
    tn = min(tn, n)
    return pl.pallas_call(
        _mm_kernel,
        grid=(pl.cdiv(n, tn), m // tm),
        in_specs=[pl.BlockSpec((tm, k), lambda j, i: (i, 0)),
                  pl.BlockSpec((k, tn), lambda j, i: (0, j))],
        out_specs=pl.BlockSpec((tm, tn), lambda j, i: (i, j)),
        out_shape=jax.ShapeDtypeStruct((m, n), out_dtype),
        scratch_shapes=[pltpu.VMEM((k, tn), BF16)],
        compiler_params=pltpu.CompilerParams(dimension_semantics=("arbitrary", "arbitrary"),
                                             vmem_limit_bytes=V7X_VMEM_LIMIT),
        name="dense_matmul",
    )(x, w)


def _merge_kernel(a_ref, d_ref, ga_ref, gd_ref, wa_ref, wd_ref, o_ref, wa_bf, wd_bf):
    @pl.when(pl.program_id(1) == 0)
    def _():
        wa_bf[...] = wa_ref[...].astype(BF16)
        wd_bf[...] = wd_ref[...].astype(BF16)

    pa = jnp.dot(a_ref[...], wa_bf[...], preferred_element_type=F32)
    pd = jnp.dot(d_ref[...], wd_bf[...], preferred_element_type=F32)
    o_ref[...] = (jax.nn.sigmoid(ga_ref[...]) * pa + jax.nn.sigmoid(gd_ref[...]) * pd).astype(o_ref.dtype)


def merge_branches(a, d, p_gate, gate_col0, w_a, w_d, tn=512):
    m, k = a.shape
    n = w_a.shape[1]
    tm = _row_tile(m)
    assert gate_col0 % tn == 0 and n % tn == 0 and w_d.shape == w_a.shape and d.shape == a.shape
    ga0, gd0 = gate_col0 // tn, (gate_col0 + n) // tn
    return pl.pallas_call(
        _merge_kernel,
        grid=(n // tn, m // tm),
        in_specs=[pl.BlockSpec((tm, k), lambda j, i: (i, 0)),
                  pl.BlockSpec((tm, k), lambda j, i: (i, 0)),
                  pl.BlockSpec((tm, tn), lambda j, i: (i, ga0 + j)),
                  pl.BlockSpec((tm, tn), lambda j, i: (i, gd0 + j)),
                  pl.BlockSpec((k, tn), lambda j, i: (0, j)),
                  pl.BlockSpec((k, tn), lambda j, i: (0, j))],
        out_specs=pl.BlockSpec((tm, tn), lambda j, i: (i, j)),
        out_shape=jax.ShapeDtypeStruct((m, n), BF16),
        scratch_shapes=[pltpu.VMEM((k, tn), BF16), pltpu.VMEM((k, tn), BF16)],
        compiler_params=pltpu.CompilerParams(dimension_semantics=("arbitrary", "arbitrary"),
                                             vmem_limit_bytes=V7X_VMEM_LIMIT),
        name="merge_branches",
    )(a, d, p_gate, p_gate, w_a, w_d)


LANES = 128


def _mm_cols_kernel(x_ref, w_ref, o_ref, wbf_ref, *, off):
    @pl.when(pl.program_id(1) == 0)
    def _():
        k, tn = wbf_ref.shape
        rows = min(k, 512)
        for r in range(0, k, rows):
            wbf_ref[r:r + rows, :] = w_ref[r:r + rows, off:off + tn].astype(BF16)

    o_ref[...] = jnp.dot(x_ref[...], wbf_ref[...], preferred_element_type=F32).astype(o_ref.dtype)


def matmul_cols(x, w, col0, ncols, out_dtype=F32, tn=512):
    m, k = x.shape
    tm = _row_tile(m)
    tn = min(tn, ncols)
    off = col0 % LANES
    base = col0 - off
    assert k % min(k, 512) == 0 and tn % LANES == 0
    w_spec = pl.BlockSpec((pl.Element(k), pl.Element(tn + LANES, (0, tn + LANES))),
                          lambda j, i: (0, pl.multiple_of(base + j * tn, LANES)))
    return pl.pallas_call(
        functools.partial(_mm_cols_kernel, off=off),
        grid=(pl.cdiv(ncols, tn), m // tm),
        in_specs=[pl.BlockSpec((tm, k), lambda j, i: (i, 0)), w_spec],
        out_specs=pl.BlockSpec((tm, tn), lambda j, i: (i, j)),
        out_shape=jax.ShapeDtypeStruct((m, ncols), out_dtype),
        scratch_shapes=[pltpu.VMEM((k, tn), BF16)],
        compiler_params=pltpu.CompilerParams(dimension_semantics=("arbitrary", "arbitrary"),
                                             vmem_limit_bytes=V7X_VMEM_LIMIT),
        name="dense_matmul_cols",
    )(x, w)


def _expert_changed(be_ref, i):
    prev = be_ref[jnp.maximum(i - 1, 0)]
    return jnp.logical_or(i == 0, be_ref[i] != prev)


def _moe_up_kernel(be_ref, nu_ref, x_ref, wg_ref, wu_ref, h_ref, wg_bf, wu_bf):
    i = pl.program_id(1)

    @pl.when(i < nu_ref[0])
    def _():
        @pl.when(_expert_changed(be_ref, i))
        def _():
            wg_bf[...] = wg_ref[...].astype(BF16)
            wu_bf[...] = wu_ref[...].astype(BF16)

        x = x_ref[...]
        g = jnp.dot(x, wg_bf[...], preferred_element_type=F32)
        u = jnp.dot(x, wu_bf[...], preferred_element_type=F32)
        h_ref[...] = (g * jax.nn.sigmoid(g) * u).astype(h_ref.dtype)


def _moe_down_kernel(be_ref, nu_ref, h_ref, gate_ref, wd_ref, y_ref, wd_bf):
    i = pl.program_id(1)

    @pl.when(i < nu_ref[0])
    def _():
        @pl.when(_expert_changed(be_ref, i))
        def _():
            wd_bf[...] = wd_ref[...].astype(BF16)

        y = jnp.dot(h_ref[...], wd_bf[...], preferred_element_type=F32)
        y_ref[...] = (y * gate_ref[...]).astype(y_ref.dtype)


def grouped_swiglu_up(x_rows, block_expert, n_used, w_gate, w_up, blk, tf=512):
    rows, d = x_rows.shape
    f = w_gate.shape[2]
    tf = min(tf, f)
    n_blocks = rows // blk
    grid_spec = pltpu.PrefetchScalarGridSpec(
        num_scalar_prefetch=2,
        grid=(f // tf, n_blocks),
        in_specs=[pl.BlockSpec((blk, d), lambda j, i, be, nu: (i, 0)),
                  pl.BlockSpec((None, d, tf), lambda j, i, be, nu: (be[i], 0, j)),
                  pl.BlockSpec((None, d, tf), lambda j, i, be, nu: (be[i], 0, j))],
        out_specs=pl.BlockSpec((blk, tf), lambda j, i, be, nu: (i, j)),
        scratch_shapes=[pltpu.VMEM((d, tf), BF16), pltpu.VMEM((d, tf), BF16)],
    )
    return pl.pallas_call(
        _moe_up_kernel, grid_spec=grid_spec,
        out_shape=jax.ShapeDtypeStruct((rows, f), BF16),
        compiler_params=pltpu.CompilerParams(dimension_semantics=("arbitrary", "arbitrary"),
                                             vmem_limit_bytes=V7X_VMEM_LIMIT),
        name="moe_up",
    )(block_expert, n_used, x_rows, w_gate, w_up)


def grouped_down(h_rows, row_gate, block_expert, n_used, w_down, blk, tn=2048):
    rows, f = h_rows.shape
    d = w_down.shape[2]
    tn = min(tn, d)
    n_blocks = rows // blk
    grid_spec = pltpu.PrefetchScalarGridSpec(
        num_scalar_prefetch=2,
        grid=(d // tn, n_blocks),
        in_specs=[pl.BlockSpec((blk, f), lambda j, i, be, nu: (i, 0)),
                  pl.BlockSpec((blk, 1), lambda j, i, be, nu: (i, 0)),
                  pl.BlockSpec((None, f, tn), lambda j, i, be, nu: (be[i], 0, j))],
        out_specs=pl.BlockSpec((blk, tn), lambda j, i, be, nu: (i, j)),
        scratch_shapes=[pltpu.VMEM((f, tn), BF16)],
    )
    return pl.pallas_call(
        _moe_down_kernel, grid_spec=grid_spec,
        out_shape=jax.ShapeDtypeStruct((rows, d), F32),
        compiler_params=pltpu.CompilerParams(dimension_semantics=("arbitrary", "arbitrary"),
                                             vmem_limit_bytes=V7X_VMEM_LIMIT),
        name="moe_down",
    )(block_expert, n_used, h_rows, row_gate, w_down)


def layer_norm(x, g, b):
    xc = x - jnp.mean(x, -1, keepdims=True)
    var = jnp.mean(xc * xc, -1, keepdims=True)
    return xc * lax.rsqrt(var + LN_EPS) * g + b


def l2norm(x):
    return x * lax.rsqrt(jnp.sum(x * x, -1, keepdims=True) + 1e-6)


def partial_rope(x, pos):
    rot = x.shape[-1] // 4
    half = rot // 2
    inv_freq = ROPE_THETA ** (-jnp.arange(half, dtype=F32) / half)
    ang = pos.astype(F32)[:, None] * inv_freq[None, :]
    cos = jnp.cos(ang)[:, None, :]
    sin = jnp.sin(ang)[:, None, :]
    x1, x2, rest = x[..., :half], x[..., half:rot], x[..., rot:]
    return jnp.concatenate([x1 * cos - x2 * sin, x2 * cos + x1 * sin, rest], axis=-1)


ATTN_COLS = IN_SPLITS[5]
DELTA_COLS = IN_SPLITS[8] - IN_SPLITS[5]
GATE_COLS = int(sum(IN_SIZES)) - IN_SPLITS[8]
AttnInputs = collections.namedtuple("AttnInputs", "q k v iq ik iw")


def attention_inputs(p, pos):
    b, t, _ = p.shape
    q, k, v, iq, ik, iw = jnp.split(p, IN_SPLITS[:5], axis=-1)
    q = partial_rope(q.reshape(b, t, N_HEADS, HEAD_DIM), pos)
    k = partial_rope(k.reshape(b, t, N_KV_HEADS, HEAD_DIM), pos)
    v = v.reshape(b, t, N_KV_HEADS, HEAD_DIM)
    iq = partial_rope(iq.reshape(b, t, N_IDX_HEADS, IDX_DIM), pos)
    ik = partial_rope(ik[:, :, None, :], pos)[:, :, 0, :]
    iw = iw * (N_IDX_HEADS ** -0.5 * IDX_DIM ** -0.5)
    return AttnInputs(q, k, v, iq, ik, iw)


KEY_CHUNK = 512
MASKED = -1e30
SOFTMAX_LOG2_SCALE = HEAD_DIM ** -0.5 * float(np.log2(np.e))
INT32_MIN = -2 ** 31
INT32_MAX = 2 ** 31 - 1
NEG_INF_KEY = int(np.int32(np.uint32(0xFF800000) ^ np.uint32(0x7FFFFFFF)))


def _sort_key(x):
    bits = lax.bitcast_convert_type(x, jnp.int32)
    return bits ^ (jnp.right_shift(bits, 31) & jnp.int32(INT32_MAX))


def _lane_tile(x, n):
    return x if n == 1 else jnp.concatenate([x] * n, axis=1)


def _count_rows(keys_ref, nkc, preds):
    _, rows, kc = keys_ref.shape

    def body(c, parts):
        keys = keys_ref[c]
        out = []
        for pred, part in zip(preds, parts):
            hit = pred(keys, c)
            for j in range(kc // 128):
                part = part + hit[:, j * 128:(j + 1) * 128]
            out.append(part)
        return tuple(out)

    parts = lax.fori_loop(0, nkc, body, tuple(jnp.zeros((rows, 128), F32) for _ in preds))
    return [jnp.broadcast_to(jnp.sum(part, axis=1, keepdims=True), (rows, 128)) for part in parts]


def _topk_selection(keys_ref, nkc, n_sel, radix_bits):
    _, rows, kc = keys_ref.shape
    reps = kc // 128
    col = lax.broadcasted_iota(jnp.int32, (rows, kc), 1)

    def thr_digit(it, carry):
        thr, cnt_thr = carry
        shift = 32 - radix_bits * (it + 1)
        cands = [thr + jnp.left_shift(jnp.int32(j), shift) for j in range(1, 2 ** radix_bits)]
        cands_w = [_lane_tile(cand, reps) for cand in cands]
        cnts = _count_rows(keys_ref, nkc, [lambda keys, c, cw=cw: jnp.where(keys >= cw, 1.0, 0.0) for cw in cands_w])
        for cand, cnt in zip(cands, cnts):
            ok = cnt >= n_sel
            thr = jnp.where(ok, cand, thr)
            cnt_thr = jnp.where(ok, cnt, cnt_thr)
        return thr, cnt_thr

    thr0 = jnp.full((rows, 128), INT32_MIN, jnp.int32)
    cnt0 = jnp.broadcast_to(jnp.asarray(nkc * kc).astype(F32), (rows, 128))
    thr, cnt_thr = lax.fori_loop(0, 32 // radix_bits, thr_digit, (thr0, cnt0))

    short = thr == NEG_INF_KEY
    thr_w = _lane_tile(thr, reps)
    cnt_gt, = _count_rows(keys_ref, nkc, [lambda keys, c: jnp.where(keys > thr_w, 1.0, 0.0)])
    need = n_sel - cnt_gt
    tied = jnp.logical_and(cnt_thr > n_sel, jnp.logical_not(short))

    def tie_limit():
        def idx_bit(it, x):
            cand = x + jnp.left_shift(jnp.int32(1), 30 - it)
            cand_w = _lane_tile(cand, reps)
            cnt, = _count_rows(keys_ref, nkc, [lambda keys, c: jnp.where(
                keys == thr_w, jnp.where(c * kc + col < cand_w, 1.0, 0.0), 0.0)])
            return jnp.where(cnt < need, cand, x)
        return lax.fori_loop(0, 31, idx_bit, jnp.zeros((rows, 128), jnp.int32))

    any_tied = jnp.max(jnp.where(tied, 1.0, 0.0)) > 0.0
    lim = lax.cond(any_tied, tie_limit, lambda: jnp.full((rows, 128), INT32_MAX, jnp.int32))
    lim = jnp.where(short, -1, jnp.where(tied, lim, INT32_MAX))
    return thr, lim


def _selected(keys, index, thr_w, lim_w, yes, no):
    keep_tie = jnp.where(index <= lim_w, yes, no)
    return jnp.where(keys > thr_w, yes, jnp.where(keys == thr_w, keep_tie, no))


def _dsa_prompt_kernel(iq_ref, iw_ref, ikt_ref, q_ref, kt_ref, v_ref, o_ref,
                       keys_ref, m_ref, l_ref, acc_ref, *, n_sel):
    qb, kc = Q_BLOCK, KEY_CHUNK
    reps = kc // 128
    grp = N_HEADS // N_KV_HEADS
    i = pl.program_id(0)
    nkc = (i * qb + qb + kc - 1) // kc
    qpos = i * qb + lax.broadcasted_iota(jnp.int32, (qb, kc), 0)
    col = lax.broadcasted_iota(jnp.int32, (qb, kc), 1)

    def score_chunk(c, carry):
        ikc = ikt_ref[c]
        acc = jnp.zeros((qb, kc), F32)
        for h in range(N_IDX_HEADS):
            s = jnp.dot(iq_ref[h], ikc, preferred_element_type=F32)
            acc = acc + jnp.maximum(s, 0.0) * iw_ref[:, h:h + 1]
        acc = jnp.where(c * kc + col <= qpos, acc, -jnp.inf)
        keys_ref[c] = _sort_key(acc)
        return carry

    lax.fori_loop(0, nkc, score_chunk, 0)

    thr, lim = _topk_selection(keys_ref, nkc, n_sel, radix_bits=1)
    thr_w, lim_w = _lane_tile(thr, reps), _lane_tile(lim, reps)

    m_ref[...] = jnp.full(m_ref.shape, MASKED, F32)
    l_ref[...] = jnp.zeros(l_ref.shape, F32)
    acc_ref[...] = jnp.zeros(acc_ref.shape, F32)
    rows = grp * qb

    def attend_chunk(c, carry):
        bias = _selected(keys_ref[c], c * kc + col, thr_w, lim_w, 0.0, MASKED)
        for n in range(N_KV_HEADS):
            r = pl.ds(n * rows, rows)
            qn = q_ref[n * grp:(n + 1) * grp].reshape(rows, HEAD_DIM)
            s = jnp.dot(qn, kt_ref[n, c], preferred_element_type=F32) * SOFTMAX_LOG2_SCALE
            s = (s.reshape(grp, qb, kc) + bias[None]).reshape(rows, kc)
            m_prev = m_ref[r, :]
            m_new = jnp.maximum(m_prev, jnp.max(s, axis=1, keepdims=True))
            alpha = jnp.exp2(m_prev - m_new)
            p = jnp.exp2(s - _lane_tile(m_new, reps))
            l_ref[r, :] = alpha * l_ref[r, :] + jnp.sum(p, axis=1, keepdims=True)
            acc_ref[r, :] = alpha * acc_ref[r, :] + jnp.dot(p.astype(BF16), v_ref[n, c],
                                                           preferred_element_type=F32)
            m_ref[r, :] = m_new
        return carry

    lax.fori_loop(0, nkc, attend_chunk, 0)
    for h in range(N_HEADS):
        r = pl.ds(h * qb, qb)
        o_ref[:, h * HEAD_DIM:(h + 1) * HEAD_DIM] = (acc_ref[r, :] / l_ref[r, :]).astype(o_ref.dtype)


def prompt_sparse_attention(q, k, v, iq, iw, ik):
    b, s = q.shape[:2]
    assert b == 1 and s % KEY_CHUNK == 0 and KEY_CHUNK % Q_BLOCK == 0 and KEY_CHUNK >= IDX_TOPK
    n_sel = min(IDX_TOPK, s // 4)
    nb, nc, kc = s // Q_BLOCK, s // KEY_CHUNK, KEY_CHUNK

    def head_major(a):
        return a.astype(BF16).reshape(nb, Q_BLOCK, a.shape[2], a.shape[3]).transpose(0, 2, 1, 3)

    ikt = ik.astype(BF16).reshape(nc, kc, IDX_DIM).transpose(0, 2, 1)
    kt = k.astype(BF16).reshape(nc, kc, N_KV_HEADS, HEAD_DIM).transpose(2, 0, 3, 1)
    vc = v.astype(BF16).reshape(nc, kc, N_KV_HEADS, HEAD_DIM).transpose(2, 0, 1, 3)
    resident = dict(pipeline_mode=pl.Buffered(1))
    out = pl.pallas_call(
        functools.partial(_dsa_prompt_kernel, n_sel=n_sel),
        grid=(nb,),
        in_specs=[pl.BlockSpec((None, N_IDX_HEADS, Q_BLOCK, IDX_DIM), lambda i: (i, 0, 0, 0)),
                  pl.BlockSpec((Q_BLOCK, N_IDX_HEADS), lambda i: (i, 0)),
                  pl.BlockSpec((nc, IDX_DIM, kc), lambda i: (0, 0, 0), **resident),
                  pl.BlockSpec((None, N_HEADS, Q_BLOCK, HEAD_DIM), lambda i: (i, 0, 0, 0)),
                  pl.BlockSpec((N_KV_HEADS, nc, HEAD_DIM, kc), lambda i: (0, 0, 0, 0), **resident),
                  pl.BlockSpec((N_KV_HEADS, nc, kc, HEAD_DIM), lambda i: (0, 0, 0, 0), **resident)],
        out_specs=pl.BlockSpec((Q_BLOCK, N_HEADS * HEAD_DIM), lambda i: (i, 0)),
        out_shape=jax.ShapeDtypeStruct((s, N_HEADS * HEAD_DIM), BF16),
        scratch_shapes=[pltpu.VMEM((nc, Q_BLOCK, kc), jnp.int32),
                        pltpu.VMEM((N_HEADS * Q_BLOCK, 128), F32),
                        pltpu.VMEM((N_HEADS * Q_BLOCK, 128), F32),
                        pltpu.VMEM((N_HEADS * Q_BLOCK, HEAD_DIM), F32)],
        compiler_params=pltpu.CompilerParams(dimension_semantics=("arbitrary",),
                                             vmem_limit_bytes=V7X_VMEM_LIMIT),
        name="dsa_prompt",
    )(head_major(iq), iw.reshape(s, N_IDX_HEADS), ikt, head_major(q), kt, vc)
    return out.reshape(b, s, N_HEADS * HEAD_DIM)


T_PAD = 8


def _dsa_sample_kernel(pt_ref, iq_ref, iw_ref, q_ref, iknew_ref, knew_ref, vnew_ref, *rest, n_sel, n_pages):
    del pt_ref
    ik_pages, k_pages, v_pages = rest[:n_pages], rest[n_pages:2 * n_pages], rest[2 * n_pages:3 * n_pages]
    o_ref, keys_ref, s_ref = rest[3 * n_pages:]
    page = ik_pages[0].shape[0]
    kv_rows = k_pages[0].shape[0]
    rows = N_HEADS * T_PAD
    nkc = n_pages + 1
    nt = (((1,), (1,)), ((), ()))
    row_t = lax.broadcasted_iota(jnp.int32, (T_PAD, page), 0)
    col = lax.broadcasted_iota(jnp.int32, (T_PAD, page), 1)

    for c in range(nkc):
        ikc = ik_pages[c][...].astype(BF16) if c < n_pages else iknew_ref[...]
        s = lax.dot_general(iq_ref[...], ikc, nt, preferred_element_type=F32)
        x = jnp.maximum(s, 0.0) * iw_ref[...]
        sc = jnp.sum(x.reshape(T_PAD, N_IDX_HEADS, page), axis=1)
        if c == n_pages:
            sc = jnp.where(col <= row_t, sc, -jnp.inf)
        keys_ref[c] = _sort_key(sc)

    thr, lim = _topk_selection(keys_ref, nkc, n_sel, radix_bits=4)

    scale = SOFTMAX_LOG2_SCALE
    spread = (lax.broadcasted_iota(jnp.int32, (page, kv_rows), 1) // N_KV_HEADS
              == lax.broadcasted_iota(jnp.int32, (page, kv_rows), 0)).astype(BF16)
    own_head = (lax.broadcasted_iota(jnp.int32, (rows, kv_rows), 0) // (rows // N_KV_HEADS)
                == lax.broadcasted_iota(jnp.int32, (rows, kv_rows), 1) % N_KV_HEADS).astype(F32)
    for c in range(nkc):
        kc_ = k_pages[c][...].astype(BF16) if c < n_pages else knew_ref[...]
        s = lax.dot_general(q_ref[...], kc_, nt, preferred_element_type=F32) * scale
        sel = _selected(keys_ref[c], c * page + col, thr, lim, 1.0, 0.0).astype(BF16)
        sel = jnp.dot(sel, spread, preferred_element_type=F32)
        keep = jnp.broadcast_to(sel[None], (N_HEADS, T_PAD, kv_rows)).reshape(rows, kv_rows) * own_head
        s_ref[:, c * kv_rows:(c + 1) * kv_rows] = jnp.where(keep > 0.5, s, MASKED)
    s = s_ref[...]
    p = jnp.exp2(s - jnp.max(s, axis=1, keepdims=True))
    inv_l = 1.0 / jnp.sum(p, axis=1, keepdims=True)
    p = p.astype(BF16)
    o = jnp.zeros((rows, HEAD_DIM), F32)
    for c in range(nkc):
        vc_ = v_pages[c][...].astype(BF16) if c < n_pages else vnew_ref[...]
        o = o + jnp.dot(p[:, c * kv_rows:(c + 1) * kv_rows], vc_, preferred_element_type=F32)
    o_ref[...] = o * inv_l


def sample_sparse_attention(q, k_new, v_new, iq, iw, ik_new, cache_k, cache_v, cache_idx_k, page_table):
    db, t = q.shape[:2]
    n_pool, page = cache_k.shape[:2]
    n_pages = page_table.shape[1]
    n_keys = n_pages * page + t
    n_sel = min(IDX_TOPK, n_keys // 4)
    grp = N_HEADS // N_KV_HEADS
    assert t <= T_PAD <= page and page == 128 and HEAD_DIM == 128 and IDX_DIM == 128

    def pad_t(a, to):
        return jnp.pad(a, [(0, 0), (0, to - a.shape[1])] + [(0, 0)] * (a.ndim - 2))

    kv_rows = page * N_KV_HEADS
    rows = N_HEADS * T_PAD
    iq_rows = pad_t(iq, T_PAD).astype(BF16).reshape(db, T_PAD * N_IDX_HEADS, IDX_DIM)
    iw_rows = pad_t(iw, T_PAD).reshape(db, T_PAD * N_IDX_HEADS, 1)
    q_rows = pad_t(q, T_PAD).astype(BF16).transpose(0, 2, 1, 3).reshape(db, rows, HEAD_DIM)
    iknew = pad_t(ik_new, page).astype(BF16)
    knew = pad_t(k_new, page).astype(BF16).reshape(db, kv_rows, HEAD_DIM)
    vnew = pad_t(v_new, page).astype(BF16).reshape(db, kv_rows, HEAD_DIM)
    ck = cache_k.reshape(n_pool, kv_rows, HEAD_DIM)
    cv = cache_v.reshape(n_pool, kv_rows, HEAD_DIM)

    def per_batch(*blk):
        return pl.BlockSpec((None,) + blk, lambda b, pt: (b,) + (0,) * len(blk))

    def paged(nrows):
        return [pl.BlockSpec((None, nrows, HEAD_DIM), lambda b, pt, j=j: (pt[b, j], 0, 0)) for j in range(n_pages)]

    grid_spec = pltpu.PrefetchScalarGridSpec(
        num_scalar_prefetch=1,
        grid=(db,),
        in_specs=[per_batch(T_PAD * N_IDX_HEADS, IDX_DIM), per_batch(T_PAD * N_IDX_HEADS, 1),
                  per_batch(rows, HEAD_DIM), per_batch(page, IDX_DIM), per_batch(kv_rows, HEAD_DIM),
                  per_batch(kv_rows, HEAD_DIM)]
        + paged(page) + paged(kv_rows) + paged(kv_rows),
        out_specs=per_batch(rows, HEAD_DIM),
        scratch_shapes=[pltpu.VMEM((n_pages + 1, T_PAD, page), jnp.int32),
                        pltpu.VMEM((rows, (n_pages + 1) * kv_rows), F32)],
    )
    o = pl.pallas_call(
        functools.partial(_dsa_sample_kernel, n_sel=n_sel, n_pages=n_pages),
        grid_spec=grid_spec,
        out_shape=jax.ShapeDtypeStruct((db, rows, HEAD_DIM), F32),
        compiler_params=pltpu.CompilerParams(dimension_semantics=("arbitrary",),
                                             vmem_limit_bytes=V7X_VMEM_LIMIT),
        name="dsa_sample",
    )(page_table, iq_rows, iw_rows, q_rows, iknew, knew, vnew,
      *([cache_idx_k] * n_pages), *([ck] * n_pages), *([cv] * n_pages))
    o = o.reshape(db, N_KV_HEADS, grp, T_PAD, HEAD_DIM)[:, :, :, :t]
    return o.transpose(0, 3, 1, 2, 4).reshape(db, t, N_HEADS * HEAD_DIM)


def causal_conv(xpad, w_conv):
    c = xpad.shape[-1]
    return lax.conv_general_dilated(xpad, w_conv[:, None, :], window_strides=(1,),
                                    padding='VALID', dimension_numbers=('NWC', 'WIO', 'NWC'),
                                    feature_group_count=c)


def chunk_gated_delta(q, k, v, beta, g, s0):
    b, t, h, dk = q.shape
    dv = v.shape[-1]
    c = min(DELTA_CHUNK, t)
    n = -(-t // c)
    pad = n * c - t

    def chunks(a):
        a = jnp.pad(a, [(0, 0), (0, pad)] + [(0, 0)] * (a.ndim - 2))
        a = a.reshape((b, n, c) + a.shape[2:])
        return jnp.moveaxis(a, (1, 3), (0, 2))

    qc, kc, vc, bc, gc = [chunks(a) for a in (q, k, v, beta, g)]
    gc = jnp.cumsum(gc, axis=-1)
    tri = jnp.tril(jnp.ones((c, c), bool))
    strict = jnp.tril(jnp.ones((c, c), bool), -1)
    decay = jnp.exp(jnp.where(tri, gc[..., :, None] - gc[..., None, :], -jnp.inf))
    kb = kc * bc[..., None]
    a_low = jnp.where(strict, jnp.einsum('nbhid,nbhjd->nbhij', kb, kc) * decay, 0.0)
    rhs = jnp.concatenate([vc * bc[..., None], kb * jnp.exp(gc)[..., None]], axis=-1)
    sol = lax.linalg.triangular_solve(a_low + jnp.eye(c, dtype=F32), rhs, left_side=True,
                                      lower=True, unit_diagonal=True)
    u, w = sol[..., :dv], sol[..., dv:]
    qk = jnp.where(tri, jnp.einsum('nbhid,nbhjd->nbhij', qc, kc) * decay, 0.0)

    def step(state, xs):
        q_i, k_i, u_i, w_i, g_i, qk_i = xs
        v_corr = u_i - jnp.einsum('bhck,bhkv->bhcv', w_i, state)
        o_i = (jnp.einsum('bhck,bhkv->bhcv', q_i * jnp.exp(g_i)[..., None], state)
               + jnp.einsum('bhij,bhjv->bhiv', qk_i, v_corr))
        g_last = g_i[..., -1:]
        state = (state * jnp.exp(g_last)[..., None]
                 + jnp.einsum('bhck,bhcv->bhkv', k_i * jnp.exp(g_last - g_i)[..., None], v_corr))
        return state, o_i

    s_final, o = lax.scan(step, s0, (qc, kc, u, w, gc, qk))
    o = jnp.moveaxis(o, (0, 2), (1, 3)).reshape(b, n * c, h, dv)[:, :t]
    return o, s_final


def gated_delta_branch(qkv, b_d, a_d, z, conv_buf, s0, w_conv, a_log, dt_bias, norm_g):
    b, t, _ = qkv.shape
    xpad = jnp.concatenate([conv_buf, qkv], axis=1)
    new_buf = xpad[:, -(CONV_WIDTH - 1):]
    hc = jax.nn.silu(causal_conv(xpad, w_conv))
    q, k, v = jnp.split(hc, (DELTA_QK_WIDTH, 2 * DELTA_QK_WIDTH), axis=-1)
    q = l2norm(q.reshape(b, t, N_DELTA_HEADS, DELTA_DK)) * DELTA_DK ** -0.5
    k = l2norm(k.reshape(b, t, N_DELTA_HEADS, DELTA_DK))
    v = v.reshape(b, t, N_DELTA_HEADS, DELTA_DV)
    beta = jax.nn.sigmoid(b_d)
    g = -jnp.exp(a_log) * jax.nn.softplus(a_d + dt_bias)
    o, s_new = chunk_gated_delta(q, k, v, beta, g, s0)
    o = o * lax.rsqrt(jnp.mean(o * o, -1, keepdims=True) + 1e-6) * norm_g
    o = o * jax.nn.silu(z.reshape(b, t, N_DELTA_HEADS, DELTA_DV))
    return o.reshape(b, t, DELTA_V_WIDTH), s_new, new_buf


DELTA_TOKEN_BLOCK = 256
SUBLANES = 8


def _delta_pre_kernel(x_ref, halo_ref, w_ref, q_ref, k_ref, v_ref, xe_ref):
    i = pl.program_id(0)
    tb = x_ref.shape[0]
    xe_ref[0:SUBLANES, :] = jnp.where(i == 0, 0.0, halo_ref[...])
    xe_ref[SUBLANES:SUBLANES + tb, :] = x_ref[...]
    outs = (q_ref, k_ref, v_ref)
    for col in range(3 * N_DELTA_HEADS):
        cs = slice(col * DELTA_DK, (col + 1) * DELTA_DK)
        y = sum(xe_ref[SUBLANES - (CONV_WIDTH - 1) + tap:SUBLANES - (CONV_WIDTH - 1) + tap + tb, cs] * w_ref[tap:tap + 1, cs]
                for tap in range(CONV_WIDTH))
        y = y * jax.nn.sigmoid(y)
        if col < 2 * N_DELTA_HEADS:
            y = y * lax.rsqrt(jnp.sum(y * y, axis=1, keepdims=True) + 1e-6)
        if col < N_DELTA_HEADS:
            y = y * DELTA_DK ** -0.5
        for cc in range(tb // DELTA_CHUNK):
            outs[col // N_DELTA_HEADS][cc, col % N_DELTA_HEADS] = y[cc * DELTA_CHUNK:(cc + 1) * DELTA_CHUNK]


def delta_pre(qkv, t, w_conv):
    c = w_conv.shape[1]
    tb = DELTA_TOKEN_BLOCK
    assert t % tb == 0 and tb % DELTA_CHUNK == 0 and DELTA_DK == DELTA_DV
    per_blk = tb // DELTA_CHUNK
    out_spec = pl.BlockSpec((per_blk, N_DELTA_HEADS, DELTA_CHUNK, DELTA_DK), lambda i: (i, 0, 0, 0))
    out_shape = jax.ShapeDtypeStruct((t // DELTA_CHUNK, N_DELTA_HEADS, DELTA_CHUNK, DELTA_DK), F32)
    return pl.pallas_call(
        _delta_pre_kernel,
        grid=(t // tb,),
        in_specs=[pl.BlockSpec((tb, c), lambda i: (i, 0)),
                  pl.BlockSpec((SUBLANES, c), lambda i: (jnp.maximum(i * (tb // SUBLANES) - 1, 0), 0)),
                  pl.BlockSpec((CONV_WIDTH, c), lambda i: (0, 0))],
        out_specs=[out_spec] * 3,
        out_shape=[out_shape] * 3,
        scratch_shapes=[pltpu.VMEM((tb + SUBLANES, c), F32)],
        compiler_params=pltpu.CompilerParams(dimension_semantics=("arbitrary",),
                                             vmem_limit_bytes=V7X_VMEM_LIMIT),
        name="delta_pre",
    )(qkv, qkv, w_conv)


def _delta_scan_kernel(wq_ref, kt_ref, u_ref, qk_ref, el_ref, z_ref, ng_ref, o_ref, sfin_ref, s_ref):
    n = pl.program_id(0)
    c = DELTA_CHUNK

    @pl.when(n == 0)
    def _():
        s_ref[...] = jnp.zeros(s_ref.shape, F32)

    for h in range(N_DELTA_HEADS):
        s = s_ref[h]
        ws = jnp.dot(wq_ref[h], s.astype(BF16), preferred_element_type=F32)
        v_corr = (u_ref[h] - ws[:c]).astype(BF16)
        o = ws[c:] + jnp.dot(qk_ref[h], v_corr, preferred_element_type=F32)
        s_ref[h] = s * el_ref[h] + jnp.dot(kt_ref[h], v_corr, preferred_element_type=F32)
        o = o * lax.rsqrt(jnp.mean(o * o, axis=1, keepdims=True) + 1e-6) * ng_ref[...]
        zz = z_ref[:, h * DELTA_DV:(h + 1) * DELTA_DV]
        o_ref[:, h * DELTA_DV:(h + 1) * DELTA_DV] = (o * (zz * jax.nn.sigmoid(zz))).astype(o_ref.dtype)

    @pl.when(n == pl.num_programs(0) - 1)
    def _():
        sfin_ref[...] = s_ref[...]


def prompt_delta_branch(p_delta, p_gate, t, w_conv, a_log, dt_bias, norm_g):
    assert t >= CONV_WIDTH - 1
    c, h = DELTA_CHUNK, N_DELTA_HEADS
    n = t // c
    new_buf = p_delta[t - (CONV_WIDTH - 1):t, :CONV_CH]
    qc, kc, vc = delta_pre(p_delta, t, w_conv)
    beta = jax.nn.sigmoid(p_delta[:t, CONV_CH:CONV_CH + h])
    g = -jnp.exp(a_log) * jax.nn.softplus(p_delta[:t, CONV_CH + h:CONV_CH + 2 * h] + dt_bias)
    bc = beta.reshape(n, c, h).transpose(0, 2, 1)
    gc = jnp.cumsum(g.reshape(n, c, h).transpose(0, 2, 1), axis=-1)
    tri = jnp.tril(jnp.ones((c, c), bool))
    strict = jnp.tril(jnp.ones((c, c), bool), -1)
    decay = jnp.exp(jnp.where(tri, gc[..., :, None] - gc[..., None, :], -jnp.inf))
    kb = kc * bc[..., None]
    a_low = jnp.where(strict, jnp.einsum('nhid,nhjd->nhij', kb, kc) * decay, 0.0)
    rhs = jnp.concatenate([vc * bc[..., None], kb * jnp.exp(gc)[..., None]], axis=-1)
    sol = lax.linalg.triangular_solve(a_low + jnp.eye(c, dtype=F32), rhs, left_side=True,
                                      lower=True, unit_diagonal=True)
    u, w = sol[..., :DELTA_DV], sol[..., DELTA_DV:]
    qk = jnp.where(tri, jnp.einsum('nhid,nhjd->nhij', qc, kc) * decay, 0.0)
    g_last = gc[..., -1:]
    wq = jnp.concatenate([w, qc * jnp.exp(gc)[..., None]], axis=-2).astype(BF16)
    kt = jnp.swapaxes(kc * jnp.exp(g_last - gc)[..., None], -1, -2).astype(BF16)
    e_last = jnp.broadcast_to(jnp.exp(g_last)[..., None], (n, h, 1, DELTA_DV))

    def per_chunk(*blk):
        return pl.BlockSpec((None,) + blk, lambda i: (i,) + (0,) * len(blk))

    d_out, s_fin = pl.pallas_call(
        _delta_scan_kernel,
        grid=(n,),
        in_specs=[per_chunk(h, 2 * c, DELTA_DK), per_chunk(h, DELTA_DK, c), per_chunk(h, c, DELTA_DV),
                  per_chunk(h, c, c), per_chunk(h, 1, DELTA_DV),
                  pl.BlockSpec((c, h * DELTA_DV), lambda i: (i, 0)),
                  pl.BlockSpec((1, DELTA_DV), lambda i: (0, 0))],
        out_specs=[pl.BlockSpec((c, h * DELTA_DV), lambda i: (i, 0)),
                   pl.BlockSpec((h, DELTA_DK, DELTA_DV), lambda i: (0, 0, 0))],
        out_shape=[jax.ShapeDtypeStruct((t, h * DELTA_DV), BF16),
                   jax.ShapeDtypeStruct((h, DELTA_DK, DELTA_DV), F32)],
        scratch_shapes=[pltpu.VMEM((h, DELTA_DK, DELTA_DV), F32)],
        compiler_params=pltpu.CompilerParams(dimension_semantics=("arbitrary",),
                                             vmem_limit_bytes=V7X_VMEM_LIMIT),
        name="delta_scan",
    )(wq, kt, u, qk.astype(BF16), e_last, p_gate, norm_g.reshape(1, DELTA_DV))
    return d_out, s_fin, new_buf


def routed_experts(xt, expert_idx, gate, w_gate, w_up, w_down):
    n, d = xt.shape
    k = expert_idx.shape[1]
    n_exp = w_gate.shape[0]
    blk = EXPERT_BLOCK
    flat_e = expert_idx.reshape(-1)
    order = jnp.argsort(flat_e).astype(jnp.int32)
    counts = jnp.bincount(flat_e, length=n_exp).astype(jnp.int32)
    padded = (counts + blk - 1) // blk * blk
    pad_end = jnp.cumsum(padded)
    pad_start = pad_end - padded
    start = jnp.cumsum(counts) - counts
    n_blocks = -(-(n * k) // blk) + n_exp
    rows = n_blocks * blk
    block_expert = jnp.minimum(jnp.searchsorted(pad_end, jnp.arange(n_blocks) * blk, side='right'),
                               n_exp - 1).astype(jnp.int32)
    n_used = (pad_end[-1:] // blk).astype(jnp.int32)
    r = jnp.arange(rows, dtype=jnp.int32)
    row_e = jnp.repeat(block_expert, blk)
    rank = r - pad_start[row_e]
    real = rank < counts[row_e]
    src = order[jnp.clip(start[row_e] + rank, 0, n * k - 1)]
    row_token = jnp.where(real, src // k, r % n)
    row_gate = jnp.where(real, gate.reshape(-1)[src], 0.0)
    xb = xt.astype(BF16)
    grp_blocks = -(-n_blocks // ROW_GROUPS)
    h_parts = []
    for b0 in range(0, n_blocks, grp_blocks):
        b1 = min(b0 + grp_blocks, n_blocks)
        x_rows = xb[row_token[b0 * blk:b1 * blk]]
        h_parts.append(grouped_swiglu_up(x_rows, block_expert[b0:b1], jnp.clip(n_used - b0, 0, b1 - b0),
                                         w_gate, w_up, blk))
    h_rows = jnp.concatenate(h_parts, axis=0)
    y_rows = grouped_down(h_rows, row_gate[:, None], block_expert, n_used, w_down, blk)
    slot = jnp.argsort(order).astype(jnp.int32)
    e_of = flat_e.astype(jnp.int32)
    pos = pad_start[e_of] + slot - start[e_of]
    return jnp.sum(y_rows[pos].reshape(n, k, d), axis=1)


def moe(xt, w_router, router_bias, w_exp_gate, w_exp_up, w_exp_down, w_sh_gate, w_sh_up, w_sh_down):
    n = xt.shape[0]
    scores = jax.nn.sigmoid(jnp.matmul(xt, w_router, preferred_element_type=F32))
    biased = scores + router_bias
    per_group = N_EXPERTS // N_GROUPS
    group_score = lax.top_k(biased.reshape(n, N_GROUPS, per_group), 2)[0].sum(-1)
    _, top_groups = lax.top_k(group_score, TOPK_GROUPS)
    group_keep = jnp.any(top_groups[:, :, None] == jnp.arange(N_GROUPS)[None, None, :], axis=1)
    biased = jnp.where(jnp.repeat(group_keep, per_group, axis=1), biased, -jnp.inf)
    _, expert_idx = lax.top_k(biased, TOP_K)
    gate = jnp.take_along_axis(scores, expert_idx, axis=1)
    gate = gate / jnp.sum(gate, -1, keepdims=True) * ROUTED_SCALE
    routed = routed_experts(xt, expert_idx, gate, w_exp_gate, w_exp_up, w_exp_down)
    xb = xt.astype(BF16)
    sg = matmul(xb, w_sh_gate)
    su = matmul(xb, w_sh_up)
    shared = matmul((jax.nn.silu(sg) * su).astype(BF16), w_sh_down)
    return routed + shared


def kernel(x_prompt, x_sample, cache_k, cache_v, cache_idx_k, page_table, state_delta, state_conv, w_in, w_conv, a_log, dt_bias, delta_norm_g, w_branch_attn, w_branch_delta, w_out, ln1_g, ln1_b, w_router, router_bias, w_exp_gate, w_exp_up, w_exp_down, w_sh_gate, w_sh_up, w_sh_down, ln2_g, ln2_b):
    b, s, d = x_prompt.shape
    db, t = x_sample.shape[:2]
    assert DEPTH == 1 and b == 1
    past = page_table.shape[1] * cache_k.shape[2]
    n_p, n_s = b * s, db * t
    (l_in, l_conv, l_alog, l_dtb, l_ng, l_ba, l_bd, l_out, l_g1, l_b1,
     l_r, l_rb, l_eg, l_eu, l_ed, l_sg, l_su, l_sd, l_g2, l_b2) = [a[0] for a in (
         w_in, w_conv, a_log, dt_bias, delta_norm_g, w_branch_attn, w_branch_delta, w_out,
         ln1_g, ln1_b, w_router, router_bias, w_exp_gate, w_exp_up, w_exp_down,
         w_sh_gate, w_sh_up, w_sh_down, ln2_g, ln2_b)]

    x_all = jnp.concatenate([x_prompt.reshape(n_p, d), x_sample.reshape(n_s, d)], axis=0)
    xb = x_all.astype(BF16)
    p_attn = matmul_cols(xb, l_in, 0, ATTN_COLS)
    p_delta = matmul_cols(xb, l_in, IN_SPLITS[5], DELTA_COLS)
    p_gate = matmul_cols(xb, l_in, IN_SPLITS[8], GATE_COLS)
    ap = attention_inputs(p_attn[:n_p].reshape(b, s, -1), jnp.arange(s))
    at = attention_inputs(p_attn[n_p:].reshape(db, t, -1), past + jnp.arange(t))

    a_p = prompt_sparse_attention(ap.q, ap.k, ap.v, ap.iq, ap.iw, ap.ik)
    a_s = sample_sparse_attention(at.q, at.k, at.v, at.iq, at.iw, at.ik,
                                  cache_k[0], cache_v[0], cache_idx_k[0], page_table)
    d_p, sd_p, sc_p = prompt_delta_branch(p_delta, p_gate, n_p, l_conv, l_alog, l_dtb, l_ng)
    pd_s = p_delta[n_p:].reshape(db, t, -1)
    nh = N_DELTA_HEADS
    d_s, sd_s, sc_s = gated_delta_branch(pd_s[..., :CONV_CH], pd_s[..., CONV_CH:CONV_CH + nh], pd_s[..., CONV_CH + nh:],
                                         p_gate[n_p:, :DELTA_V_WIDTH].reshape(db, t, -1), state_conv[0], state_delta[0],
                                         l_conv, l_alog, l_dtb, l_ng)

    a_all = jnp.concatenate([a_p.reshape(n_p, -1).astype(BF16), a_s.reshape(n_s, -1).astype(BF16)], axis=0)
    d_all = jnp.concatenate([d_p, d_s.reshape(n_s, -1).astype(BF16)], axis=0)
    merged = merge_branches(a_all, d_all, p_gate, DELTA_V_WIDTH, l_ba, l_bd)
    h = layer_norm(ALPHA * x_all + matmul(merged, l_out), l_g1, l_b1)
    y = layer_norm(ALPHA * h + moe(h, l_r, l_rb, l_eg, l_eu, l_ed, l_sg, l_su, l_sd), l_g2, l_b2)
    return (y[:n_p].reshape(b, s, d), y[n_p:].reshape(db, t, d),
            ap.k[None], ap.v[None], ap.ik[None], sd_p[None, None], sc_p[None, None],
            at.k[None], at.v[None], at.ik[None], sd_s[None], sc_s[None])
```

```python
import collections
import functools

import jax
import jax.numpy as jnp
import numpy as np
from jax import lax
from jax.experimental import pallas as pl
from jax.experimental.pallas import tpu as pltpu

D_MODEL = 4096
DEPTH = 1
N_HEADS = 16
N_KV_HEADS = 4
HEAD_DIM = 128
ROPE_THETA = 500000.0
N_IDX_HEADS = 16
IDX_DIM = 128
IDX_TOPK = 256
Q_BLOCK = 128
N_DELTA_HEADS = 16
DELTA_DK = 128
DELTA_DV = 128
CONV_WIDTH = 4
DELTA_CHUNK = 64
N_EXPERTS = 64
TOP_K = 8
N_GROUPS = 8
TOPK_GROUPS = 4
EXPERT_DIM = 1024
ROUTED_SCALE = 2.5
EXPERT_BLOCK = 256
ROW_GROUPS = 2
ALPHA = (2 * DEPTH) ** 0.25
LN_EPS = 1e-5
ATTN_WIDTH = N_HEADS * HEAD_DIM
KV_WIDTH = N_KV_HEADS * HEAD_DIM
IDXQ_WIDTH = N_IDX_HEADS * IDX_DIM
DELTA_QK_WIDTH = N_DELTA_HEADS * DELTA_DK
DELTA_V_WIDTH = N_DELTA_HEADS * DELTA_DV
CONV_CH = 2 * DELTA_QK_WIDTH + DELTA_V_WIDTH
IN_SIZES = (ATTN_WIDTH, KV_WIDTH, KV_WIDTH, IDXQ_WIDTH, IDX_DIM, N_IDX_HEADS, CONV_CH,
            N_DELTA_HEADS, N_DELTA_HEADS, DELTA_V_WIDTH, 2 * D_MODEL)
IN_SPLITS = tuple(int(s) for s in np.cumsum(IN_SIZES)[:-1])

V7X_VMEM_LIMIT = 56 * 1024 * 1024
BF16 = jnp.bfloat16
F32 = jnp.float32
SUBLANES = 8


def _mm_kernel(x_ref, w_ref, o_ref, wbf_ref):
    @pl.when(pl.program_id(1) == 0)
    def _():
        wbf_ref[...] = w_ref[...].astype(BF16)

    o_ref[...] = jnp.dot(x_ref[...], wbf_ref[...], preferred_element_type=F32).astype(o_ref.dtype)


def _row_tile(m):
    for tm in (1024, 1088, 512, 256, 128):
        if m % tm == 0:
            return tm
    raise ValueError(f"no row tile for {m} rows")


def matmul(x, w, out_dtype=F32, tn=512):
    m, k = x.shape
    n = w.shape[1]
    tm = _row_tile(m)
    tn = min(tn, n)
    return pl.pallas_call(
        _mm_kernel,
        grid=(pl.cdiv(n, tn), m // tm),
        in_specs=[pl.BlockSpec((tm, k), lambda j, i: (i, 0)),
                  pl.BlockSpec((k, tn), lambda j, i: (0, j))],
        out_specs=pl.BlockSpec((tm, tn), lambda j, i: (i, j)),
        out_shape=jax.ShapeDtypeStruct((m, n), out_dtype),
        scratch_shapes=[pltpu.VMEM((k, tn), BF16)],
        compiler_params=pltpu.CompilerParams(dimension_semantics=("arbitrary", "arbitrary"),
                                             vmem_limit_bytes=V7X_VMEM_LIMIT),
        name="dense_matmul",
    )(x, w)


def _merge_kernel(a_ref, d_ref, ga_ref, gd_ref, wa_ref, wd_ref, o_ref, wa_bf, wd_bf):
    @pl.when(pl.program_id(1) == 0)
    def _():
        wa_bf[...] = wa_ref[...].astype(BF16)
        wd_bf[...] = wd_ref[...].astype(BF16)

    pa = jnp.dot(a_ref[...], wa_bf[...], preferred_element_type=F32)
    pd = jnp.dot(d_ref[...], wd_bf[...], preferred_element_type=F32)
    o_ref[...] = (jax.nn.sigmoid(ga_ref[...]) * pa + jax.nn.sigmoid(gd_ref[...]) * pd).astype(o_ref.dtype)


def merge_branches(a, d, p_gate, gate_col0, w_a, w_d, tn=512):
    m, k = a.shape
    n = w_a.shape[1]
    tm = _row_tile(m)
    assert gate_col0 % tn == 0 and n % tn == 0 and w_d.shape == w_a.shape and d.shape == a.shape
    ga0, gd0 = gate_col0 // tn, (gate_col0 + n) // tn
    return pl.pallas_call(
        _merge_kernel,
        grid=(n // tn, m // tm),
        in_specs=[pl.BlockSpec((tm, k), lambda j, i: (i, 0)),
                  pl.BlockSpec((tm, k), lambda j, i: (i, 0)),
                  pl.BlockSpec((tm, tn), lambda j, i: (i, ga0 + j)),
                  pl.BlockSpec((tm, tn), lambda j, i: (i, gd0 + j)),
                  pl.BlockSpec((k, tn), lambda j, i: (0, j)),
                  pl.BlockSpec((k, tn), lambda j, i: (0, j))],
        out_specs=pl.BlockSpec((tm, tn), lambda j, i: (i, j)),
        out_shape=jax.ShapeDtypeStruct((m, n), BF16),
        scratch_shapes=[pltpu.VMEM((k, tn), BF16), pltpu.VMEM((k, tn), BF16)],
        compiler_params=pltpu.CompilerParams(dimension_semantics=("arbitrary", "arbitrary"),
                                             vmem_limit_bytes=V7X_VMEM_LIMIT),
        name="merge_branches",
    )(a, d, p_gate, p_gate, w_a, w_d)


def _mm_rows_kernel(x_ref, wt_ref, o_ref, wbf_ref):
    @pl.when(pl.program_id(1) == 0)
    def _():
        wbf_ref[...] = wt_ref[...].astype(BF16)

    o_ref[...] = lax.dot_general(x_ref[...], wbf_ref[...], (((1,), (1,)), ((), ())),
                                 preferred_element_type=F32).astype(o_ref.dtype)


def matmul_rows(x, wt, row0, nrows, out_dtype=F32, tn=512):
    m, k = x.shape
    tm = _row_tile(m)
    tn = min(tn, nrows)
    assert row0 % SUBLANES == 0 and wt.shape[1] == k
    w_spec = pl.BlockSpec((pl.Element(tn, (0, tn)), pl.Element(k)),
                          lambda j, i: (pl.multiple_of(row0 + j * tn, SUBLANES), 0))
    return pl.pallas_call(
        _mm_rows_kernel,
        grid=(pl.cdiv(nrows, tn), m // tm),
        in_specs=[pl.BlockSpec((tm, k), lambda j, i: (i, 0)), w_spec],
        out_specs=pl.BlockSpec((tm, tn), lambda j, i: (i, j)),
        out_shape=jax.ShapeDtypeStruct((m, nrows), out_dtype),
        scratch_shapes=[pltpu.VMEM((tn, k), BF16)],
        compiler_params=pltpu.CompilerParams(dimension_semantics=("arbitrary", "arbitrary"),
                                             vmem_limit_bytes=V7X_VMEM_LIMIT),
        name="dense_matmul_rows",
    )(x, wt)


def _expert_changed(be_ref, i):
    prev = be_ref[jnp.maximum(i - 1, 0)]
    return jnp.logical_or(i == 0, be_ref[i] != prev)


def _moe_up_kernel(be_ref, nu_ref, x_ref, wg_ref, wu_ref, h_ref, wg_bf, wu_bf):
    i = pl.program_id(1)

    @pl.when(i < nu_ref[0])
    def _():
        @pl.when(_expert_changed(be_ref, i))
        def _():
            wg_bf[...] = wg_ref[...].astype(BF16)
            wu_bf[...] = wu_ref[...].astype(BF16)

        x = x_ref[...]
        g = jnp.dot(x, wg_bf[...], preferred_element_type=F32)
        u = jnp.dot(x, wu_bf[...], preferred_element_type=F32)
        h_ref[...] = (g * jax.nn.sigmoid(g) * u).astype(h_ref.dtype)


def _moe_down_kernel(be_ref, nu_ref, h_ref, gate_ref, wd_ref, y_ref, wd_bf):
    i = pl.program_id(1)

    @pl.when(i < nu_ref[0])
    def _():
        @pl.when(_expert_changed(be_ref, i))
        def _():
            wd_bf[...] = wd_ref[...].astype(BF16)

        y = jnp.dot(h_ref[...], wd_bf[...], preferred_element_type=F32)
        y_ref[...] = (y * gate_ref[...]).astype(y_ref.dtype)


def grouped_swiglu_up(x_rows, block_expert, n_used, w_gate, w_up, blk, tf=512):
    rows, d = x_rows.shape
    f = w_gate.shape[2]
    tf = min(tf, f)
    n_blocks = rows // blk
    grid_spec = pltpu.PrefetchScalarGridSpec(
        num_scalar_prefetch=2,
        grid=(f // tf, n_blocks),
        in_specs=[pl.BlockSpec((blk, d), lambda j, i, be, nu: (i, 0)),
                  pl.BlockSpec((None, d, tf), lambda j, i, be, nu: (be[i], 0, j)),
                  pl.BlockSpec((None, d, tf), lambda j, i, be, nu: (be[i], 0, j))],
        out_specs=pl.BlockSpec((blk, tf), lambda j, i, be, nu: (i, j)),
        scratch_shapes=[pltpu.VMEM((d, tf), BF16), pltpu.VMEM((d, tf), BF16)],
    )
    return pl.pallas_call(
        _moe_up_kernel, grid_spec=grid_spec,
        out_shape=jax.ShapeDtypeStruct((rows, f), BF16),
        compiler_params=pltpu.CompilerParams(dimension_semantics=("arbitrary", "arbitrary"),
                                             vmem_limit_bytes=V7X_VMEM_LIMIT),
        name="moe_up",
    )(block_expert, n_used, x_rows, w_gate, w_up)


def grouped_down(h_rows, row_gate, block_expert, n_used, w_down, blk, tn=2048):
    rows, f = h_rows.shape
    d = w_down.shape[2]
    tn = min(tn, d)
    n_blocks = rows // blk
    grid_spec = pltpu.PrefetchScalarGridSpec(
        num_scalar_prefetch=2,
        grid=(d // tn, n_blocks),
        in_specs=[pl.BlockSpec((blk, f), lambda j, i, be, nu: (i, 0)),
                  pl.BlockSpec((blk, 1), lambda j, i, be, nu: (i, 0)),
                  pl.BlockSpec((None, f, tn), lambda j, i, be, nu: (be[i], 0, j))],
        out_specs=pl.BlockSpec((blk, tn), lambda j, i, be, nu: (i, j)),
        scratch_shapes=[pltpu.VMEM((f, tn), BF16)],
    )
    return pl.pallas_call(
        _moe_down_kernel, grid_spec=grid_spec,
        out_shape=jax.ShapeDtypeStruct((rows, d), F32),
        compiler_params=pltpu.CompilerParams(dimension_semantics=("arbitrary", "arbitrary"),
                                             vmem_limit_bytes=V7X_VMEM_LIMIT),
        name="moe_down",
    )(block_expert, n_used, h_rows, row_gate, w_down)


def layer_norm(x, g, b):
    xc = x - jnp.mean(x, -1, keepdims=True)
    var = jnp.mean(xc * xc, -1, keepdims=True)
    return xc * lax.rsqrt(var + LN_EPS) * g + b


def l2norm(x):
    return x * lax.rsqrt(jnp.sum(x * x, -1, keepdims=True) + 1e-6)


def partial_rope(x, pos):
    rot = x.shape[-1] // 4
    half = rot // 2
    inv_freq = ROPE_THETA ** (-jnp.arange(half, dtype=F32) / half)
    ang = pos.astype(F32)[:, None] * inv_freq[None, :]
    cos = jnp.cos(ang)[:, None, :]
    sin = jnp.sin(ang)[:, None, :]
    x1, x2, rest = x[..., :half], x[..., half:rot], x[..., rot:]
    return jnp.concatenate([x1 * cos - x2 * sin, x2 * cos + x1 * sin, rest], axis=-1)


ATTN_COLS = IN_SPLITS[5]
DELTA_COLS = IN_SPLITS[8] - IN_SPLITS[5]
GATE_COLS = int(sum(IN_SIZES)) - IN_SPLITS[8]
AttnInputs = collections.namedtuple("AttnInputs", "q k v iq ik iw")


def attention_inputs(p, pos):
    b, t, _ = p.shape
    q, k, v, iq, ik, iw = jnp.split(p, IN_SPLITS[:5], axis=-1)
    q = partial_rope(q.reshape(b, t, N_HEADS, HEAD_DIM), pos)
    k = partial_rope(k.reshape(b, t, N_KV_HEADS, HEAD_DIM), pos)
    v = v.reshape(b, t, N_KV_HEADS, HEAD_DIM)
    iq = partial_rope(iq.reshape(b, t, N_IDX_HEADS, IDX_DIM), pos)
    ik = partial_rope(ik[:, :, None, :], pos)[:, :, 0, :]
    iw = iw * (N_IDX_HEADS ** -0.5 * IDX_DIM ** -0.5)
    return AttnInputs(q, k, v, iq, ik, iw)


KEY_CHUNK = 512
MASKED = -1e30
SOFTMAX_LOG2_SCALE = HEAD_DIM ** -0.5 * float(np.log2(np.e))
INT32_MIN = -2 ** 31
INT32_MAX = 2 ** 31 - 1
NEG_INF_KEY = int(np.int32(np.uint32(0xFF800000) ^ np.uint32(0x7FFFFFFF)))


def _sort_key(x):
    bits = lax.bitcast_convert_type(x, jnp.int32)
    return bits ^ (jnp.right_shift(bits, 31) & jnp.int32(INT32_MAX))


def _lane_tile(x, n):
    return x if n == 1 else jnp.concatenate([x] * n, axis=1)


def _count_rows(keys_ref, nkc, preds):
    _, rows, kc = keys_ref.shape

    def body(c, parts):
        keys = keys_ref[c]
        out = []
        for pred, part in zip(preds, parts):
            hit = pred(keys, c)
            for j in range(kc // 128):
                part = part + hit[:, j * 128:(j + 1) * 128]
            out.append(part)
        return tuple(out)

    parts = lax.fori_loop(0, nkc, body, tuple(jnp.zeros((rows, 128), F32) for _ in preds))
    return [jnp.broadcast_to(jnp.sum(part, axis=1, keepdims=True), (rows, 128)) for part in parts]


def _topk_selection(keys_ref, nkc, n_sel, radix_bits):
    _, rows, kc = keys_ref.shape
    reps = kc // 128
    col = lax.broadcasted_iota(jnp.int32, (rows, kc), 1)

    def thr_digit(it, carry):
        thr, cnt_thr = carry
        shift = 32 - radix_bits * (it + 1)
        cands = [thr + jnp.left_shift(jnp.int32(j), shift) for j in range(1, 2 ** radix_bits)]
        cands_w = [_lane_tile(cand, reps) for cand in cands]
        cnts = _count_rows(keys_ref, nkc, [lambda keys, c, cw=cw: jnp.where(keys >= cw, 1.0, 0.0) for cw in cands_w])
        for cand, cnt in zip(cands, cnts):
            ok = cnt >= n_sel
            thr = jnp.where(ok, cand, thr)
            cnt_thr = jnp.where(ok, cnt, cnt_thr)
        return thr, cnt_thr

    thr0 = jnp.full((rows, 128), INT32_MIN, jnp.int32)
    cnt0 = jnp.broadcast_to(jnp.asarray(nkc * kc).astype(F32), (rows, 128))
    thr, cnt_thr = lax.fori_loop(0, 32 // radix_bits, thr_digit, (thr0, cnt0))

    short = thr == NEG_INF_KEY
    thr_w = _lane_tile(thr, reps)
    cnt_gt, = _count_rows(keys_ref, nkc, [lambda keys, c: jnp.where(keys > thr_w, 1.0, 0.0)])
    need = n_sel - cnt_gt
    tied = jnp.logical_and(cnt_thr > n_sel, jnp.logical_not(short))

    def tie_limit():
        def idx_bit(it, x):
            cand = x + jnp.left_shift(jnp.int32(1), 30 - it)
            cand_w = _lane_tile(cand, reps)
            cnt, = _count_rows(keys_ref, nkc, [lambda keys, c: jnp.where(
                keys == thr_w, jnp.where(c * kc + col < cand_w, 1.0, 0.0), 0.0)])
            return jnp.where(cnt < need, cand, x)
        return lax.fori_loop(0, 31, idx_bit, jnp.zeros((rows, 128), jnp.int32))

    any_tied = jnp.max(jnp.where(tied, 1.0, 0.0)) > 0.0
    lim = lax.cond(any_tied, tie_limit, lambda: jnp.full((rows, 128), INT32_MAX, jnp.int32))
    lim = jnp.where(short, -1, jnp.where(tied, lim, INT32_MAX))
    return thr, lim


def _selected(keys, index, thr_w, lim_w, yes, no):
    keep_tie = jnp.where(index <= lim_w, yes, no)
    return jnp.where(keys > thr_w, yes, jnp.where(keys == thr_w, keep_tie, no))


def _dsa_prompt_kernel(iq_ref, iw_ref, ikt_ref, q_ref, kt_ref, v_ref, o_ref,
                       keys_ref, m_ref, l_ref, acc_ref, *, n_sel):
    qb, kc = Q_BLOCK, KEY_CHUNK
    reps = kc // 128
    grp = N_HEADS // N_KV_HEADS
    i = pl.program_id(0)
    nkc = (i * qb + qb + kc - 1) // kc
    qpos = i * qb + lax.broadcasted_iota(jnp.int32, (qb, kc), 0)
    col = lax.broadcasted_iota(jnp.int32, (qb, kc), 1)

    def score_chunk(c, carry):
        ikc = ikt_ref[c]
        acc = jnp.zeros((qb, kc), F32)
        for h in range(N_IDX_HEADS):
            s = jnp.dot(iq_ref[h], ikc, preferred_element_type=F32)
            acc = acc + jnp.maximum(s, 0.0) * iw_ref[:, h:h + 1]
        acc = jnp.where(c * kc + col <= qpos, acc, -jnp.inf)
        keys_ref[c] = _sort_key(acc)
        return carry

    lax.fori_loop(0, nkc, score_chunk, 0)

    thr, lim = _topk_selection(keys_ref, nkc, n_sel, radix_bits=1)
    thr_w, lim_w = _lane_tile(thr, reps), _lane_tile(lim, reps)

    m_ref[...] = jnp.full(m_ref.shape, MASKED, F32)
    l_ref[...] = jnp.zeros(l_ref.shape, F32)
    acc_ref[...] = jnp.zeros(acc_ref.shape, F32)
    rows = grp * qb

    def attend_chunk(c, carry):
        bias = _selected(keys_ref[c], c * kc + col, thr_w, lim_w, 0.0, MASKED)
        for n in range(N_KV_HEADS):
            r = pl.ds(n * rows, rows)
            qn = q_ref[n * grp:(n + 1) * grp].reshape(rows, HEAD_DIM)
            s = jnp.dot(qn, kt_ref[n, c], preferred_element_type=F32) * SOFTMAX_LOG2_SCALE
            s = (s.reshape(grp, qb, kc) + bias[None]).reshape(rows, kc)
            m_prev = m_ref[r, :]
            m_new = jnp.maximum(m_prev, jnp.max(s, axis=1, keepdims=True))
            alpha = jnp.exp2(m_prev - m_new)
            p = jnp.exp2(s - _lane_tile(m_new, reps))
            l_ref[r, :] = alpha * l_ref[r, :] + jnp.sum(p, axis=1, keepdims=True)
            acc_ref[r, :] = alpha * acc_ref[r, :] + jnp.dot(p.astype(BF16), v_ref[n, c],
                                                           preferred_element_type=F32)
            m_ref[r, :] = m_new
        return carry

    lax.fori_loop(0, nkc, attend_chunk, 0)
    for h in range(N_HEADS):
        r = pl.ds(h * qb, qb)
        o_ref[:, h * HEAD_DIM:(h + 1) * HEAD_DIM] = (acc_ref[r, :] / l_ref[r, :]).astype(o_ref.dtype)


def prompt_sparse_attention(q, k, v, iq, iw, ik):
    b, s = q.shape[:2]
    assert b == 1 and s % KEY_CHUNK == 0 and KEY_CHUNK % Q_BLOCK == 0 and KEY_CHUNK >= IDX_TOPK
    n_sel = min(IDX_TOPK, s // 4)
    nb, nc, kc = s // Q_BLOCK, s // KEY_CHUNK, KEY_CHUNK

    def head_major(a):
        return a.astype(BF16).reshape(nb, Q_BLOCK, a.shape[2], a.shape[3]).transpose(0, 2, 1, 3)

    ikt = ik.astype(BF16).reshape(nc, kc, IDX_DIM).transpose(0, 2, 1)
    kt = k.astype(BF16).reshape(nc, kc, N_KV_HEADS, HEAD_DIM).transpose(2, 0, 3, 1)
    vc = v.astype(BF16).reshape(nc, kc, N_KV_HEADS, HEAD_DIM).transpose(2, 0, 1, 3)
    resident = dict(pipeline_mode=pl.Buffered(1))
    out = pl.pallas_call(
        functools.partial(_dsa_prompt_kernel, n_sel=n_sel),
        grid=(nb,),
        in_specs=[pl.BlockSpec((None, N_IDX_HEADS, Q_BLOCK, IDX_DIM), lambda i: (i, 0, 0, 0)),
                  pl.BlockSpec((Q_BLOCK, N_IDX_HEADS), lambda i: (i, 0)),
                  pl.BlockSpec((nc, IDX_DIM, kc), lambda i: (0, 0, 0), **resident),
                  pl.BlockSpec((None, N_HEADS, Q_BLOCK, HEAD_DIM), lambda i: (i, 0, 0, 0)),
                  pl.BlockSpec((N_KV_HEADS, nc, HEAD_DIM, kc), lambda i: (0, 0, 0, 0), **resident),
                  pl.BlockSpec((N_KV_HEADS, nc, kc, HEAD_DIM), lambda i: (0, 0, 0, 0), **resident)],
        out_specs=pl.BlockSpec((Q_BLOCK, N_HEADS * HEAD_DIM), lambda i: (i, 0)),
        out_shape=jax.ShapeDtypeStruct((s, N_HEADS * HEAD_DIM), BF16),
        scratch_shapes=[pltpu.VMEM((nc, Q_BLOCK, kc), jnp.int32),
                        pltpu.VMEM((N_HEADS * Q_BLOCK, 128), F32),
                        pltpu.VMEM((N_HEADS * Q_BLOCK, 128), F32),
                        pltpu.VMEM((N_HEADS * Q_BLOCK, HEAD_DIM), F32)],
        compiler_params=pltpu.CompilerParams(dimension_semantics=("arbitrary",),
                                             vmem_limit_bytes=V7X_VMEM_LIMIT),
        name="dsa_prompt",
    )(head_major(iq), iw.reshape(s, N_IDX_HEADS), ikt, head_major(q), kt, vc)
    return out.reshape(b, s, N_HEADS * HEAD_DIM)


T_PAD = 8


def _dsa_sample_kernel(pt_ref, iq_ref, iw_ref, q_ref, iknew_ref, knew_ref, vnew_ref, *rest, n_sel, n_pages):
    del pt_ref
    ik_pages, k_pages, v_pages = rest[:n_pages], rest[n_pages:2 * n_pages], rest[2 * n_pages:3 * n_pages]
    o_ref, keys_ref, s_ref = rest[3 * n_pages:]
    page = ik_pages[0].shape[0]
    kv_rows = k_pages[0].shape[0]
    rows = N_HEADS * T_PAD
    nkc = n_pages + 1
    nt = (((1,), (1,)), ((), ()))
    row_t = lax.broadcasted_iota(jnp.int32, (T_PAD, page), 0)
    col = lax.broadcasted_iota(jnp.int32, (T_PAD, page), 1)

    for c in range(nkc):
        ikc = ik_pages[c][...].astype(BF16) if c < n_pages else iknew_ref[...]
        s = lax.dot_general(iq_ref[...], ikc, nt, preferred_element_type=F32)
        x = jnp.maximum(s, 0.0) * iw_ref[...]
        sc = jnp.sum(x.reshape(T_PAD, N_IDX_HEADS, page), axis=1)
        if c == n_pages:
            sc = jnp.where(col <= row_t, sc, -jnp.inf)
        keys_ref[c] = _sort_key(sc)

    thr, lim = _topk_selection(keys_ref, nkc, n_sel, radix_bits=4)

    scale = SOFTMAX_LOG2_SCALE
    spread = (lax.broadcasted_iota(jnp.int32, (page, kv_rows), 1) // N_KV_HEADS
              == lax.broadcasted_iota(jnp.int32, (page, kv_rows), 0)).astype(BF16)
    own_head = (lax.broadcasted_iota(jnp.int32, (rows, kv_rows), 0) // (rows // N_KV_HEADS)
                == lax.broadcasted_iota(jnp.int32, (rows, kv_rows), 1) % N_KV_HEADS).astype(F32)
    for c in range(nkc):
        kc_ = k_pages[c][...].astype(BF16) if c < n_pages else knew_ref[...]
        s = lax.dot_general(q_ref[...], kc_, nt, preferred_element_type=F32) * scale
        sel = _selected(keys_ref[c], c * page + col, thr, lim, 1.0, 0.0).astype(BF16)
        sel = jnp.dot(sel, spread, preferred_element_type=F32)
        keep = jnp.broadcast_to(sel[None], (N_HEADS, T_PAD, kv_rows)).reshape(rows, kv_rows) * own_head
        s_ref[:, c * kv_rows:(c + 1) * kv_rows] = jnp.where(keep > 0.5, s, MASKED)
    s = s_ref[...]
    p = jnp.exp2(s - jnp.max(s, axis=1, keepdims=True))
    inv_l = 1.0 / jnp.sum(p, axis=1, keepdims=True)
    p = p.astype(BF16)
    o = jnp.zeros((rows, HEAD_DIM), F32)
    for c in range(nkc):
        vc_ = v_pages[c][...].astype(BF16) if c < n_pages else vnew_ref[...]
        o = o + jnp.dot(p[:, c * kv_rows:(c + 1) * kv_rows], vc_, preferred_element_type=F32)
    o_ref[...] = o * inv_l


def sample_sparse_attention(q, k_new, v_new, iq, iw, ik_new, cache_k, cache_v, cache_idx_k, page_table):
    db, t = q.shape[:2]
    n_pool, page = cache_k.shape[:2]
    n_pages = page_table.shape[1]
    n_keys = n_pages * page + t
    n_sel = min(IDX_TOPK, n_keys // 4)
    grp = N_HEADS // N_KV_HEADS
    assert t <= T_PAD <= page and page == 128 and HEAD_DIM == 128 and IDX_DIM == 128

    def pad_t(a, to):
        return jnp.pad(a, [(0, 0), (0, to - a.shape[1])] + [(0, 0)] * (a.ndim - 2))

    kv_rows = page * N_KV_HEADS
    rows = N_HEADS * T_PAD
    iq_rows = pad_t(iq, T_PAD).astype(BF16).reshape(db, T_PAD * N_IDX_HEADS, IDX_DIM)
    iw_rows = pad_t(iw, T_PAD).reshape(db, T_PAD * N_IDX_HEADS, 1)
    q_rows = pad_t(q, T_PAD).astype(BF16).transpose(0, 2, 1, 3).reshape(db, rows, HEAD_DIM)
    iknew = pad_t(ik_new, page).astype(BF16)
    knew = pad_t(k_new, page).astype(BF16).reshape(db, kv_rows, HEAD_DIM)
    vnew = pad_t(v_new, page).astype(BF16).reshape(db, kv_rows, HEAD_DIM)
    ck = cache_k.reshape(n_pool, kv_rows, HEAD_DIM)
    cv = cache_v.reshape(n_pool, kv_rows, HEAD_DIM)

    def per_batch(*blk):
        return pl.BlockSpec((None,) + blk, lambda b, pt: (b,) + (0,) * len(blk))

    def paged(nrows):
        return [pl.BlockSpec((None, nrows, HEAD_DIM), lambda b, pt, j=j: (pt[b, j], 0, 0)) for j in range(n_pages)]

    grid_spec = pltpu.PrefetchScalarGridSpec(
        num_scalar_prefetch=1,
        grid=(db,),
        in_specs=[per_batch(T_PAD * N_IDX_HEADS, IDX_DIM), per_batch(T_PAD * N_IDX_HEADS, 1),
                  per_batch(rows, HEAD_DIM), per_batch(page, IDX_DIM), per_batch(kv_rows, HEAD_DIM),
                  per_batch(kv_rows, HEAD_DIM)]
        + paged(page) + paged(kv_rows) + paged(kv_rows),
        out_specs=per_batch(rows, HEAD_DIM),
        scratch_shapes=[pltpu.VMEM((n_pages + 1, T_PAD, page), jnp.int32),
                        pltpu.VMEM((rows, (n_pages + 1) * kv_rows), F32)],
    )
    o = pl.pallas_call(
        functools.partial(_dsa_sample_kernel, n_sel=n_sel, n_pages=n_pages),
        grid_spec=grid_spec,
        out_shape=jax.ShapeDtypeStruct((db, rows, HEAD_DIM), F32),
        compiler_params=pltpu.CompilerParams(dimension_semantics=("arbitrary",),
                                             vmem_limit_bytes=V7X_VMEM_LIMIT),
        name="dsa_sample",
    )(page_table, iq_rows, iw_rows, q_rows, iknew, knew, vnew,
      *([cache_idx_k] * n_pages), *([ck] * n_pages), *([cv] * n_pages))
    o = o.reshape(db, N_KV_HEADS, grp, T_PAD, HEAD_DIM)[:, :, :, :t]
    return o.transpose(0, 3, 1, 2, 4).reshape(db, t, N_HEADS * HEAD_DIM)


def causal_conv(xpad, w_conv):
    c = xpad.shape[-1]
    return lax.conv_general_dilated(xpad, w_conv[:, None, :], window_strides=(1,),
                                    padding='VALID', dimension_numbers=('NWC', 'WIO', 'NWC'),
                                    feature_group_count=c)


def chunk_gated_delta(q, k, v, beta, g, s0):
    b, t, h, dk = q.shape
    dv = v.shape[-1]
    c = min(DELTA_CHUNK, t)
    n = -(-t // c)
    pad = n * c - t

    def chunks(a):
        a = jnp.pad(a, [(0, 0), (0, pad)] + [(0, 0)] * (a.ndim - 2))
        a = a.reshape((b, n, c) + a.shape[2:])
        return jnp.moveaxis(a, (1, 3), (0, 2))

    qc, kc, vc, bc, gc = [chunks(a) for a in (q, k, v, beta, g)]
    gc = jnp.cumsum(gc, axis=-1)
    tri = jnp.tril(jnp.ones((c, c), bool))
    strict = jnp.tril(jnp.ones((c, c), bool), -1)
    decay = jnp.exp(jnp.where(tri, gc[..., :, None] - gc[..., None, :], -jnp.inf))
    kb = kc * bc[..., None]
    a_low = jnp.where(strict, jnp.einsum('nbhid,nbhjd->nbhij', kb, kc) * decay, 0.0)
    rhs = jnp.concatenate([vc * bc[..., None], kb * jnp.exp(gc)[..., None]], axis=-1)
    sol = lax.linalg.triangular_solve(a_low + jnp.eye(c, dtype=F32), rhs, left_side=True,
                                      lower=True, unit_diagonal=True)
    u, w = sol[..., :dv], sol[..., dv:]
    qk = jnp.where(tri, jnp.einsum('nbhid,nbhjd->nbhij', qc, kc) * decay, 0.0)

    def step(state, xs):
        q_i, k_i, u_i, w_i, g_i, qk_i = xs
        v_corr = u_i - jnp.einsum('bhck,bhkv->bhcv', w_i, state)
        o_i = (jnp.einsum('bhck,bhkv->bhcv', q_i * jnp.exp(g_i)[..., None], state)
               + jnp.einsum('bhij,bhjv->bhiv', qk_i, v_corr))
        g_last = g_i[..., -1:]
        state = (state * jnp.exp(g_last)[..., None]
                 + jnp.einsum('bhck,bhcv->bhkv', k_i * jnp.exp(g_last - g_i)[..., None], v_corr))
        return state, o_i

    s_final, o = lax.scan(step, s0, (qc, kc, u, w, gc, qk))
    o = jnp.moveaxis(o, (0, 2), (1, 3)).reshape(b, n * c, h, dv)[:, :t]
    return o, s_final


def gated_delta_branch(qkv, b_d, a_d, z, conv_buf, s0, w_conv, a_log, dt_bias, norm_g):
    b, t, _ = qkv.shape
    xpad = jnp.concatenate([conv_buf, qkv], axis=1)
    new_buf = xpad[:, -(CONV_WIDTH - 1):]
    hc = jax.nn.silu(causal_conv(xpad, w_conv))
    q, k, v = jnp.split(hc, (DELTA_QK_WIDTH, 2 * DELTA_QK_WIDTH), axis=-1)
    q = l2norm(q.reshape(b, t, N_DELTA_HEADS, DELTA_DK)) * DELTA_DK ** -0.5
    k = l2norm(k.reshape(b, t, N_DELTA_HEADS, DELTA_DK))
    v = v.reshape(b, t, N_DELTA_HEADS, DELTA_DV)
    beta = jax.nn.sigmoid(b_d)
    g = -jnp.exp(a_log) * jax.nn.softplus(a_d + dt_bias)
    o, s_new = chunk_gated_delta(q, k, v, beta, g, s0)
    o = o * lax.rsqrt(jnp.mean(o * o, -1, keepdims=True) + 1e-6) * norm_g
    o = o * jax.nn.silu(z.reshape(b, t, N_DELTA_HEADS, DELTA_DV))
    return o.reshape(b, t, DELTA_V_WIDTH), s_new, new_buf


DELTA_TOKEN_BLOCK = 256


def _delta_pre_kernel(x_ref, halo_ref, w_ref, q_ref, k_ref, v_ref, xe_ref):
    i = pl.program_id(0)
    tb = x_ref.shape[0]
    xe_ref[0:SUBLANES, :] = jnp.where(i == 0, 0.0, halo_ref[...])
    xe_ref[SUBLANES:SUBLANES + tb, :] = x_ref[...]
    outs = (q_ref, k_ref, v_ref)
    for col in range(3 * N_DELTA_HEADS):
        cs = slice(col * DELTA_DK, (col + 1) * DELTA_DK)
        y = sum(xe_ref[SUBLANES - (CONV_WIDTH - 1) + tap:SUBLANES - (CONV_WIDTH - 1) + tap + tb, cs] * w_ref[tap:tap + 1, cs]
                for tap in range(CONV_WIDTH))
        y = y * jax.nn.sigmoid(y)
        if col < 2 * N_DELTA_HEADS:
            y = y * lax.rsqrt(jnp.sum(y * y, axis=1, keepdims=True) + 1e-6)
        if col < N_DELTA_HEADS:
            y = y * DELTA_DK ** -0.5
        for cc in range(tb // DELTA_CHUNK):
            outs[col // N_DELTA_HEADS][cc, col % N_DELTA_HEADS] = y[cc * DELTA_CHUNK:(cc + 1) * DELTA_CHUNK]


def delta_pre(qkv, t, w_conv):
    c = w_conv.shape[1]
    tb = DELTA_TOKEN_BLOCK
    assert t % tb == 0 and tb % DELTA_CHUNK == 0 and DELTA_DK == DELTA_DV
    per_blk = tb // DELTA_CHUNK
    out_spec = pl.BlockSpec((per_blk, N_DELTA_HEADS, DELTA_CHUNK, DELTA_DK), lambda i: (i, 0, 0, 0))
    out_shape = jax.ShapeDtypeStruct((t // DELTA_CHUNK, N_DELTA_HEADS, DELTA_CHUNK, DELTA_DK), F32)
    return pl.pallas_call(
        _delta_pre_kernel,
        grid=(t // tb,),
        in_specs=[pl.BlockSpec((tb, c), lambda i: (i, 0)),
                  pl.BlockSpec((SUBLANES, c), lambda i: (jnp.maximum(i * (tb // SUBLANES) - 1, 0), 0)),
                  pl.BlockSpec((CONV_WIDTH, c), lambda i: (0, 0))],
        out_specs=[out_spec] * 3,
        out_shape=[out_shape] * 3,
        scratch_shapes=[pltpu.VMEM((tb + SUBLANES, c), F32)],
        compiler_params=pltpu.CompilerParams(dimension_semantics=("arbitrary",),
                                             vmem_limit_bytes=V7X_VMEM_LIMIT),
        name="delta_pre",
    )(qkv, qkv, w_conv)


def _delta_scan_kernel(wq_ref, kt_ref, u_ref, qk_ref, el_ref, z_ref, ng_ref, o_ref, sfin_ref, s_ref):
    n = pl.program_id(0)
    c = DELTA_CHUNK

    @pl.when(n == 0)
    def _():
        s_ref[...] = jnp.zeros(s_ref.shape, F32)

    for h in range(N_DELTA_HEADS):
        s = s_ref[h]
        ws = jnp.dot(wq_ref[h], s.astype(BF16), preferred_element_type=F32)
        v_corr = (u_ref[h] - ws[:c]).astype(BF16)
        o = ws[c:] + jnp.dot(qk_ref[h], v_corr, preferred_element_type=F32)
        s_ref[h] = s * el_ref[h] + jnp.dot(kt_ref[h], v_corr, preferred_element_type=F32)
        o = o * lax.rsqrt(jnp.mean(o * o, axis=1, keepdims=True) + 1e-6) * ng_ref[...]
        zz = z_ref[:, h * DELTA_DV:(h + 1) * DELTA_DV]
        o_ref[:, h * DELTA_DV:(h + 1) * DELTA_DV] = (o * (zz * jax.nn.sigmoid(zz))).astype(o_ref.dtype)

    @pl.when(n == pl.num_programs(0) - 1)
    def _():
        sfin_ref[...] = s_ref[...]


def prompt_delta_branch(p_delta, p_gate, t, w_conv, a_log, dt_bias, norm_g):
    assert t >= CONV_WIDTH - 1
    c, h = DELTA_CHUNK, N_DELTA_HEADS
    n = t // c
    new_buf = p_delta[t - (CONV_WIDTH - 1):t, :CONV_CH]
    qc, kc, vc = delta_pre(p_delta, t, w_conv)
    beta = jax.nn.sigmoid(p_delta[:t, CONV_CH:CONV_CH + h])
    g = -jnp.exp(a_log) * jax.nn.softplus(p_delta[:t, CONV_CH + h:CONV_CH + 2 * h] + dt_bias)
    bc = beta.reshape(n, c, h).transpose(0, 2, 1)
    gc = jnp.cumsum(g.reshape(n, c, h).transpose(0, 2, 1), axis=-1)
    tri = jnp.tril(jnp.ones((c, c), bool))
    strict = jnp.tril(jnp.ones((c, c), bool), -1)
    decay = jnp.exp(jnp.where(tri, gc[..., :, None] - gc[..., None, :], -jnp.inf))
    kb = kc * bc[..., None]
    a_low = jnp.where(strict, jnp.einsum('nhid,nhjd->nhij', kb, kc) * decay, 0.0)
    rhs = jnp.concatenate([vc * bc[..., None], kb * jnp.exp(gc)[..., None]], axis=-1)
    sol = lax.linalg.triangular_solve(a_low + jnp.eye(c, dtype=F32), rhs, left_side=True,
                                      lower=True, unit_diagonal=True)
    u, w = sol[..., :DELTA_DV], sol[..., DELTA_DV:]
    qk = jnp.where(tri, jnp.einsum('nhid,nhjd->nhij', qc, kc) * decay, 0.0)
    g_last = gc[..., -1:]
    wq = jnp.concatenate([w, qc * jnp.exp(gc)[..., None]], axis=-2).astype(BF16)
    kt = jnp.swapaxes(kc * jnp.exp(g_last - gc)[..., None], -1, -2).astype(BF16)
    e_last = jnp.broadcast_to(jnp.exp(g_last)[..., None], (n, h, 1, DELTA_DV))

    def per_chunk(*blk):
        return pl.BlockSpec((None,) + blk, lambda i: (i,) + (0,) * len(blk))

    d_out, s_fin = pl.pallas_call(
        _delta_scan_kernel,
        grid=(n,),
        in_specs=[per_chunk(h, 2 * c, DELTA_DK), per_chunk(h, DELTA_DK, c), per_chunk(h, c, DELTA_DV),
                  per_chunk(h, c, c), per_chunk(h, 1, DELTA_DV),
                  pl.BlockSpec((c, h * DELTA_DV), lambda i: (i, 0)),
                  pl.BlockSpec((1, DELTA_DV), lambda i: (0, 0))],
        out_specs=[pl.BlockSpec((c, h * DELTA_DV), lambda i: (i, 0)),
                   pl.BlockSpec((h, DELTA_DK, DELTA_DV), lambda i: (0, 0, 0))],
        out_shape=[jax.ShapeDtypeStruct((t, h * DELTA_DV), BF16),
                   jax.ShapeDtypeStruct((h, DELTA_DK, DELTA_DV), F32)],
        scratch_shapes=[pltpu.VMEM((h, DELTA_DK, DELTA_DV), F32)],
        compiler_params=pltpu.CompilerParams(dimension_semantics=("arbitrary",),
                                             vmem_limit_bytes=V7X_VMEM_LIMIT),
        name="delta_scan",
    )(wq, kt, u, qk.astype(BF16), e_last, p_gate, norm_g.reshape(1, DELTA_DV))
    return d_out, s_fin, new_buf


def routed_experts(xt, expert_idx, gate, w_gate, w_up, w_down):
    n, d = xt.shape
    k = expert_idx.shape[1]
    n_exp = w_gate.shape[0]
    blk = EXPERT_BLOCK
    flat_e = expert_idx.reshape(-1)
    order = jnp.argsort(flat_e).astype(jnp.int32)
    counts = jnp.bincount(flat_e, length=n_exp).astype(jnp.int32)
    padded = (counts + blk - 1) // blk * blk
    pad_end = jnp.cumsum(padded)
    pad_start = pad_end - padded
    start = jnp.cumsum(counts) - counts
    n_blocks = -(-(n * k) // blk) + n_exp
    rows = n_blocks * blk
    blk_row0 = jnp.arange(n_blocks, dtype=jnp.int32) * blk
    block_expert = jnp.minimum(jnp.sum(pad_end[None, :] <= blk_row0[:, None], axis=1), n_exp - 1).astype(jnp.int32)
    n_used = (pad_end[-1:] // blk).astype(jnp.int32)
    r = jnp.arange(rows, dtype=jnp.int32)
    rank = (r.reshape(n_blocks, blk) - pad_start[block_expert][:, None])
    real = (rank < counts[block_expert][:, None]).reshape(rows)
    src = order[jnp.clip(start[block_expert][:, None] + rank, 0, n * k - 1).reshape(rows)]
    row_token = jnp.where(real, src // k, r % n)
    row_gate = jnp.where(real, gate.reshape(-1)[src], 0.0)
    xb = xt.astype(BF16)
    grp_blocks = -(-n_blocks // ROW_GROUPS)
    h_parts = []
    for b0 in range(0, n_blocks, grp_blocks):
        b1 = min(b0 + grp_blocks, n_blocks)
        x_rows = xb[row_token[b0 * blk:b1 * blk]]
        h_parts.append(grouped_swiglu_up(x_rows, block_expert[b0:b1], jnp.clip(n_used - b0, 0, b1 - b0),
                                         w_gate, w_up, blk))
    h_rows = jnp.concatenate(h_parts, axis=0)
    y_rows = grouped_down(h_rows, row_gate[:, None], block_expert, n_used, w_down, blk)
    slot = jnp.argsort(order).astype(jnp.int32)
    e_of = flat_e.astype(jnp.int32)
    pos = pad_start[e_of] + slot - start[e_of]
    return jnp.sum(y_rows[pos].reshape(n, k, d), axis=1)


def moe(xt, w_router, router_bias, w_exp_gate, w_exp_up, w_exp_down, w_sh_gate, w_sh_up, w_sh_down):
    n = xt.shape[0]
    scores = jax.nn.sigmoid(jnp.matmul(xt, w_router, preferred_element_type=F32))
    biased = scores + router_bias
    per_group = N_EXPERTS // N_GROUPS
    group_score = lax.top_k(biased.reshape(n, N_GROUPS, per_group), 2)[0].sum(-1)
    _, top_groups = lax.top_k(group_score, TOPK_GROUPS)
    group_keep = jnp.any(top_groups[:, :, None] == jnp.arange(N_GROUPS)[None, None, :], axis=1)
    biased = jnp.where(jnp.repeat(group_keep, per_group, axis=1), biased, -jnp.inf)
    _, expert_idx = lax.top_k(biased, TOP_K)
    gate = jnp.take_along_axis(scores, expert_idx, axis=1)
    gate = gate / jnp.sum(gate, -1, keepdims=True) * ROUTED_SCALE
    routed = routed_experts(xt, expert_idx, gate, w_exp_gate, w_exp_up, w_exp_down)
    xb = xt.astype(BF16)
    sg = matmul(xb, w_sh_gate)
    su = matmul(xb, w_sh_up)
    shared = matmul((jax.nn.silu(sg) * su).astype(BF16), w_sh_down)
    return routed + shared


def kernel(x_prompt, x_sample, cache_k, cache_v, cache_idx_k, page_table, state_delta, state_conv, w_in, w_conv, a_log, dt_bias, delta_norm_g, w_branch_attn, w_branch_delta, w_out, ln1_g, ln1_b, w_router, router_bias, w_exp_gate, w_exp_up, w_exp_down, w_sh_gate, w_sh_up, w_sh_down, ln2_g, ln2_b):
    b, s, d = x_prompt.shape
    db, t = x_sample.shape[:2]
    assert DEPTH == 1 and b == 1
    past = page_table.shape[1] * cache_k.shape[2]
    n_p, n_s = b * s, db * t
    (l_in, l_conv, l_alog, l_dtb, l_ng, l_ba, l_bd, l_out, l_g1, l_b1,
     l_r, l_rb, l_eg, l_eu, l_ed, l_sg, l_su, l_sd, l_g2, l_b2) = [a[0] for a in (
         w_in, w_conv, a_log, dt_bias, delta_norm_g, w_branch_attn, w_branch_delta, w_out,
         ln1_g, ln1_b, w_router, router_bias, w_exp_gate, w_exp_up, w_exp_down,
         w_sh_gate, w_sh_up, w_sh_down, ln2_g, ln2_b)]

    x_all = jnp.concatenate([x_prompt.reshape(n_p, d), x_sample.reshape(n_s, d)], axis=0)
    xb = x_all.astype(BF16)
    w_in_t = jnp.swapaxes(l_in, 0, 1)
    p_attn = matmul_rows(xb, w_in_t, 0, ATTN_COLS)
    p_delta = matmul_rows(xb, w_in_t, IN_SPLITS[5], DELTA_COLS)
    p_gate = matmul_rows(xb, w_in_t, IN_SPLITS[8], GATE_COLS)
    ap = attention_inputs(p_attn[:n_p].reshape(b, s, -1), jnp.arange(s))
    at = attention_inputs(p_attn[n_p:].reshape(db, t, -1), past + jnp.arange(t))

    a_p = prompt_sparse_attention(ap.q, ap.k, ap.v, ap.iq, ap.iw, ap.ik)
    a_s = sample_sparse_attention(at.q, at.k, at.v, at.iq, at.iw, at.ik,
                                  cache_k[0], cache_v[0], cache_idx_k[0], page_table)
    d_p, sd_p, sc_p = prompt_delta_branch(p_delta, p_gate, n_p, l_conv, l_alog, l_dtb, l_ng)
    pd_s = p_delta[n_p:].reshape(db, t, -1)
    nh = N_DELTA_HEADS
    d_s, sd_s, sc_s = gated_delta_branch(pd_s[..., :CONV_CH], pd_s[..., CONV_CH:CONV_CH + nh], pd_s[..., CONV_CH + nh:],
                                         p_gate[n_p:, :DELTA_V_WIDTH].reshape(db, t, -1), state_conv[0], state_delta[0],
                                         l_conv, l_alog, l_dtb, l_ng)

    a_all = jnp.concatenate([a_p.reshape(n_p, -1).astype(BF16), a_s.reshape(n_s, -1).astype(BF16)], axis=0)
    d_all = jnp.concatenate([d_p, d_s.reshape(n_s, -1).astype(BF16)], axis=0)
    merged = merge_branches(a_all, d_all, p_gate, DELTA_V_WIDTH, l_ba, l_bd)
    h = layer_norm(ALPHA * x_all + matmul(merged, l_out), l_g1, l_b1)
    y = layer_norm(ALPHA * h + moe(h, l_r, l_rb, l_eg, l_eu, l_ed, l_sg, l_su, l_sd), l_g2, l_b2)
    return (y[:n_p].reshape(b, s, d), y[n_p:].reshape(db, t, d),
            ap.k[None], ap.v[None], ap.ik[None], sd_p[None, None], sc_p[None, None],
            at.k[None], at.v[None], at.ik[None], sd_s[None], sc_s[None])
```

```python
import collections
import functools

import jax
import jax.numpy as jnp
import numpy as np
from jax import lax
from jax.experimental import pallas as pl
from jax.experimental.pallas import tpu as pltpu

D_MODEL = 4096
DEPTH = 1
N_HEADS = 16
N_KV_HEADS = 4
HEAD_DIM = 128
ROPE_THETA = 500000.0
N_IDX_HEADS = 16
IDX_DIM = 128
IDX_TOPK = 256
Q_BLOCK = 128
N_DELTA_HEADS = 16
DELTA_DK = 128
DELTA_DV = 128
CONV_WIDTH = 4
DELTA_CHUNK = 64
N_EXPERTS = 64
TOP_K = 8
N_GROUPS = 8
TOPK_GROUPS = 4
EXPERT_DIM = 1024
ROUTED_SCALE = 2.5
EXPERT_BLOCK = 256
ROW_GROUPS = 2
ALPHA = (2 * DEPTH) ** 0.25
LN_EPS = 1e-5
ATTN_WIDTH = N_HEADS * HEAD_DIM
KV_WIDTH = N_KV_HEADS * HEAD_DIM
IDXQ_WIDTH = N_IDX_HEADS * IDX_DIM
DELTA_QK_WIDTH = N_DELTA_HEADS * DELTA_DK
DELTA_V_WIDTH = N_DELTA_HEADS * DELTA_DV
CONV_CH = 2 * DELTA_QK_WIDTH + DELTA_V_WIDTH
IN_SIZES = (ATTN_WIDTH, KV_WIDTH, KV_WIDTH, IDXQ_WIDTH, IDX_DIM, N_IDX_HEADS, CONV_CH,
            N_DELTA_HEADS, N_DELTA_HEADS, DELTA_V_WIDTH, 2 * D_MODEL)
IN_SPLITS = tuple(int(s) for s in np.cumsum(IN_SIZES)[:-1])

V7X_VMEM_LIMIT = 56 * 1024 * 1024
BF16 = jnp.bfloat16
F32 = jnp.float32
SUBLANES = 8


def _mm_kernel(x_ref, w_ref, o_ref, wbf_ref):
    @pl.when(pl.program_id(1) == 0)
    def _():
        wbf_ref[...] = w_ref[...].astype(BF16)

    o_ref[...] = jnp.dot(x_ref[...], wbf_ref[...], preferred_element_type=F32).astype(o_ref.dtype)


def _row_tile(m):
    for tm in (1024, 1088, 512, 256, 128):
        if m % tm == 0:
            return tm
    raise ValueError(f"no row tile for {m} rows")


def matmul(x, w, out_dtype=F32, tn=512):
    m, k = x.shape
    n = w.shape[1]
    tm = _row_tile(m)
    tn = min(tn, n)
    return pl.pallas_call(
        _mm_kernel,
        grid=(pl.cdiv(n, tn), m // tm),
        in_specs=[pl.BlockSpec((tm, k), lambda j, i: (i, 0)),
                  pl.BlockSpec((k, tn), lambda j, i: (0, j))],
        out_specs=pl.BlockSpec((tm, tn), lambda j, i: (i, j)),
        out_shape=jax.ShapeDtypeStruct((m, n), out_dtype),
        scratch_shapes=[pltpu.VMEM((k, tn), BF16)],
        compiler_params=pltpu.CompilerParams(dimension_semantics=("arbitrary", "arbitrary"),
                                             vmem_limit_bytes=V7X_VMEM_LIMIT),
        name="dense_matmul",
    )(x, w)


def _swiglu_kernel(x_ref, wg_ref, wu_ref, o_ref, wg_bf, wu_bf):
    @pl.when(pl.program_id(1) == 0)
    def _():
        wg_bf[...] = wg_ref[...].astype(BF16)
        wu_bf[...] = wu_ref[...].astype(BF16)

    x = x_ref[...]
    g = jnp.dot(x, wg_bf[...], preferred_element_type=F32)
    u = jnp.dot(x, wu_bf[...], preferred_element_type=F32)
    o_ref[...] = (g * jax.nn.sigmoid(g) * u).astype(o_ref.dtype)


def swiglu_up(x, w_gate, w_up, tn=256):
    m, k = x.shape
    f = w_gate.shape[1]
    tm = _row_tile(m)
    tn = min(tn, f)
    assert f % tn == 0 and w_up.shape == w_gate.shape
    return pl.pallas_call(
        _swiglu_kernel,
        grid=(f // tn, m // tm),
        in_specs=[pl.BlockSpec((tm, k), lambda j, i: (i, 0)),
                  pl.BlockSpec((k, tn), lambda j, i: (0, j)),
                  pl.BlockSpec((k, tn), lambda j, i: (0, j))],
        out_specs=pl.BlockSpec((tm, tn), lambda j, i: (i, j)),
        out_shape=jax.ShapeDtypeStruct((m, f), BF16),
        scratch_shapes=[pltpu.VMEM((k, tn), BF16), pltpu.VMEM((k, tn), BF16)],
        compiler_params=pltpu.CompilerParams(dimension_semantics=("arbitrary", "arbitrary"),
                                             vmem_limit_bytes=V7X_VMEM_LIMIT),
        name="swiglu_up",
    )(x, w_gate, w_up)


def _merge_kernel(a_ref, d_ref, ga_ref, gd_ref, wa_ref, wd_ref, o_ref, wa_bf, wd_bf):
    @pl.when(pl.program_id(1) == 0)
    def _():
        wa_bf[...] = wa_ref[...].astype(BF16)
        wd_bf[...] = wd_ref[...].astype(BF16)

    pa = jnp.dot(a_ref[...], wa_bf[...], preferred_element_type=F32)
    pd = jnp.dot(d_ref[...], wd_bf[...], preferred_element_type=F32)
    o_ref[...] = (jax.nn.sigmoid(ga_ref[...]) * pa + jax.nn.sigmoid(gd_ref[...]) * pd).astype(o_ref.dtype)


def merge_branches(a, d, p_gate, gate_col0, w_a, w_d, tn=512):
    m, k = a.shape
    n = w_a.shape[1]
    tm = _row_tile(m)
    assert gate_col0 % tn == 0 and n % tn == 0 and w_d.shape == w_a.shape and d.shape == a.shape
    ga0, gd0 = gate_col0 // tn, (gate_col0 + n) // tn
    return pl.pallas_call(
        _merge_kernel,
        grid=(n // tn, m // tm),
        in_specs=[pl.BlockSpec((tm, k), lambda j, i: (i, 0)),
                  pl.BlockSpec((tm, k), lambda j, i: (i, 0)),
                  pl.BlockSpec((tm, tn), lambda j, i: (i, ga0 + j)),
                  pl.BlockSpec((tm, tn), lambda j, i: (i, gd0 + j)),
                  pl.BlockSpec((k, tn), lambda j, i: (0, j)),
                  pl.BlockSpec((k, tn), lambda j, i: (0, j))],
        out_specs=pl.BlockSpec((tm, tn), lambda j, i: (i, j)),
        out_shape=jax.ShapeDtypeStruct((m, n), BF16),
        scratch_shapes=[pltpu.VMEM((k, tn), BF16), pltpu.VMEM((k, tn), BF16)],
        compiler_params=pltpu.CompilerParams(dimension_semantics=("arbitrary", "arbitrary"),
                                             vmem_limit_bytes=V7X_VMEM_LIMIT),
        name="merge_branches",
    )(a, d, p_gate, p_gate, w_a, w_d)


def _mm_rows_kernel(x_ref, wt_ref, o_ref, wbf_ref):
    @pl.when(pl.program_id(1) == 0)
    def _():
        wbf_ref[...] = wt_ref[...].astype(BF16)

    o_ref[...] = lax.dot_general(x_ref[...], wbf_ref[...], (((1,), (1,)), ((), ())),
                                 preferred_element_type=F32).astype(o_ref.dtype)


def matmul_rows(x, wt, row0, nrows, out_dtype=F32, tn=512):
    m, k = x.shape
    tm = _row_tile(m)
    tn = min(tn, nrows)
    assert row0 % SUBLANES == 0 and wt.shape[1] == k
    w_spec = pl.BlockSpec((pl.Element(tn, (0, tn)), pl.Element(k)),
                          lambda j, i: (pl.multiple_of(row0 + j * tn, SUBLANES), 0))
    return pl.pallas_call(
        _mm_rows_kernel,
        grid=(pl.cdiv(nrows, tn), m // tm),
        in_specs=[pl.BlockSpec((tm, k), lambda j, i: (i, 0)), w_spec],
        out_specs=pl.BlockSpec((tm, tn), lambda j, i: (i, j)),
        out_shape=jax.ShapeDtypeStruct((m, nrows), out_dtype),
        scratch_shapes=[pltpu.VMEM((tn, k), BF16)],
        compiler_params=pltpu.CompilerParams(dimension_semantics=("arbitrary", "arbitrary"),
                                             vmem_limit_bytes=V7X_VMEM_LIMIT),
        name="dense_matmul_rows",
    )(x, wt)


def _expert_changed(be_ref, i):
    prev = be_ref[jnp.maximum(i - 1, 0)]
    return jnp.logical_or(i == 0, be_ref[i] != prev)


def _moe_up_kernel(be_ref, nu_ref, x_ref, wg_ref, wu_ref, *rest):
    h_ref, wg_bf, wu_bf = rest[-3:]
    i = pl.program_id(1)

    @pl.when(i < nu_ref[0])
    def _():
        @pl.when(_expert_changed(be_ref, i))
        def _():
            wg_bf[...] = wg_ref[...].astype(BF16)
            wu_bf[...] = wu_ref[...].astype(BF16)

        x = x_ref[...]
        g = jnp.dot(x, wg_bf[...], preferred_element_type=F32)
        u = jnp.dot(x, wu_bf[...], preferred_element_type=F32)
        h_ref[...] = (g * jax.nn.sigmoid(g) * u).astype(h_ref.dtype)


def _moe_down_kernel(be_ref, nu_ref, h_ref, gate_ref, wd_ref, y_ref, wd_bf):
    i = pl.program_id(1)

    @pl.when(i < nu_ref[0])
    def _():
        @pl.when(_expert_changed(be_ref, i))
        def _():
            wd_bf[...] = wd_ref[...].astype(BF16)

        y = jnp.dot(h_ref[...], wd_bf[...], preferred_element_type=F32)
        y_ref[...] = (y * gate_ref[...]).astype(y_ref.dtype)


def grouped_swiglu_up(x_rows, block_expert, n_used, w_gate, w_up, blk, block0, total_blocks, h_prev=None, tf=512):
    rows, d = x_rows.shape
    f = w_gate.shape[2]
    tf = min(tf, f)
    n_blocks = rows // blk
    in_specs = [pl.BlockSpec((blk, d), lambda j, i, be, nu: (i, 0)),
                pl.BlockSpec((None, d, tf), lambda j, i, be, nu: (be[i], 0, j)),
                pl.BlockSpec((None, d, tf), lambda j, i, be, nu: (be[i], 0, j))]
    operands = [block_expert, n_used, x_rows, w_gate, w_up]
    aliases = {}
    if h_prev is not None:
        in_specs.append(pl.BlockSpec(memory_space=pl.ANY))
        operands.append(h_prev)
        aliases = {len(operands) - 1: 0}
    grid_spec = pltpu.PrefetchScalarGridSpec(
        num_scalar_prefetch=2,
        grid=(f // tf, n_blocks),
        in_specs=in_specs,
        out_specs=pl.BlockSpec((blk, tf), lambda j, i, be, nu: (block0 + i, j)),
        scratch_shapes=[pltpu.VMEM((d, tf), BF16), pltpu.VMEM((d, tf), BF16)],
    )
    return pl.pallas_call(
        _moe_up_kernel, grid_spec=grid_spec,
        out_shape=jax.ShapeDtypeStruct((total_blocks * blk, f), BF16),
        input_output_aliases=aliases,
        compiler_params=pltpu.CompilerParams(dimension_semantics=("arbitrary", "arbitrary"),
                                             vmem_limit_bytes=V7X_VMEM_LIMIT),
        name="moe_up",
    )(*operands)


def grouped_down(h_rows, row_gate, block_expert, n_used, w_down, blk, tn=2048):
    rows, f = h_rows.shape
    d = w_down.shape[2]
    tn = min(tn, d)
    n_blocks = rows // blk
    grid_spec = pltpu.PrefetchScalarGridSpec(
        num_scalar_prefetch=2,
        grid=(d // tn, n_blocks),
        in_specs=[pl.BlockSpec((blk, f), lambda j, i, be, nu: (i, 0)),
                  pl.BlockSpec((blk, 1), lambda j, i, be, nu: (i, 0)),
                  pl.BlockSpec((None, f, tn), lambda j, i, be, nu: (be[i], 0, j))],
        out_specs=pl.BlockSpec((blk, tn), lambda j, i, be, nu: (i, j)),
        scratch_shapes=[pltpu.VMEM((f, tn), BF16)],
    )
    return pl.pallas_call(
        _moe_down_kernel, grid_spec=grid_spec,
        out_shape=jax.ShapeDtypeStruct((rows, d), BF16),
        compiler_params=pltpu.CompilerParams(dimension_semantics=("arbitrary", "arbitrary"),
                                             vmem_limit_bytes=V7X_VMEM_LIMIT),
        name="moe_down",
    )(block_expert, n_used, h_rows, row_gate, w_down)


def layer_norm(x, g, b):
    xc = x - jnp.mean(x, -1, keepdims=True)
    var = jnp.mean(xc * xc, -1, keepdims=True)
    return xc * lax.rsqrt(var + LN_EPS) * g + b


def l2norm(x):
    return x * lax.rsqrt(jnp.sum(x * x, -1, keepdims=True) + 1e-6)


def partial_rope(x, pos):
    rot = x.shape[-1] // 4
    half = rot // 2
    inv_freq = ROPE_THETA ** (-jnp.arange(half, dtype=F32) / half)
    ang = pos.astype(F32)[:, None] * inv_freq[None, :]
    cos = jnp.cos(ang)[:, None, :]
    sin = jnp.sin(ang)[:, None, :]
    x1, x2, rest = x[..., :half], x[..., half:rot], x[..., rot:]
    return jnp.concatenate([x1 * cos - x2 * sin, x2 * cos + x1 * sin, rest], axis=-1)


ATTN_COLS = IN_SPLITS[5]
DELTA_COLS = IN_SPLITS[8] - IN_SPLITS[5]
GATE_COLS = int(sum(IN_SIZES)) - IN_SPLITS[8]
AttnInputs = collections.namedtuple("AttnInputs", "q k v iq ik iw")


def attention_inputs(p, pos):
    b, t, _ = p.shape
    q, k, v, iq, ik, iw = jnp.split(p, IN_SPLITS[:5], axis=-1)
    q = partial_rope(q.reshape(b, t, N_HEADS, HEAD_DIM), pos)
    k = partial_rope(k.reshape(b, t, N_KV_HEADS, HEAD_DIM), pos)
    v = v.reshape(b, t, N_KV_HEADS, HEAD_DIM)
    iq = partial_rope(iq.reshape(b, t, N_IDX_HEADS, IDX_DIM), pos)
    ik = partial_rope(ik[:, :, None, :], pos)[:, :, 0, :]
    iw = iw * (N_IDX_HEADS ** -0.5 * IDX_DIM ** -0.5)
    return AttnInputs(q, k, v, iq, ik, iw)


KEY_CHUNK = 512
MASKED = -1e30
SOFTMAX_LOG2_SCALE = HEAD_DIM ** -0.5 * float(np.log2(np.e))
INT32_MIN = -2 ** 31
INT32_MAX = 2 ** 31 - 1
NEG_INF_KEY = int(np.int32(np.uint32(0xFF800000) ^ np.uint32(0x7FFFFFFF)))


def _sort_key(x):
    bits = lax.bitcast_convert_type(x, jnp.int32)
    return bits ^ (jnp.right_shift(bits, 31) & jnp.int32(INT32_MAX))


def _lane_tile(x, n):
    return x if n == 1 else jnp.concatenate([x] * n, axis=1)


def _count_rows(keys_ref, nkc, preds):
    _, rows, kc = keys_ref.shape

    def body(c, parts):
        keys = keys_ref[c]
        out = []
        for pred, part in zip(preds, parts):
            hit = pred(keys, c)
            for j in range(kc // 128):
                part = part + hit[:, j * 128:(j + 1) * 128]
            out.append(part)
        return tuple(out)

    parts = lax.fori_loop(0, nkc, body, tuple(jnp.zeros((rows, 128), F32) for _ in preds))
    return [jnp.broadcast_to(jnp.sum(part, axis=1, keepdims=True), (rows, 128)) for part in parts]


def _topk_selection(keys_ref, nkc, n_sel, radix_bits):
    _, rows, kc = keys_ref.shape
    reps = kc // 128
    col = lax.broadcasted_iota(jnp.int32, (rows, kc), 1)

    def thr_digit(it, carry):
        thr, cnt_thr = carry
        shift = 32 - radix_bits * (it + 1)
        cands = [thr + jnp.left_shift(jnp.int32(j), shift) for j in range(1, 2 ** radix_bits)]
        cands_w = [_lane_tile(cand, reps) for cand in cands]
        cnts = _count_rows(keys_ref, nkc, [lambda keys, c, cw=cw: jnp.where(keys >= cw, 1.0, 0.0) for cw in cands_w])
        for cand, cnt in zip(cands, cnts):
            ok = cnt >= n_sel
            thr = jnp.where(ok, cand, thr)
            cnt_thr = jnp.where(ok, cnt, cnt_thr)
        return thr, cnt_thr

    thr0 = jnp.full((rows, 128), INT32_MIN, jnp.int32)
    cnt0 = jnp.broadcast_to(jnp.asarray(nkc * kc).astype(F32), (rows, 128))
    thr, cnt_thr = lax.fori_loop(0, 32 // radix_bits, thr_digit, (thr0, cnt0))

    short = thr == NEG_INF_KEY
    thr_w = _lane_tile(thr, reps)
    cnt_gt, = _count_rows(keys_ref, nkc, [lambda keys, c: jnp.where(keys > thr_w, 1.0, 0.0)])
    need = n_sel - cnt_gt
    tied = jnp.logical_and(cnt_thr > n_sel, jnp.logical_not(short))

    def tie_limit():
        def idx_bit(it, x):
            cand = x + jnp.left_shift(jnp.int32(1), 30 - it)
            cand_w = _lane_tile(cand, reps)
            cnt, = _count_rows(keys_ref, nkc, [lambda keys, c: jnp.where(
                keys == thr_w, jnp.where(c * kc + col < cand_w, 1.0, 0.0), 0.0)])
            return jnp.where(cnt < need, cand, x)
        return lax.fori_loop(0, 31, idx_bit, jnp.zeros((rows, 128), jnp.int32))

    any_tied = jnp.max(jnp.where(tied, 1.0, 0.0)) > 0.0
    lim = lax.cond(any_tied, tie_limit, lambda: jnp.full((rows, 128), INT32_MAX, jnp.int32))
    lim = jnp.where(short, -1, jnp.where(tied, lim, INT32_MAX))
    return thr, lim


def _selected(keys, index, thr_w, lim_w, yes, no):
    keep_tie = jnp.where(index <= lim_w, yes, no)
    return jnp.where(keys > thr_w, yes, jnp.where(keys == thr_w, keep_tie, no))


def _dsa_prompt_kernel(iq_ref, iw_ref, ikt_ref, q_ref, kt_ref, v_ref, o_ref,
                       keys_ref, m_ref, l_ref, acc_ref, *, n_sel):
    qb, kc = Q_BLOCK, KEY_CHUNK
    reps = kc // 128
    grp = N_HEADS // N_KV_HEADS
    i = pl.program_id(0)
    nkc = (i * qb + qb + kc - 1) // kc
    qpos = i * qb + lax.broadcasted_iota(jnp.int32, (qb, kc), 0)
    col = lax.broadcasted_iota(jnp.int32, (qb, kc), 1)

    def score_chunk(c, carry):
        ikc = ikt_ref[c]
        acc = jnp.zeros((qb, kc), F32)
        for h in range(N_IDX_HEADS):
            s = jnp.dot(iq_ref[h], ikc, preferred_element_type=F32)
            acc = acc + jnp.maximum(s, 0.0) * iw_ref[:, h:h + 1]
        acc = jnp.where(c * kc + col <= qpos, acc, -jnp.inf)
        keys_ref[c] = _sort_key(acc)
        return carry

    lax.fori_loop(0, nkc, score_chunk, 0)

    thr, lim = _topk_selection(keys_ref, nkc, n_sel, radix_bits=1)
    thr_w, lim_w = _lane_tile(thr, reps), _lane_tile(lim, reps)

    m_ref[...] = jnp.full(m_ref.shape, MASKED, F32)
    l_ref[...] = jnp.zeros(l_ref.shape, F32)
    acc_ref[...] = jnp.zeros(acc_ref.shape, F32)
    rows = grp * qb

    def attend_chunk(c, carry):
        bias = _selected(keys_ref[c], c * kc + col, thr_w, lim_w, 0.0, MASKED)
        for n in range(N_KV_HEADS):
            r = pl.ds(n * rows, rows)
            qn = q_ref[n * grp:(n + 1) * grp].reshape(rows, HEAD_DIM)
            s = jnp.dot(qn, kt_ref[n, c], preferred_element_type=F32) * SOFTMAX_LOG2_SCALE
            s = (s.reshape(grp, qb, kc) + bias[None]).reshape(rows, kc)
            m_prev = m_ref[r, :]
            m_new = jnp.maximum(m_prev, jnp.max(s, axis=1, keepdims=True))
            alpha = jnp.exp2(m_prev - m_new)
            p = jnp.exp2(s - _lane_tile(m_new, reps))
            l_ref[r, :] = alpha * l_ref[r, :] + jnp.sum(p, axis=1, keepdims=True)
            acc_ref[r, :] = alpha * acc_ref[r, :] + jnp.dot(p.astype(BF16), v_ref[n, c],
                                                           preferred_element_type=F32)
            m_ref[r, :] = m_new
        return carry

    lax.fori_loop(0, nkc, attend_chunk, 0)
    for h in range(N_HEADS):
        r = pl.ds(h * qb, qb)
        o_ref[:, h * HEAD_DIM:(h + 1) * HEAD_DIM] = (acc_ref[r, :] / l_ref[r, :]).astype(o_ref.dtype)


def prompt_sparse_attention(q, k, v, iq, iw, ik):
    b, s = q.shape[:2]
    assert b == 1 and s % KEY_CHUNK == 0 and KEY_CHUNK % Q_BLOCK == 0 and KEY_CHUNK >= IDX_TOPK
    n_sel = min(IDX_TOPK, s // 4)
    nb, nc, kc = s // Q_BLOCK, s // KEY_CHUNK, KEY_CHUNK

    def head_major(a):
        return a.astype(BF16).reshape(nb, Q_BLOCK, a.shape[2], a.shape[3]).transpose(0, 2, 1, 3)

    ikt = ik.astype(BF16).reshape(nc, kc, IDX_DIM).transpose(0, 2, 1)
    kt = k.astype(BF16).reshape(nc, kc, N_KV_HEADS, HEAD_DIM).transpose(2, 0, 3, 1)
    vc = v.astype(BF16).reshape(nc, kc, N_KV_HEADS, HEAD_DIM).transpose(2, 0, 1, 3)
    resident = dict(pipeline_mode=pl.Buffered(1))
    out = pl.pallas_call(
        functools.partial(_dsa_prompt_kernel, n_sel=n_sel),
        grid=(nb,),
        in_specs=[pl.BlockSpec((None, N_IDX_HEADS, Q_BLOCK, IDX_DIM), lambda i: (i, 0, 0, 0)),
                  pl.BlockSpec((Q_BLOCK, N_IDX_HEADS), lambda i: (i, 0)),
                  pl.BlockSpec((nc, IDX_DIM, kc), lambda i: (0, 0, 0), **resident),
                  pl.BlockSpec((None, N_HEADS, Q_BLOCK, HEAD_DIM), lambda i: (i, 0, 0, 0)),
                  pl.BlockSpec((N_KV_HEADS, nc, HEAD_DIM, kc), lambda i: (0, 0, 0, 0), **resident),
                  pl.BlockSpec((N_KV_HEADS, nc, kc, HEAD_DIM), lambda i: (0, 0, 0, 0), **resident)],
        out_specs=pl.BlockSpec((Q_BLOCK, N_HEADS * HEAD_DIM), lambda i: (i, 0)),
        out_shape=jax.ShapeDtypeStruct((s, N_HEADS * HEAD_DIM), BF16),
        scratch_shapes=[pltpu.VMEM((nc, Q_BLOCK, kc), jnp.int32),
                        pltpu.VMEM((N_HEADS * Q_BLOCK, 128), F32),
                        pltpu.VMEM((N_HEADS * Q_BLOCK, 128), F32),
                        pltpu.VMEM((N_HEADS * Q_BLOCK, HEAD_DIM), F32)],
        compiler_params=pltpu.CompilerParams(dimension_semantics=("arbitrary",),
                                             vmem_limit_bytes=V7X_VMEM_LIMIT),
        name="dsa_prompt",
    )(head_major(iq), iw.reshape(s, N_IDX_HEADS), ikt, head_major(q), kt, vc)
    return out.reshape(b, s, N_HEADS * HEAD_DIM)


T_PAD = 8


def _dsa_sample_kernel(pt_ref, iq_ref, iw_ref, q_ref, iknew_ref, knew_ref, vnew_ref, *rest, n_sel, n_pages):
    del pt_ref
    ik_pages, k_pages, v_pages = rest[:n_pages], rest[n_pages:2 * n_pages], rest[2 * n_pages:3 * n_pages]
    o_ref, keys_ref, s_ref = rest[3 * n_pages:]
    page = ik_pages[0].shape[0]
    kv_rows = k_pages[0].shape[0]
    rows = N_HEADS * T_PAD
    nkc = n_pages + 1
    nt = (((1,), (1,)), ((), ()))
    row_t = lax.broadcasted_iota(jnp.int32, (T_PAD, page), 0)
    col = lax.broadcasted_iota(jnp.int32, (T_PAD, page), 1)

    for c in range(nkc):
        ikc = ik_pages[c][...].astype(BF16) if c < n_pages else iknew_ref[...]
        s = lax.dot_general(iq_ref[...], ikc, nt, preferred_element_type=F32)
        x = jnp.maximum(s, 0.0) * iw_ref[...]
        sc = jnp.sum(x.reshape(T_PAD, N_IDX_HEADS, page), axis=1)
        if c == n_pages:
            sc = jnp.where(col <= row_t, sc, -jnp.inf)
        keys_ref[c] = _sort_key(sc)

    thr, lim = _topk_selection(keys_ref, nkc, n_sel, radix_bits=4)

    scale = SOFTMAX_LOG2_SCALE
    spread = (lax.broadcasted_iota(jnp.int32, (page, kv_rows), 1) // N_KV_HEADS
              == lax.broadcasted_iota(jnp.int32, (page, kv_rows), 0)).astype(BF16)
    own_head = (lax.broadcasted_iota(jnp.int32, (rows, kv_rows), 0) // (rows // N_KV_HEADS)
                == lax.broadcasted_iota(jnp.int32, (rows, kv_rows), 1) % N_KV_HEADS).astype(F32)
    for c in range(nkc):
        kc_ = k_pages[c][...].astype(BF16) if c < n_pages else knew_ref[...]
        s = lax.dot_general(q_ref[...], kc_, nt, preferred_element_type=F32) * scale
        sel = _selected(keys_ref[c], c * page + col, thr, lim, 1.0, 0.0).astype(BF16)
        sel = jnp.dot(sel, spread, preferred_element_type=F32)
        keep = jnp.broadcast_to(sel[None], (N_HEADS, T_PAD, kv_rows)).reshape(rows, kv_rows) * own_head
        s_ref[:, c * kv_rows:(c + 1) * kv_rows] = jnp.where(keep > 0.5, s, MASKED)
    s = s_ref[...]
    p = jnp.exp2(s - jnp.max(s, axis=1, keepdims=True))
    inv_l = 1.0 / jnp.sum(p, axis=1, keepdims=True)
    p = p.astype(BF16)
    o = jnp.zeros((rows, HEAD_DIM), F32)
    for c in range(nkc):
        vc_ = v_pages[c][...].astype(BF16) if c < n_pages else vnew_ref[...]
        o = o + jnp.dot(p[:, c * kv_rows:(c + 1) * kv_rows], vc_, preferred_element_type=F32)
    o_ref[...] = o * inv_l


def sample_sparse_attention(q, k_new, v_new, iq, iw, ik_new, cache_k, cache_v, cache_idx_k, page_table):
    db, t = q.shape[:2]
    n_pool, page = cache_k.shape[:2]
    n_pages = page_table.shape[1]
    n_keys = n_pages * page + t
    n_sel = min(IDX_TOPK, n_keys // 4)
    grp = N_HEADS // N_KV_HEADS
    assert t <= T_PAD <= page and page == 128 and HEAD_DIM == 128 and IDX_DIM == 128

    def pad_t(a, to):
        return jnp.pad(a, [(0, 0), (0, to - a.shape[1])] + [(0, 0)] * (a.ndim - 2))

    kv_rows = page * N_KV_HEADS
    rows = N_HEADS * T_PAD
    iq_rows = pad_t(iq, T_PAD).astype(BF16).reshape(db, T_PAD * N_IDX_HEADS, IDX_DIM)
    iw_rows = pad_t(iw, T_PAD).reshape(db, T_PAD * N_IDX_HEADS, 1)
    q_rows = pad_t(q, T_PAD).astype(BF16).transpose(0, 2, 1, 3).reshape(db, rows, HEAD_DIM)
    iknew = pad_t(ik_new, page).astype(BF16)
    knew = pad_t(k_new, page).astype(BF16).reshape(db, kv_rows, HEAD_DIM)
    vnew = pad_t(v_new, page).astype(BF16).reshape(db, kv_rows, HEAD_DIM)
    ck = cache_k.reshape(n_pool, kv_rows, HEAD_DIM)
    cv = cache_v.reshape(n_pool, kv_rows, HEAD_DIM)

    def per_batch(*blk):
        return pl.BlockSpec((None,) + blk, lambda b, pt: (b,) + (0,) * len(blk))

    def paged(nrows):
        return [pl.BlockSpec((None, nrows, HEAD_DIM), lambda b, pt, j=j: (pt[b, j], 0, 0)) for j in range(n_pages)]

    grid_spec = pltpu.PrefetchScalarGridSpec(
        num_scalar_prefetch=1,
        grid=(db,),
        in_specs=[per_batch(T_PAD * N_IDX_HEADS, IDX_DIM), per_batch(T_PAD * N_IDX_HEADS, 1),
                  per_batch(rows, HEAD_DIM), per_batch(page, IDX_DIM), per_batch(kv_rows, HEAD_DIM),
                  per_batch(kv_rows, HEAD_DIM)]
        + paged(page) + paged(kv_rows) + paged(kv_rows),
        out_specs=per_batch(rows, HEAD_DIM),
        scratch_shapes=[pltpu.VMEM((n_pages + 1, T_PAD, page), jnp.int32),
                        pltpu.VMEM((rows, (n_pages + 1) * kv_rows), F32)],
    )
    o = pl.pallas_call(
        functools.partial(_dsa_sample_kernel, n_sel=n_sel, n_pages=n_pages),
        grid_spec=grid_spec,
        out_shape=jax.ShapeDtypeStruct((db, rows, HEAD_DIM), F32),
        compiler_params=pltpu.CompilerParams(dimension_semantics=("arbitrary",),
                                             vmem_limit_bytes=V7X_VMEM_LIMIT),
        name="dsa_sample",
    )(page_table, iq_rows, iw_rows, q_rows, iknew, knew, vnew,
      *([cache_idx_k] * n_pages), *([ck] * n_pages), *([cv] * n_pages))
    o = o.reshape(db, N_KV_HEADS, grp, T_PAD, HEAD_DIM)[:, :, :, :t]
    return o.transpose(0, 3, 1, 2, 4).reshape(db, t, N_HEADS * HEAD_DIM)


def causal_conv(xpad, w_conv):
    c = xpad.shape[-1]
    return lax.conv_general_dilated(xpad, w_conv[:, None, :], window_strides=(1,),
                                    padding='VALID', dimension_numbers=('NWC', 'WIO', 'NWC'),
                                    feature_group_count=c)


def chunk_gated_delta(q, k, v, beta, g, s0):
    b, t, h, dk = q.shape
    dv = v.shape[-1]
    c = min(DELTA_CHUNK, t)
    n = -(-t // c)
    pad = n * c - t

    def chunks(a):
        a = jnp.pad(a, [(0, 0), (0, pad)] + [(0, 0)] * (a.ndim - 2))
        a = a.reshape((b, n, c) + a.shape[2:])
        return jnp.moveaxis(a, (1, 3), (0, 2))

    qc, kc, vc, bc, gc = [chunks(a) for a in (q, k, v, beta, g)]
    gc = jnp.cumsum(gc, axis=-1)
    tri = jnp.tril(jnp.ones((c, c), bool))
    strict = jnp.tril(jnp.ones((c, c), bool), -1)
    decay = jnp.exp(jnp.where(tri, gc[..., :, None] - gc[..., None, :], -jnp.inf))
    kb = kc * bc[..., None]
    a_low = jnp.where(strict, jnp.einsum('nbhid,nbhjd->nbhij', kb, kc) * decay, 0.0)
    rhs = jnp.concatenate([vc * bc[..., None], kb * jnp.exp(gc)[..., None]], axis=-1)
    sol = lax.linalg.triangular_solve(a_low + jnp.eye(c, dtype=F32), rhs, left_side=True,
                                      lower=True, unit_diagonal=True)
    u, w = sol[..., :dv], sol[..., dv:]
    qk = jnp.where(tri, jnp.einsum('nbhid,nbhjd->nbhij', qc, kc) * decay, 0.0)

    def step(state, xs):
        q_i, k_i, u_i, w_i, g_i, qk_i = xs
        v_corr = u_i - jnp.einsum('bhck,bhkv->bhcv', w_i, state)
        o_i = (jnp.einsum('bhck,bhkv->bhcv', q_i * jnp.exp(g_i)[..., None], state)
               + jnp.einsum('bhij,bhjv->bhiv', qk_i, v_corr))
        g_last = g_i[..., -1:]
        state = (state * jnp.exp(g_last)[..., None]
                 + jnp.einsum('bhck,bhcv->bhkv', k_i * jnp.exp(g_last - g_i)[..., None], v_corr))
        return state, o_i

    s_final, o = lax.scan(step, s0, (qc, kc, u, w, gc, qk))
    o = jnp.moveaxis(o, (0, 2), (1, 3)).reshape(b, n * c, h, dv)[:, :t]
    return o, s_final


def gated_delta_branch(qkv, b_d, a_d, z, conv_buf, s0, w_conv, a_log, dt_bias, norm_g):
    b, t, _ = qkv.shape
    xpad = jnp.concatenate([conv_buf, qkv], axis=1)
    new_buf = xpad[:, -(CONV_WIDTH - 1):]
    hc = jax.nn.silu(causal_conv(xpad, w_conv))
    q, k, v = jnp.split(hc, (DELTA_QK_WIDTH, 2 * DELTA_QK_WIDTH), axis=-1)
    q = l2norm(q.reshape(b, t, N_DELTA_HEADS, DELTA_DK)) * DELTA_DK ** -0.5
    k = l2norm(k.reshape(b, t, N_DELTA_HEADS, DELTA_DK))
    v = v.reshape(b, t, N_DELTA_HEADS, DELTA_DV)
    beta = jax.nn.sigmoid(b_d)
    g = -jnp.exp(a_log) * jax.nn.softplus(a_d + dt_bias)
    o, s_new = chunk_gated_delta(q, k, v, beta, g, s0)
    o = o * lax.rsqrt(jnp.mean(o * o, -1, keepdims=True) + 1e-6) * norm_g
    o = o * jax.nn.silu(z.reshape(b, t, N_DELTA_HEADS, DELTA_DV))
    return o.reshape(b, t, DELTA_V_WIDTH), s_new, new_buf


DELTA_TOKEN_BLOCK = 256


def _delta_pre_kernel(x_ref, halo_ref, w_ref, q_ref, k_ref, v_ref, xe_ref):
    i = pl.program_id(0)
    tb = x_ref.shape[0]
    xe_ref[0:SUBLANES, :] = jnp.where(i == 0, 0.0, halo_ref[...])
    xe_ref[SUBLANES:SUBLANES + tb, :] = x_ref[...]
    outs = (q_ref, k_ref, v_ref)
    for col in range(3 * N_DELTA_HEADS):
        cs = slice(col * DELTA_DK, (col + 1) * DELTA_DK)
        y = sum(xe_ref[SUBLANES - (CONV_WIDTH - 1) + tap:SUBLANES - (CONV_WIDTH - 1) + tap + tb, cs] * w_ref[tap:tap + 1, cs]
                for tap in range(CONV_WIDTH))
        y = y * jax.nn.sigmoid(y)
        if col < 2 * N_DELTA_HEADS:
            y = y * lax.rsqrt(jnp.sum(y * y, axis=1, keepdims=True) + 1e-6)
        if col < N_DELTA_HEADS:
            y = y * DELTA_DK ** -0.5
        for cc in range(tb // DELTA_CHUNK):
            outs[col // N_DELTA_HEADS][cc, col % N_DELTA_HEADS] = y[cc * DELTA_CHUNK:(cc + 1) * DELTA_CHUNK]


def delta_pre(qkv, t, w_conv):
    c = w_conv.shape[1]
    tb = DELTA_TOKEN_BLOCK
    assert t % tb == 0 and tb % DELTA_CHUNK == 0 and DELTA_DK == DELTA_DV
    per_blk = tb // DELTA_CHUNK
    out_spec = pl.BlockSpec((per_blk, N_DELTA_HEADS, DELTA_CHUNK, DELTA_DK), lambda i: (i, 0, 0, 0))
    out_shape = jax.ShapeDtypeStruct((t // DELTA_CHUNK, N_DELTA_HEADS, DELTA_CHUNK, DELTA_DK), F32)
    return pl.pallas_call(
        _delta_pre_kernel,
        grid=(t // tb,),
        in_specs=[pl.BlockSpec((tb, c), lambda i: (i, 0)),
                  pl.BlockSpec((SUBLANES, c), lambda i: (jnp.maximum(i * (tb // SUBLANES) - 1, 0), 0)),
                  pl.BlockSpec((CONV_WIDTH, c), lambda i: (0, 0))],
        out_specs=[out_spec] * 3,
        out_shape=[out_shape] * 3,
        scratch_shapes=[pltpu.VMEM((tb + SUBLANES, c), F32)],
        compiler_params=pltpu.CompilerParams(dimension_semantics=("arbitrary",),
                                             vmem_limit_bytes=V7X_VMEM_LIMIT),
        name="delta_pre",
    )(qkv, qkv, w_conv)


def _delta_scan_kernel(wq_ref, kt_ref, u_ref, qk_ref, el_ref, z_ref, ng_ref, o_ref, sfin_ref, s_ref):
    n = pl.program_id(0)
    c = DELTA_CHUNK

    @pl.when(n == 0)
    def _():
        s_ref[...] = jnp.zeros(s_ref.shape, F32)

    for h in range(N_DELTA_HEADS):
        s = s_ref[h]
        ws = jnp.dot(wq_ref[h], s.astype(BF16), preferred_element_type=F32)
        v_corr = (u_ref[h] - ws[:c]).astype(BF16)
        o = ws[c:] + jnp.dot(qk_ref[h], v_corr, preferred_element_type=F32)
        s_ref[h] = s * el_ref[h] + jnp.dot(kt_ref[h], v_corr, preferred_element_type=F32)
        o = o * lax.rsqrt(jnp.mean(o * o, axis=1, keepdims=True) + 1e-6) * ng_ref[...]
        zz = z_ref[:, h * DELTA_DV:(h + 1) * DELTA_DV]
        o_ref[:, h * DELTA_DV:(h + 1) * DELTA_DV] = (o * (zz * jax.nn.sigmoid(zz))).astype(o_ref.dtype)

    @pl.when(n == pl.num_programs(0) - 1)
    def _():
        sfin_ref[...] = s_ref[...]


def prompt_delta_branch(p_delta, p_gate, t, w_conv, a_log, dt_bias, norm_g):
    assert t >= CONV_WIDTH - 1
    c, h = DELTA_CHUNK, N_DELTA_HEADS
    n = t // c
    new_buf = p_delta[t - (CONV_WIDTH - 1):t, :CONV_CH]
    qc, kc, vc = delta_pre(p_delta, t, w_conv)
    beta = jax.nn.sigmoid(p_delta[:t, CONV_CH:CONV_CH + h])
    g = -jnp.exp(a_log) * jax.nn.softplus(p_delta[:t, CONV_CH + h:CONV_CH + 2 * h] + dt_bias)
    bc = beta.reshape(n, c, h).transpose(0, 2, 1)
    gc = jnp.cumsum(g.reshape(n, c, h).transpose(0, 2, 1), axis=-1)
    tri = jnp.tril(jnp.ones((c, c), bool))
    strict = jnp.tril(jnp.ones((c, c), bool), -1)
    decay = jnp.exp(jnp.where(tri, gc[..., :, None] - gc[..., None, :], -jnp.inf))
    kb = kc * bc[..., None]
    a_low = jnp.where(strict, jnp.einsum('nhid,nhjd->nhij', kb, kc) * decay, 0.0)
    rhs = jnp.concatenate([vc * bc[..., None], kb * jnp.exp(gc)[..., None]], axis=-1)
    sol = lax.linalg.triangular_solve(a_low + jnp.eye(c, dtype=F32), rhs, left_side=True,
                                      lower=True, unit_diagonal=True)
    u, w = sol[..., :DELTA_DV], sol[..., DELTA_DV:]
    qk = jnp.where(tri, jnp.einsum('nhid,nhjd->nhij', qc, kc) * decay, 0.0)
    g_last = gc[..., -1:]
    wq = jnp.concatenate([w, qc * jnp.exp(gc)[..., None]], axis=-2).astype(BF16)
    kt = jnp.swapaxes(kc * jnp.exp(g_last - gc)[..., None], -1, -2).astype(BF16)
    e_last = jnp.broadcast_to(jnp.exp(g_last)[..., None], (n, h, 1, DELTA_DV))

    def per_chunk(*blk):
        return pl.BlockSpec((None,) + blk, lambda i: (i,) + (0,) * len(blk))

    d_out, s_fin = pl.pallas_call(
        _delta_scan_kernel,
        grid=(n,),
        in_specs=[per_chunk(h, 2 * c, DELTA_DK), per_chunk(h, DELTA_DK, c), per_chunk(h, c, DELTA_DV),
                  per_chunk(h, c, c), per_chunk(h, 1, DELTA_DV),
                  pl.BlockSpec((c, h * DELTA_DV), lambda i: (i, 0)),
                  pl.BlockSpec((1, DELTA_DV), lambda i: (0, 0))],
        out_specs=[pl.BlockSpec((c, h * DELTA_DV), lambda i: (i, 0)),
                   pl.BlockSpec((h, DELTA_DK, DELTA_DV), lambda i: (0, 0, 0))],
        out_shape=[jax.ShapeDtypeStruct((t, h * DELTA_DV), BF16),
                   jax.ShapeDtypeStruct((h, DELTA_DK, DELTA_DV), F32)],
        scratch_shapes=[pltpu.VMEM((h, DELTA_DK, DELTA_DV), F32)],
        compiler_params=pltpu.CompilerParams(dimension_semantics=("arbitrary",),
                                             vmem_limit_bytes=V7X_VMEM_LIMIT),
        name="delta_scan",
    )(wq, kt, u, qk.astype(BF16), e_last, p_gate, norm_g.reshape(1, DELTA_DV))
    return d_out, s_fin, new_buf


def routed_experts(xt, expert_idx, gate, w_gate, w_up, w_down):
    n, d = xt.shape
    k = expert_idx.shape[1]
    n_exp = w_gate.shape[0]
    blk = EXPERT_BLOCK
    flat_e = expert_idx.reshape(-1)
    order = jnp.argsort(flat_e).astype(jnp.int32)
    counts = jnp.bincount(flat_e, length=n_exp).astype(jnp.int32)
    padded = (counts + blk - 1) // blk * blk
    pad_end = jnp.cumsum(padded)
    pad_start = pad_end - padded
    start = jnp.cumsum(counts) - counts
    n_blocks = -(-(n * k) // blk) + n_exp
    rows = n_blocks * blk
    blk_row0 = jnp.arange(n_blocks, dtype=jnp.int32) * blk
    block_expert = jnp.minimum(jnp.sum(pad_end[None, :] <= blk_row0[:, None], axis=1), n_exp - 1).astype(jnp.int32)
    n_used = (pad_end[-1:] // blk).astype(jnp.int32)
    r = jnp.arange(rows, dtype=jnp.int32)
    rank = (r.reshape(n_blocks, blk) - pad_start[block_expert][:, None])
    real = (rank < counts[block_expert][:, None]).reshape(rows)
    src = order[jnp.clip(start[block_expert][:, None] + rank, 0, n * k - 1).reshape(rows)]
    row_token = jnp.where(real, src // k, r % n)
    row_gate = jnp.where(real, gate.reshape(-1)[src], 0.0)
    xb = xt.astype(BF16)
    grp_blocks = -(-n_blocks // ROW_GROUPS)
    h_rows = None
    for b0 in range(0, n_blocks, grp_blocks):
        b1 = min(b0 + grp_blocks, n_blocks)
        x_rows = xb[row_token[b0 * blk:b1 * blk]]
        h_rows = grouped_swiglu_up(x_rows, block_expert[b0:b1], jnp.clip(n_used - b0, 0, b1 - b0),
                                   w_gate, w_up, blk, b0, n_blocks, h_prev=h_rows)
    y_rows = grouped_down(h_rows, row_gate[:, None], block_expert, n_used, w_down, blk)
    slot = jnp.argsort(order).astype(jnp.int32)
    e_of = flat_e.astype(jnp.int32)
    pos = pad_start[e_of] + slot - start[e_of]
    return jnp.sum(y_rows[pos].reshape(n, k, d).astype(F32), axis=1)


def moe(xt, w_router, router_bias, w_exp_gate, w_exp_up, w_exp_down, w_sh_gate, w_sh_up, w_sh_down):
    n = xt.shape[0]
    scores = jax.nn.sigmoid(jnp.matmul(xt, w_router, preferred_element_type=F32))
    biased = scores + router_bias
    per_group = N_EXPERTS // N_GROUPS
    group_score = lax.top_k(biased.reshape(n, N_GROUPS, per_group), 2)[0].sum(-1)
    _, top_groups = lax.top_k(group_score, TOPK_GROUPS)
    group_keep = jnp.any(top_groups[:, :, None] == jnp.arange(N_GROUPS)[None, None, :], axis=1)
    biased = jnp.where(jnp.repeat(group_keep, per_group, axis=1), biased, -jnp.inf)
    _, expert_idx = lax.top_k(biased, TOP_K)
    gate = jnp.take_along_axis(scores, expert_idx, axis=1)
    gate = gate / jnp.sum(gate, -1, keepdims=True) * ROUTED_SCALE
    routed = routed_experts(xt, expert_idx, gate, w_exp_gate, w_exp_up, w_exp_down)
    xb = xt.astype(BF16)
    shared = matmul(swiglu_up(xb, w_sh_gate, w_sh_up), w_sh_down)
    return routed + shared


def kernel(x_prompt, x_sample, cache_k, cache_v, cache_idx_k, page_table, state_delta, state_conv, w_in, w_conv, a_log, dt_bias, delta_norm_g, w_branch_attn, w_branch_delta, w_out, ln1_g, ln1_b, w_router, router_bias, w_exp_gate, w_exp_up, w_exp_down, w_sh_gate, w_sh_up, w_sh_down, ln2_g, ln2_b):
    b, s, d = x_prompt.shape
    db, t = x_sample.shape[:2]
    assert DEPTH == 1 and b == 1
    past = page_table.shape[1] * cache_k.shape[2]
    n_p, n_s = b * s, db * t
    (l_in, l_conv, l_alog, l_dtb, l_ng, l_ba, l_bd, l_out, l_g1, l_b1,
     l_r, l_rb, l_eg, l_eu, l_ed, l_sg, l_su, l_sd, l_g2, l_b2) = [a[0] for a in (
         w_in, w_conv, a_log, dt_bias, delta_norm_g, w_branch_attn, w_branch_delta, w_out,
         ln1_g, ln1_b, w_router, router_bias, w_exp_gate, w_exp_up, w_exp_down,
         w_sh_gate, w_sh_up, w_sh_down, ln2_g, ln2_b)]

    x_all = jnp.concatenate([x_prompt.reshape(n_p, d), x_sample.reshape(n_s, d)], axis=0)
    xb = x_all.astype(BF16)
    w_in_t = jnp.swapaxes(l_in, 0, 1)
    p_attn = matmul_rows(xb, w_in_t, 0, ATTN_COLS)
    p_delta = matmul_rows(xb, w_in_t, IN_SPLITS[5], DELTA_COLS)
    p_gate = matmul_rows(xb, w_in_t, IN_SPLITS[8], GATE_COLS)
    ap = attention_inputs(p_attn[:n_p].reshape(b, s, -1), jnp.arange(s))
    at = attention_inputs(p_attn[n_p:].reshape(db, t, -1), past + jnp.arange(t))

    a_p = prompt_sparse_attention(ap.q, ap.k, ap.v, ap.iq, ap.iw, ap.ik)
    a_s = sample_sparse_attention(at.q, at.k, at.v, at.iq, at.iw, at.ik,
                                  cache_k[0], cache_v[0], cache_idx_k[0], page_table)
    d_p, sd_p, sc_p = prompt_delta_branch(p_delta, p_gate, n_p, l_conv, l_alog, l_dtb, l_ng)
    pd_s = p_delta[n_p:].reshape(db, t, -1)
    nh = N_DELTA_HEADS
    d_s, sd_s, sc_s = gated_delta_branch(pd_s[..., :CONV_CH], pd_s[..., CONV_CH:CONV_CH + nh], pd_s[..., CONV_CH + nh:],
                                         p_gate[n_p:, :DELTA_V_WIDTH].reshape(db, t, -1), state_conv[0], state_delta[0],
                                         l_conv, l_alog, l_dtb, l_ng)

    a_all = jnp.concatenate([a_p.reshape(n_p, -1).astype(BF16), a_s.reshape(n_s, -1).astype(BF16)], axis=0)
    d_all = jnp.concatenate([d_p, d_s.reshape(n_s, -1).astype(BF16)], axis=0)
    merged = merge_branches(a_all, d_all, p_gate, DELTA_V_WIDTH, l_ba, l_bd)
    h = layer_norm(ALPHA * x_all + matmul(merged, l_out), l_g1, l_b1)
    y = layer_norm(ALPHA * h + moe(h, l_r, l_rb, l_eg, l_eu, l_ed, l_sg, l_su, l_sd), l_g2, l_b2)
    return (y[:n_p].reshape(b, s, d), y[n_p:].reshape(db, t, d),
            ap.k[None], ap.v[None], ap.ik[None], sd_p[None, None], sc_p[None, None],
            at.k[None], at.v[None], at.ik[None], sd_s[None], sc_s[None])
```

```python
import collections
import functools

import jax
import jax.numpy as jnp
import numpy as np
from jax import lax
from jax.experimental import pallas as pl
from jax.experimental.pallas import tpu as pltpu

D_MODEL = 4096
DEPTH = 1
N_HEADS = 16
N_KV_HEADS = 4
HEAD_DIM = 128
ROPE_THETA = 500000.0
N_IDX_HEADS = 16
IDX_DIM = 128
IDX_TOPK = 256
Q_BLOCK = 128
N_DELTA_HEADS = 16
DELTA_DK = 128
DELTA_DV = 128
CONV_WIDTH = 4
DELTA_CHUNK = 64
N_EXPERTS = 64
TOP_K = 8
N_GROUPS = 8
TOPK_GROUPS = 4
EXPERT_DIM = 1024
ROUTED_SCALE = 2.5
EXPERT_BLOCK = 512
ROW_GROUPS = 2
ALPHA = (2 * DEPTH) ** 0.25
LN_EPS = 1e-5
ATTN_WIDTH = N_HEADS * HEAD_DIM
KV_WIDTH = N_KV_HEADS * HEAD_DIM
IDXQ_WIDTH = N_IDX_HEADS * IDX_DIM
DELTA_QK_WIDTH = N_DELTA_HEADS * DELTA_DK
DELTA_V_WIDTH = N_DELTA_HEADS * DELTA_DV
CONV_CH = 2 * DELTA_QK_WIDTH + DELTA_V_WIDTH
IN_SIZES = (ATTN_WIDTH, KV_WIDTH, KV_WIDTH, IDXQ_WIDTH, IDX_DIM, N_IDX_HEADS, CONV_CH,
            N_DELTA_HEADS, N_DELTA_HEADS, DELTA_V_WIDTH, 2 * D_MODEL)
IN_SPLITS = tuple(int(s) for s in np.cumsum(IN_SIZES)[:-1])

V7X_VMEM_LIMIT = 56 * 1024 * 1024
BF16 = jnp.bfloat16
F32 = jnp.float32
SUBLANES = 8


def _mm_kernel(x_ref, w_ref, o_ref, wbf_ref):
    @pl.when(pl.program_id(1) == 0)
    def _():
        wbf_ref[...] = w_ref[...].astype(BF16)

    o_ref[...] = jnp.dot(x_ref[...], wbf_ref[...], preferred_element_type=F32).astype(o_ref.dtype)


def _row_tile(m):
    for tm in (1024, 1088, 512, 256, 128):
        if m % tm == 0:
            return tm
    raise ValueError(f"no row tile for {m} rows")


def matmul(x, w, out_dtype=F32, tn=512):
    m, k = x.shape
    n = w.shape[1]
    tm = _row_tile(m)
    tn = min(tn, n)
    return pl.pallas_call(
        _mm_kernel,
        grid=(pl.cdiv(n, tn), m // tm),
        in_specs=[pl.BlockSpec((tm, k), lambda j, i: (i, 0)),
                  pl.BlockSpec((k, tn), lambda j, i: (0, j))],
        out_specs=pl.BlockSpec((tm, tn), lambda j, i: (i, j)),
        out_shape=jax.ShapeDtypeStruct((m, n), out_dtype),
        scratch_shapes=[pltpu.VMEM((k, tn), BF16)],
        compiler_params=pltpu.CompilerParams(dimension_semantics=("arbitrary", "arbitrary"),
                                             vmem_limit_bytes=V7X_VMEM_LIMIT),
        name="dense_matmul",
    )(x, w)


def _swiglu_kernel(x_ref, wg_ref, wu_ref, o_ref, wg_bf, wu_bf):
    @pl.when(pl.program_id(1) == 0)
    def _():
        wg_bf[...] = wg_ref[...].astype(BF16)
        wu_bf[...] = wu_ref[...].astype(BF16)

    x = x_ref[...]
    g = jnp.dot(x, wg_bf[...], preferred_element_type=F32)
    u = jnp.dot(x, wu_bf[...], preferred_element_type=F32)
    o_ref[...] = (g * jax.nn.sigmoid(g) * u).astype(o_ref.dtype)


def swiglu_up(x, w_gate, w_up, tn=256):
    m, k = x.shape
    f = w_gate.shape[1]
    tm = _row_tile(m)
    tn = min(tn, f)
    assert f % tn == 0 and w_up.shape == w_gate.shape
    return pl.pallas_call(
        _swiglu_kernel,
        grid=(f // tn, m // tm),
        in_specs=[pl.BlockSpec((tm, k), lambda j, i: (i, 0)),
                  pl.BlockSpec((k, tn), lambda j, i: (0, j)),
                  pl.BlockSpec((k, tn), lambda j, i: (0, j))],
        out_specs=pl.BlockSpec((tm, tn), lambda j, i: (i, j)),
        out_shape=jax.ShapeDtypeStruct((m, f), BF16),
        scratch_shapes=[pltpu.VMEM((k, tn), BF16), pltpu.VMEM((k, tn), BF16)],
        compiler_params=pltpu.CompilerParams(dimension_semantics=("arbitrary", "arbitrary"),
                                             vmem_limit_bytes=V7X_VMEM_LIMIT),
        name="swiglu_up",
    )(x, w_gate, w_up)


def _merge_kernel(a_ref, d_ref, ga_ref, gd_ref, wa_ref, wd_ref, o_ref, wa_bf, wd_bf):
    @pl.when(pl.program_id(1) == 0)
    def _():
        wa_bf[...] = wa_ref[...].astype(BF16)
        wd_bf[...] = wd_ref[...].astype(BF16)

    pa = jnp.dot(a_ref[...], wa_bf[...], preferred_element_type=F32)
    pd = jnp.dot(d_ref[...], wd_bf[...], preferred_element_type=F32)
    o_ref[...] = (jax.nn.sigmoid(ga_ref[...]) * pa + jax.nn.sigmoid(gd_ref[...]) * pd).astype(o_ref.dtype)


def merge_branches(a, d, p_gate, gate_col0, w_a, w_d, tn=512):
    m, k = a.shape
    n = w_a.shape[1]
    tm = _row_tile(m)
    assert gate_col0 % tn == 0 and n % tn == 0 and w_d.shape == w_a.shape and d.shape == a.shape
    ga0, gd0 = gate_col0 // tn, (gate_col0 + n) // tn
    return pl.pallas_call(
        _merge_kernel,
        grid=(n // tn, m // tm),
        in_specs=[pl.BlockSpec((tm, k), lambda j, i: (i, 0)),
                  pl.BlockSpec((tm, k), lambda j, i: (i, 0)),
                  pl.BlockSpec((tm, tn), lambda j, i: (i, ga0 + j)),
                  pl.BlockSpec((tm, tn), lambda j, i: (i, gd0 + j)),
                  pl.BlockSpec((k, tn), lambda j, i: (0, j)),
                  pl.BlockSpec((k, tn), lambda j, i: (0, j))],
        out_specs=pl.BlockSpec((tm, tn), lambda j, i: (i, j)),
        out_shape=jax.ShapeDtypeStruct((m, n), BF16),
        scratch_shapes=[pltpu.VMEM((k, tn), BF16), pltpu.VMEM((k, tn), BF16)],
        compiler_params=pltpu.CompilerParams(dimension_semantics=("arbitrary", "arbitrary"),
                                             vmem_limit_bytes=V7X_VMEM_LIMIT),
        name="merge_branches",
    )(a, d, p_gate, p_gate, w_a, w_d)


def _mm_rows_kernel(x_ref, wt_ref, o_ref, wbf_ref):
    @pl.when(pl.program_id(1) == 0)
    def _():
        wbf_ref[...] = wt_ref[...].astype(BF16)

    o_ref[...] = lax.dot_general(x_ref[...], wbf_ref[...], (((1,), (1,)), ((), ())),
                                 preferred_element_type=F32).astype(o_ref.dtype)


def matmul_rows(x, wt, row0, nrows, out_dtype=F32, tn=512):
    m, k = x.shape
    tm = _row_tile(m)
    tn = min(tn, nrows)
    assert row0 % SUBLANES == 0 and wt.shape[1] == k
    w_spec = pl.BlockSpec((pl.Element(tn, (0, tn)), pl.Element(k)),
                          lambda j, i: (pl.multiple_of(row0 + j * tn, SUBLANES), 0))
    return pl.pallas_call(
        _mm_rows_kernel,
        grid=(pl.cdiv(nrows, tn), m // tm),
        in_specs=[pl.BlockSpec((tm, k), lambda j, i: (i, 0)), w_spec],
        out_specs=pl.BlockSpec((tm, tn), lambda j, i: (i, j)),
        out_shape=jax.ShapeDtypeStruct((m, nrows), out_dtype),
        scratch_shapes=[pltpu.VMEM((tn, k), BF16)],
        compiler_params=pltpu.CompilerParams(dimension_semantics=("arbitrary", "arbitrary"),
                                             vmem_limit_bytes=V7X_VMEM_LIMIT),
        name="dense_matmul_rows",
    )(x, wt)


def _expert_changed(be_ref, i):
    prev = be_ref[jnp.maximum(i - 1, 0)]
    return jnp.logical_or(i == 0, be_ref[i] != prev)


def _moe_up_kernel(be_ref, nu_ref, x_ref, wg_ref, wu_ref, *rest):
    h_ref, wg_bf, wu_bf = rest[-3:]
    i = pl.program_id(1)

    @pl.when(i < nu_ref[0])
    def _():
        @pl.when(_expert_changed(be_ref, i))
        def _():
            wg_bf[...] = wg_ref[...].astype(BF16)
            wu_bf[...] = wu_ref[...].astype(BF16)

        x = x_ref[...]
        g = jnp.dot(x, wg_bf[...], preferred_element_type=F32)
        u = jnp.dot(x, wu_bf[...], preferred_element_type=F32)
        h_ref[...] = (g * jax.nn.sigmoid(g) * u).astype(h_ref.dtype)


def _moe_down_kernel(be_ref, nu_ref, h_ref, gate_ref, wd_ref, y_ref, wd_bf):
    i = pl.program_id(1)

    @pl.when(i < nu_ref[0])
    def _():
        @pl.when(_expert_changed(be_ref, i))
        def _():
            wd_bf[...] = wd_ref[...].astype(BF16)

        y = jnp.dot(h_ref[...], wd_bf[...], preferred_element_type=F32)
        y_ref[...] = (y * gate_ref[...]).astype(y_ref.dtype)


def grouped_swiglu_up(x_rows, block_expert, n_used, w_gate, w_up, blk, block0, total_blocks, h_prev=None, tf=512):
    rows, d = x_rows.shape
    f = w_gate.shape[2]
    tf = min(tf, f)
    n_blocks = rows // blk
    in_specs = [pl.BlockSpec((blk, d), lambda j, i, be, nu: (i, 0)),
                pl.BlockSpec((None, d, tf), lambda j, i, be, nu: (be[i], 0, j)),
                pl.BlockSpec((None, d, tf), lambda j, i, be, nu: (be[i], 0, j))]
    operands = [block_expert, n_used, x_rows, w_gate, w_up]
    aliases = {}
    if h_prev is not None:
        in_specs.append(pl.BlockSpec(memory_space=pl.ANY))
        operands.append(h_prev)
        aliases = {len(operands) - 1: 0}
    grid_spec = pltpu.PrefetchScalarGridSpec(
        num_scalar_prefetch=2,
        grid=(f // tf, n_blocks),
        in_specs=in_specs,
        out_specs=pl.BlockSpec((blk, tf), lambda j, i, be, nu: (block0 + i, j)),
        scratch_shapes=[pltpu.VMEM((d, tf), BF16), pltpu.VMEM((d, tf), BF16)],
    )
    return pl.pallas_call(
        _moe_up_kernel, grid_spec=grid_spec,
        out_shape=jax.ShapeDtypeStruct((total_blocks * blk, f), BF16),
        input_output_aliases=aliases,
        compiler_params=pltpu.CompilerParams(dimension_semantics=("arbitrary", "arbitrary"),
                                             vmem_limit_bytes=V7X_VMEM_LIMIT),
        name="moe_up",
    )(*operands)


def grouped_down(h_rows, row_gate, block_expert, n_used, w_down, blk, tn=2048):
    rows, f = h_rows.shape
    d = w_down.shape[2]
    tn = min(tn, d)
    n_blocks = rows // blk
    grid_spec = pltpu.PrefetchScalarGridSpec(
        num_scalar_prefetch=2,
        grid=(d // tn, n_blocks),
        in_specs=[pl.BlockSpec((blk, f), lambda j, i, be, nu: (i, 0)),
                  pl.BlockSpec((blk, 1), lambda j, i, be, nu: (i, 0)),
                  pl.BlockSpec((None, f, tn), lambda j, i, be, nu: (be[i], 0, j))],
        out_specs=pl.BlockSpec((blk, tn), lambda j, i, be, nu: (i, j)),
        scratch_shapes=[pltpu.VMEM((f, tn), BF16)],
    )
    return pl.pallas_call(
        _moe_down_kernel, grid_spec=grid_spec,
        out_shape=jax.ShapeDtypeStruct((rows, d), BF16),
        compiler_params=pltpu.CompilerParams(dimension_semantics=("arbitrary", "arbitrary"),
                                             vmem_limit_bytes=V7X_VMEM_LIMIT),
        name="moe_down",
    )(block_expert, n_used, h_rows, row_gate, w_down)


def layer_norm(x, g, b):
    xc = x - jnp.mean(x, -1, keepdims=True)
    var = jnp.mean(xc * xc, -1, keepdims=True)
    return xc * lax.rsqrt(var + LN_EPS) * g + b


def l2norm(x):
    return x * lax.rsqrt(jnp.sum(x * x, -1, keepdims=True) + 1e-6)


def partial_rope(x, pos):
    rot = x.shape[-1] // 4
    half = rot // 2
    inv_freq = ROPE_THETA ** (-jnp.arange(half, dtype=F32) / half)
    ang = pos.astype(F32)[:, None] * inv_freq[None, :]
    cos = jnp.cos(ang)[:, None, :]
    sin = jnp.sin(ang)[:, None, :]
    x1, x2, rest = x[..., :half], x[..., half:rot], x[..., rot:]
    return jnp.concatenate([x1 * cos - x2 * sin, x2 * cos + x1 * sin, rest], axis=-1)


ATTN_COLS = IN_SPLITS[5]
DELTA_COLS = IN_SPLITS[8] - IN_SPLITS[5]
GATE_COLS = int(sum(IN_SIZES)) - IN_SPLITS[8]
AttnInputs = collections.namedtuple("AttnInputs", "q k v iq ik iw")


def attention_inputs(p, pos):
    b, t, _ = p.shape
    q, k, v, iq, ik, iw = jnp.split(p, IN_SPLITS[:5], axis=-1)
    q = partial_rope(q.reshape(b, t, N_HEADS, HEAD_DIM), pos)
    k = partial_rope(k.reshape(b, t, N_KV_HEADS, HEAD_DIM), pos)
    v = v.reshape(b, t, N_KV_HEADS, HEAD_DIM)
    iq = partial_rope(iq.reshape(b, t, N_IDX_HEADS, IDX_DIM), pos)
    ik = partial_rope(ik[:, :, None, :], pos)[:, :, 0, :]
    iw = iw * (N_IDX_HEADS ** -0.5 * IDX_DIM ** -0.5)
    return AttnInputs(q, k, v, iq, ik, iw)


KEY_CHUNK = 512
IDX_HEAD_GROUP = 4
MASKED = -1e30
SOFTMAX_LOG2_SCALE = HEAD_DIM ** -0.5 * float(np.log2(np.e))
INT32_MIN = -2 ** 31
INT32_MAX = 2 ** 31 - 1
NEG_INF_KEY = int(np.int32(np.uint32(0xFF800000) ^ np.uint32(0x7FFFFFFF)))


def _sort_key(x):
    bits = lax.bitcast_convert_type(x, jnp.int32)
    return bits ^ (jnp.right_shift(bits, 31) & jnp.int32(INT32_MAX))


def _lane_tile(x, n):
    return x if n == 1 else jnp.concatenate([x] * n, axis=1)


def _count_rows(keys_ref, nkc, preds):
    _, rows, kc = keys_ref.shape

    def body(c, parts):
        keys = keys_ref[c]
        out = []
        for pred, part in zip(preds, parts):
            hit = pred(keys, c)
            for j in range(kc // 128):
                part = part + hit[:, j * 128:(j + 1) * 128]
            out.append(part)
        return tuple(out)

    parts = lax.fori_loop(0, nkc, body, tuple(jnp.zeros((rows, 128), F32) for _ in preds))
    return [jnp.broadcast_to(jnp.sum(part, axis=1, keepdims=True), (rows, 128)) for part in parts]


def _topk_selection(keys_ref, nkc, n_sel, radix_bits):
    _, rows, kc = keys_ref.shape
    reps = kc // 128
    col = lax.broadcasted_iota(jnp.int32, (rows, kc), 1)

    def thr_digit(it, carry):
        thr, cnt_thr = carry
        shift = 32 - radix_bits * (it + 1)
        cands = [thr + jnp.left_shift(jnp.int32(j), shift) for j in range(1, 2 ** radix_bits)]
        cands_w = [_lane_tile(cand, reps) for cand in cands]
        cnts = _count_rows(keys_ref, nkc, [lambda keys, c, cw=cw: jnp.where(keys >= cw, 1.0, 0.0) for cw in cands_w])
        for cand, cnt in zip(cands, cnts):
            ok = cnt >= n_sel
            thr = jnp.where(ok, cand, thr)
            cnt_thr = jnp.where(ok, cnt, cnt_thr)
        return thr, cnt_thr

    thr0 = jnp.full((rows, 128), INT32_MIN, jnp.int32)
    cnt0 = jnp.broadcast_to(jnp.asarray(nkc * kc).astype(F32), (rows, 128))
    thr, cnt_thr = lax.fori_loop(0, 32 // radix_bits, thr_digit, (thr0, cnt0))

    short = thr == NEG_INF_KEY
    thr_w = _lane_tile(thr, reps)
    cnt_gt, = _count_rows(keys_ref, nkc, [lambda keys, c: jnp.where(keys > thr_w, 1.0, 0.0)])
    need = n_sel - cnt_gt
    tied = jnp.logical_and(cnt_thr > n_sel, jnp.logical_not(short))

    def tie_limit():
        def idx_bit(it, x):
            cand = x + jnp.left_shift(jnp.int32(1), 30 - it)
            cand_w = _lane_tile(cand, reps)
            cnt, = _count_rows(keys_ref, nkc, [lambda keys, c: jnp.where(
                keys == thr_w, jnp.where(c * kc + col < cand_w, 1.0, 0.0), 0.0)])
            return jnp.where(cnt < need, cand, x)
        return lax.fori_loop(0, 31, idx_bit, jnp.zeros((rows, 128), jnp.int32))

    any_tied = jnp.max(jnp.where(tied, 1.0, 0.0)) > 0.0
    lim = lax.cond(any_tied, tie_limit, lambda: jnp.full((rows, 128), INT32_MAX, jnp.int32))
    lim = jnp.where(short, -1, jnp.where(tied, lim, INT32_MAX))
    return thr, lim


def _selected(keys, index, thr_w, lim_w, yes, no):
    keep_tie = jnp.where(index <= lim_w, yes, no)
    return jnp.where(keys > thr_w, yes, jnp.where(keys == thr_w, keep_tie, no))


def _dsa_prompt_kernel(iq_ref, iw_ref, ikt_ref, q_ref, kt_ref, v_ref, o_ref,
                       keys_ref, m_ref, l_ref, acc_ref, *, n_sel):
    qb, kc = Q_BLOCK, KEY_CHUNK
    reps = kc // 128
    grp = N_HEADS // N_KV_HEADS
    i = pl.program_id(0)
    nkc = (i * qb + qb + kc - 1) // kc
    qpos = i * qb + lax.broadcasted_iota(jnp.int32, (qb, kc), 0)
    col = lax.broadcasted_iota(jnp.int32, (qb, kc), 1)

    def score_chunk(c, carry):
        ikc = ikt_ref[c]
        acc = jnp.zeros((qb, kc), F32)
        for h0 in range(0, N_IDX_HEADS, IDX_HEAD_GROUP):
            iq_rows = iq_ref[h0:h0 + IDX_HEAD_GROUP].reshape(IDX_HEAD_GROUP * qb, IDX_DIM)
            s = jnp.dot(iq_rows, ikc, preferred_element_type=F32)
            for j in range(IDX_HEAD_GROUP):
                acc = acc + jnp.maximum(s[j * qb:(j + 1) * qb], 0.0) * iw_ref[:, h0 + j:h0 + j + 1]
        acc = jnp.where(c * kc + col <= qpos, acc, -jnp.inf)
        keys_ref[c] = _sort_key(acc)
        return carry

    lax.fori_loop(0, nkc, score_chunk, 0)

    thr, lim = _topk_selection(keys_ref, nkc, n_sel, radix_bits=1)
    thr_w, lim_w = _lane_tile(thr, reps), _lane_tile(lim, reps)

    m_ref[...] = jnp.full(m_ref.shape, MASKED, F32)
    l_ref[...] = jnp.zeros(l_ref.shape, F32)
    acc_ref[...] = jnp.zeros(acc_ref.shape, F32)
    rows = grp * qb

    def attend_chunk(c, carry):
        bias = _selected(keys_ref[c], c * kc + col, thr_w, lim_w, 0.0, MASKED)
        for n in range(N_KV_HEADS):
            r = pl.ds(n * rows, rows)
            qn = q_ref[n * grp:(n + 1) * grp].reshape(rows, HEAD_DIM)
            s = jnp.dot(qn, kt_ref[n, c], preferred_element_type=F32) * SOFTMAX_LOG2_SCALE
            s = (s.reshape(grp, qb, kc) + bias[None]).reshape(rows, kc)
            m_prev = m_ref[r, :]
            m_new = jnp.maximum(m_prev, jnp.max(s, axis=1, keepdims=True))
            alpha = jnp.exp2(m_prev - m_new)
            p = jnp.exp2(s - _lane_tile(m_new, reps))
            l_ref[r, :] = alpha * l_ref[r, :] + jnp.sum(p, axis=1, keepdims=True)
            acc_ref[r, :] = alpha * acc_ref[r, :] + jnp.dot(p.astype(BF16), v_ref[n, c],
                                                           preferred_element_type=F32)
            m_ref[r, :] = m_new
        return carry

    lax.fori_loop(0, nkc, attend_chunk, 0)
    for h in range(N_HEADS):
        r = pl.ds(h * qb, qb)
        o_ref[:, h * HEAD_DIM:(h + 1) * HEAD_DIM] = (acc_ref[r, :] / l_ref[r, :]).astype(o_ref.dtype)


def prompt_sparse_attention(q, k, v, iq, iw, ik):
    b, s = q.shape[:2]
    assert b == 1 and s % KEY_CHUNK == 0 and KEY_CHUNK % Q_BLOCK == 0 and KEY_CHUNK >= IDX_TOPK
    n_sel = min(IDX_TOPK, s // 4)
    nb, nc, kc = s // Q_BLOCK, s // KEY_CHUNK, KEY_CHUNK

    def head_major(a):
        return a.astype(BF16).reshape(nb, Q_BLOCK, a.shape[2], a.shape[3]).transpose(0, 2, 1, 3)

    ikt = ik.astype(BF16).reshape(nc, kc, IDX_DIM).transpose(0, 2, 1)
    kt = k.astype(BF16).reshape(nc, kc, N_KV_HEADS, HEAD_DIM).transpose(2, 0, 3, 1)
    vc = v.astype(BF16).reshape(nc, kc, N_KV_HEADS, HEAD_DIM).transpose(2, 0, 1, 3)
    resident = dict(pipeline_mode=pl.Buffered(1))
    out = pl.pallas_call(
        functools.partial(_dsa_prompt_kernel, n_sel=n_sel),
        grid=(nb,),
        in_specs=[pl.BlockSpec((None, N_IDX_HEADS, Q_BLOCK, IDX_DIM), lambda i: (i, 0, 0, 0)),
                  pl.BlockSpec((Q_BLOCK, N_IDX_HEADS), lambda i: (i, 0)),
                  pl.BlockSpec((nc, IDX_DIM, kc), lambda i: (0, 0, 0), **resident),
                  pl.BlockSpec((None, N_HEADS, Q_BLOCK, HEAD_DIM), lambda i: (i, 0, 0, 0)),
                  pl.BlockSpec((N_KV_HEADS, nc, HEAD_DIM, kc), lambda i: (0, 0, 0, 0), **resident),
                  pl.BlockSpec((N_KV_HEADS, nc, kc, HEAD_DIM), lambda i: (0, 0, 0, 0), **resident)],
        out_specs=pl.BlockSpec((Q_BLOCK, N_HEADS * HEAD_DIM), lambda i: (i, 0)),
        out_shape=jax.ShapeDtypeStruct((s, N_HEADS * HEAD_DIM), BF16),
        scratch_shapes=[pltpu.VMEM((nc, Q_BLOCK, kc), jnp.int32),
                        pltpu.VMEM((N_HEADS * Q_BLOCK, 128), F32),
                        pltpu.VMEM((N_HEADS * Q_BLOCK, 128), F32),
                        pltpu.VMEM((N_HEADS * Q_BLOCK, HEAD_DIM), F32)],
        compiler_params=pltpu.CompilerParams(dimension_semantics=("arbitrary",),
                                             vmem_limit_bytes=V7X_VMEM_LIMIT),
        name="dsa_prompt",
    )(head_major(iq), iw.reshape(s, N_IDX_HEADS), ikt, head_major(q), kt, vc)
    return out.reshape(b, s, N_HEADS * HEAD_DIM)


T_PAD = 8


def _dsa_sample_kernel(pt_ref, iq_ref, iw_ref, q_ref, iknew_ref, knew_ref, vnew_ref, *rest, n_sel, n_pages):
    del pt_ref
    ik_pages, k_pages, v_pages = rest[:n_pages], rest[n_pages:2 * n_pages], rest[2 * n_pages:3 * n_pages]
    o_ref, keys_ref, s_ref = rest[3 * n_pages:]
    page = ik_pages[0].shape[0]
    kv_rows = k_pages[0].shape[0]
    rows = N_HEADS * T_PAD
    nkc = n_pages + 1
    nt = (((1,), (1,)), ((), ()))
    row_t = lax.broadcasted_iota(jnp.int32, (T_PAD, page), 0)
    col = lax.broadcasted_iota(jnp.int32, (T_PAD, page), 1)

    for c in range(nkc):
        ikc = ik_pages[c][...].astype(BF16) if c < n_pages else iknew_ref[...]
        s = lax.dot_general(iq_ref[...], ikc, nt, preferred_element_type=F32)
        x = jnp.maximum(s, 0.0) * iw_ref[...]
        sc = jnp.sum(x.reshape(T_PAD, N_IDX_HEADS, page), axis=1)
        if c == n_pages:
            sc = jnp.where(col <= row_t, sc, -jnp.inf)
        keys_ref[c] = _sort_key(sc)

    thr, lim = _topk_selection(keys_ref, nkc, n_sel, radix_bits=4)

    scale = SOFTMAX_LOG2_SCALE
    spread = (lax.broadcasted_iota(jnp.int32, (page, kv_rows), 1) // N_KV_HEADS
              == lax.broadcasted_iota(jnp.int32, (page, kv_rows), 0)).astype(BF16)
    own_head = (lax.broadcasted_iota(jnp.int32, (rows, kv_rows), 0) // (rows // N_KV_HEADS)
                == lax.broadcasted_iota(jnp.int32, (rows, kv_rows), 1) % N_KV_HEADS).astype(F32)
    for c in range(nkc):
        kc_ = k_pages[c][...].astype(BF16) if c < n_pages else knew_ref[...]
        s = lax.dot_general(q_ref[...], kc_, nt, preferred_element_type=F32) * scale
        sel = _selected(keys_ref[c], c * page + col, thr, lim, 1.0, 0.0).astype(BF16)
        sel = jnp.dot(sel, spread, preferred_element_type=F32)
        keep = jnp.broadcast_to(sel[None], (N_HEADS, T_PAD, kv_rows)).reshape(rows, kv_rows) * own_head
        s_ref[:, c * kv_rows:(c + 1) * kv_rows] = jnp.where(keep > 0.5, s, MASKED)
    s = s_ref[...]
    p = jnp.exp2(s - jnp.max(s, axis=1, keepdims=True))
    inv_l = 1.0 / jnp.sum(p, axis=1, keepdims=True)
    p = p.astype(BF16)
    o = jnp.zeros((rows, HEAD_DIM), F32)
    for c in range(nkc):
        vc_ = v_pages[c][...].astype(BF16) if c < n_pages else vnew_ref[...]
        o = o + jnp.dot(p[:, c * kv_rows:(c + 1) * kv_rows], vc_, preferred_element_type=F32)
    o_ref[...] = o * inv_l


def sample_sparse_attention(q, k_new, v_new, iq, iw, ik_new, cache_k, cache_v, cache_idx_k, page_table):
    db, t = q.shape[:2]
    n_pool, page = cache_k.shape[:2]
    n_pages = page_table.shape[1]
    n_keys = n_pages * page + t
    n_sel = min(IDX_TOPK, n_keys // 4)
    grp = N_HEADS // N_KV_HEADS
    assert t <= T_PAD <= page and page == 128 and HEAD_DIM == 128 and IDX_DIM == 128

    def pad_t(a, to):
        return jnp.pad(a, [(0, 0), (0, to - a.shape[1])] + [(0, 0)] * (a.ndim - 2))

    kv_rows = page * N_KV_HEADS
    rows = N_HEADS * T_PAD
    iq_rows = pad_t(iq, T_PAD).astype(BF16).reshape(db, T_PAD * N_IDX_HEADS, IDX_DIM)
    iw_rows = pad_t(iw, T_PAD).reshape(db, T_PAD * N_IDX_HEADS, 1)
    q_rows = pad_t(q, T_PAD).astype(BF16).transpose(0, 2, 1, 3).reshape(db, rows, HEAD_DIM)
    iknew = pad_t(ik_new, page).astype(BF16)
    knew = pad_t(k_new, page).astype(BF16).reshape(db, kv_rows, HEAD_DIM)
    vnew = pad_t(v_new, page).astype(BF16).reshape(db, kv_rows, HEAD_DIM)
    ck = cache_k.reshape(n_pool, kv_rows, HEAD_DIM)
    cv = cache_v.reshape(n_pool, kv_rows, HEAD_DIM)

    def per_batch(*blk):
        return pl.BlockSpec((None,) + blk, lambda b, pt: (b,) + (0,) * len(blk))

    def paged(nrows):
        return [pl.BlockSpec((None, nrows, HEAD_DIM), lambda b, pt, j=j: (pt[b, j], 0, 0)) for j in range(n_pages)]

    grid_spec = pltpu.PrefetchScalarGridSpec(
        num_scalar_prefetch=1,
        grid=(db,),
        in_specs=[per_batch(T_PAD * N_IDX_HEADS, IDX_DIM), per_batch(T_PAD * N_IDX_HEADS, 1),
                  per_batch(rows, HEAD_DIM), per_batch(page, IDX_DIM), per_batch(kv_rows, HEAD_DIM),
                  per_batch(kv_rows, HEAD_DIM)]
        + paged(page) + paged(kv_rows) + paged(kv_rows),
        out_specs=per_batch(rows, HEAD_DIM),
        scratch_shapes=[pltpu.VMEM((n_pages + 1, T_PAD, page), jnp.int32),
                        pltpu.VMEM((rows, (n_pages + 1) * kv_rows), F32)],
    )
    o = pl.pallas_call(
        functools.partial(_dsa_sample_kernel, n_sel=n_sel, n_pages=n_pages),
        grid_spec=grid_spec,
        out_shape=jax.ShapeDtypeStruct((db, rows, HEAD_DIM), F32),
        compiler_params=pltpu.CompilerParams(dimension_semantics=("arbitrary",),
                                             vmem_limit_bytes=V7X_VMEM_LIMIT),
        name="dsa_sample",
    )(page_table, iq_rows, iw_rows, q_rows, iknew, knew, vnew,
      *([cache_idx_k] * n_pages), *([ck] * n_pages), *([cv] * n_pages))
    o = o.reshape(db, N_KV_HEADS, grp, T_PAD, HEAD_DIM)[:, :, :, :t]
    return o.transpose(0, 3, 1, 2, 4).reshape(db, t, N_HEADS * HEAD_DIM)


def causal_conv(xpad, w_conv):
    c = xpad.shape[-1]
    return lax.conv_general_dilated(xpad, w_conv[:, None, :], window_strides=(1,),
                                    padding='VALID', dimension_numbers=('NWC', 'WIO', 'NWC'),
                                    feature_group_count=c)


def chunk_gated_delta(q, k, v, beta, g, s0):
    b, t, h, dk = q.shape
    dv = v.shape[-1]
    c = min(DELTA_CHUNK, t)
    n = -(-t // c)
    pad = n * c - t

    def chunks(a):
        a = jnp.pad(a, [(0, 0), (0, pad)] + [(0, 0)] * (a.ndim - 2))
        a = a.reshape((b, n, c) + a.shape[2:])
        return jnp.moveaxis(a, (1, 3), (0, 2))

    qc, kc, vc, bc, gc = [chunks(a) for a in (q, k, v, beta, g)]
    gc = jnp.cumsum(gc, axis=-1)
    tri = jnp.tril(jnp.ones((c, c), bool))
    strict = jnp.tril(jnp.ones((c, c), bool), -1)
    decay = jnp.exp(jnp.where(tri, gc[..., :, None] - gc[..., None, :], -jnp.inf))
    kb = kc * bc[..., None]
    a_low = jnp.where(strict, jnp.einsum('nbhid,nbhjd->nbhij', kb, kc) * decay, 0.0)
    rhs = jnp.concatenate([vc * bc[..., None], kb * jnp.exp(gc)[..., None]], axis=-1)
    sol = lax.linalg.triangular_solve(a_low + jnp.eye(c, dtype=F32), rhs, left_side=True,
                                      lower=True, unit_diagonal=True)
    u, w = sol[..., :dv], sol[..., dv:]
    qk = jnp.where(tri, jnp.einsum('nbhid,nbhjd->nbhij', qc, kc) * decay, 0.0)

    def step(state, xs):
        q_i, k_i, u_i, w_i, g_i, qk_i = xs
        v_corr = u_i - jnp.einsum('bhck,bhkv->bhcv', w_i, state)
        o_i = (jnp.einsum('bhck,bhkv->bhcv', q_i * jnp.exp(g_i)[..., None], state)
               + jnp.einsum('bhij,bhjv->bhiv', qk_i, v_corr))
        g_last = g_i[..., -1:]
        state = (state * jnp.exp(g_last)[..., None]
                 + jnp.einsum('bhck,bhcv->bhkv', k_i * jnp.exp(g_last - g_i)[..., None], v_corr))
        return state, o_i

    s_final, o = lax.scan(step, s0, (qc, kc, u, w, gc, qk))
    o = jnp.moveaxis(o, (0, 2), (1, 3)).reshape(b, n * c, h, dv)[:, :t]
    return o, s_final


def gated_delta_branch(qkv, b_d, a_d, z, conv_buf, s0, w_conv, a_log, dt_bias, norm_g):
    b, t, _ = qkv.shape
    xpad = jnp.concatenate([conv_buf, qkv], axis=1)
    new_buf = xpad[:, -(CONV_WIDTH - 1):]
    hc = jax.nn.silu(causal_conv(xpad, w_conv))
    q, k, v = jnp.split(hc, (DELTA_QK_WIDTH, 2 * DELTA_QK_WIDTH), axis=-1)
    q = l2norm(q.reshape(b, t, N_DELTA_HEADS, DELTA_DK)) * DELTA_DK ** -0.5
    k = l2norm(k.reshape(b, t, N_DELTA_HEADS, DELTA_DK))
    v = v.reshape(b, t, N_DELTA_HEADS, DELTA_DV)
    beta = jax.nn.sigmoid(b_d)
    g = -jnp.exp(a_log) * jax.nn.softplus(a_d + dt_bias)
    o, s_new = chunk_gated_delta(q, k, v, beta, g, s0)
    o = o * lax.rsqrt(jnp.mean(o * o, -1, keepdims=True) + 1e-6) * norm_g
    o = o * jax.nn.silu(z.reshape(b, t, N_DELTA_HEADS, DELTA_DV))
    return o.reshape(b, t, DELTA_V_WIDTH), s_new, new_buf


DELTA_TOKEN_BLOCK = 256


def _delta_pre_kernel(x_ref, halo_ref, w_ref, q_ref, k_ref, v_ref, xe_ref):
    i = pl.program_id(0)
    tb = x_ref.shape[0]
    xe_ref[0:SUBLANES, :] = jnp.where(i == 0, 0.0, halo_ref[...])
    xe_ref[SUBLANES:SUBLANES + tb, :] = x_ref[...]
    outs = (q_ref, k_ref, v_ref)
    for col in range(3 * N_DELTA_HEADS):
        cs = slice(col * DELTA_DK, (col + 1) * DELTA_DK)
        y = sum(xe_ref[SUBLANES - (CONV_WIDTH - 1) + tap:SUBLANES - (CONV_WIDTH - 1) + tap + tb, cs] * w_ref[tap:tap + 1, cs]
                for tap in range(CONV_WIDTH))
        y = y * jax.nn.sigmoid(y)
        if col < 2 * N_DELTA_HEADS:
            y = y * lax.rsqrt(jnp.sum(y * y, axis=1, keepdims=True) + 1e-6)
        if col < N_DELTA_HEADS:
            y = y * DELTA_DK ** -0.5
        for cc in range(tb // DELTA_CHUNK):
            outs[col // N_DELTA_HEADS][cc, col % N_DELTA_HEADS] = y[cc * DELTA_CHUNK:(cc + 1) * DELTA_CHUNK]


def delta_pre(qkv, t, w_conv):
    c = w_conv.shape[1]
    tb = DELTA_TOKEN_BLOCK
    assert t % tb == 0 and tb % DELTA_CHUNK == 0 and DELTA_DK == DELTA_DV
    per_blk = tb // DELTA_CHUNK
    out_spec = pl.BlockSpec((per_blk, N_DELTA_HEADS, DELTA_CHUNK, DELTA_DK), lambda i: (i, 0, 0, 0))
    out_shape = jax.ShapeDtypeStruct((t // DELTA_CHUNK, N_DELTA_HEADS, DELTA_CHUNK, DELTA_DK), F32)
    return pl.pallas_call(
        _delta_pre_kernel,
        grid=(t // tb,),
        in_specs=[pl.BlockSpec((tb, c), lambda i: (i, 0)),
                  pl.BlockSpec((SUBLANES, c), lambda i: (jnp.maximum(i * (tb // SUBLANES) - 1, 0), 0)),
                  pl.BlockSpec((CONV_WIDTH, c), lambda i: (0, 0))],
        out_specs=[out_spec] * 3,
        out_shape=[out_shape] * 3,
        scratch_shapes=[pltpu.VMEM((tb + SUBLANES, c), F32)],
        compiler_params=pltpu.CompilerParams(dimension_semantics=("arbitrary",),
                                             vmem_limit_bytes=V7X_VMEM_LIMIT),
        name="delta_pre",
    )(qkv, qkv, w_conv)


def _delta_scan_kernel(wq_ref, kt_ref, u_ref, qk_ref, el_ref, z_ref, ng_ref, o_ref, sfin_ref, s_ref):
    n = pl.program_id(0)
    c = DELTA_CHUNK

    @pl.when(n == 0)
    def _():
        s_ref[...] = jnp.zeros(s_ref.shape, F32)

    for h in range(N_DELTA_HEADS):
        s = s_ref[h]
        ws = jnp.dot(wq_ref[h], s.astype(BF16), preferred_element_type=F32)
        v_corr = (u_ref[h] - ws[:c]).astype(BF16)
        o = ws[c:] + jnp.dot(qk_ref[h], v_corr, preferred_element_type=F32)
        s_ref[h] = s * el_ref[h] + jnp.dot(kt_ref[h], v_corr, preferred_element_type=F32)
        o = o * lax.rsqrt(jnp.mean(o * o, axis=1, keepdims=True) + 1e-6) * ng_ref[...]
        zz = z_ref[:, h * DELTA_DV:(h + 1) * DELTA_DV]
        o_ref[:, h * DELTA_DV:(h + 1) * DELTA_DV] = (o * (zz * jax.nn.sigmoid(zz))).astype(o_ref.dtype)

    @pl.when(n == pl.num_programs(0) - 1)
    def _():
        sfin_ref[...] = s_ref[...]


def prompt_delta_branch(p_delta, p_gate, t, w_conv, a_log, dt_bias, norm_g):
    assert t >= CONV_WIDTH - 1
    c, h = DELTA_CHUNK, N_DELTA_HEADS
    n = t // c
    new_buf = p_delta[t - (CONV_WIDTH - 1):t, :CONV_CH]
    qc, kc, vc = delta_pre(p_delta, t, w_conv)
    beta = jax.nn.sigmoid(p_delta[:t, CONV_CH:CONV_CH + h])
    g = -jnp.exp(a_log) * jax.nn.softplus(p_delta[:t, CONV_CH + h:CONV_CH + 2 * h] + dt_bias)
    bc = beta.reshape(n, c, h).transpose(0, 2, 1)
    gc = jnp.cumsum(g.reshape(n, c, h).transpose(0, 2, 1), axis=-1)
    tri = jnp.tril(jnp.ones((c, c), bool))
    strict = jnp.tril(jnp.ones((c, c), bool), -1)
    decay = jnp.exp(jnp.where(tri, gc[..., :, None] - gc[..., None, :], -jnp.inf))
    kb = kc * bc[..., None]
    a_low = jnp.where(strict, jnp.einsum('nhid,nhjd->nhij', kb, kc) * decay, 0.0)
    rhs = jnp.concatenate([vc * bc[..., None], kb * jnp.exp(gc)[..., None]], axis=-1)
    sol = lax.linalg.triangular_solve(a_low + jnp.eye(c, dtype=F32), rhs, left_side=True,
                                      lower=True, unit_diagonal=True)
    u, w = sol[..., :DELTA_DV], sol[..., DELTA_DV:]
    qk = jnp.where(tri, jnp.einsum('nhid,nhjd->nhij', qc, kc) * decay, 0.0)
    g_last = gc[..., -1:]
    wq = jnp.concatenate([w, qc * jnp.exp(gc)[..., None]], axis=-2).astype(BF16)
    kt = jnp.swapaxes(kc * jnp.exp(g_last - gc)[..., None], -1, -2).astype(BF16)
    e_last = jnp.broadcast_to(jnp.exp(g_last)[..., None], (n, h, 1, DELTA_DV))

    def per_chunk(*blk):
        return pl.BlockSpec((None,) + blk, lambda i: (i,) + (0,) * len(blk))

    d_out, s_fin = pl.pallas_call(
        _delta_scan_kernel,
        grid=(n,),
        in_specs=[per_chunk(h, 2 * c, DELTA_DK), per_chunk(h, DELTA_DK, c), per_chunk(h, c, DELTA_DV),
                  per_chunk(h, c, c), per_chunk(h, 1, DELTA_DV),
                  pl.BlockSpec((c, h * DELTA_DV), lambda i: (i, 0)),
                  pl.BlockSpec((1, DELTA_DV), lambda i: (0, 0))],
        out_specs=[pl.BlockSpec((c, h * DELTA_DV), lambda i: (i, 0)),
                   pl.BlockSpec((h, DELTA_DK, DELTA_DV), lambda i: (0, 0, 0))],
        out_shape=[jax.ShapeDtypeStruct((t, h * DELTA_DV), BF16),
                   jax.ShapeDtypeStruct((h, DELTA_DK, DELTA_DV), F32)],
        scratch_shapes=[pltpu.VMEM((h, DELTA_DK, DELTA_DV), F32)],
        compiler_params=pltpu.CompilerParams(dimension_semantics=("arbitrary",),
                                             vmem_limit_bytes=V7X_VMEM_LIMIT),
        name="delta_scan",
    )(wq, kt, u, qk.astype(BF16), e_last, p_gate, norm_g.reshape(1, DELTA_DV))
    return d_out, s_fin, new_buf


def routed_experts(xt, expert_idx, gate, w_gate, w_up, w_down):
    n, d = xt.shape
    k = expert_idx.shape[1]
    n_exp = w_gate.shape[0]
    blk = EXPERT_BLOCK
    flat_e = expert_idx.reshape(-1)
    order = jnp.argsort(flat_e).astype(jnp.int32)
    counts = jnp.bincount(flat_e, length=n_exp).astype(jnp.int32)
    padded = (counts + blk - 1) // blk * blk
    pad_end = jnp.cumsum(padded)
    pad_start = pad_end - padded
    start = jnp.cumsum(counts) - counts
    n_blocks = -(-(n * k) // blk) + n_exp
    rows = n_blocks * blk
    blk_row0 = jnp.arange(n_blocks, dtype=jnp.int32) * blk
    block_expert = jnp.minimum(jnp.sum(pad_end[None, :] <= blk_row0[:, None], axis=1), n_exp - 1).astype(jnp.int32)
    n_used = (pad_end[-1:] // blk).astype(jnp.int32)
    r = jnp.arange(rows, dtype=jnp.int32)
    rank = (r.reshape(n_blocks, blk) - pad_start[block_expert][:, None])
    real = (rank < counts[block_expert][:, None]).reshape(rows)
    src = order[jnp.clip(start[block_expert][:, None] + rank, 0, n * k - 1).reshape(rows)]
    row_token = jnp.where(real, src // k, r % n)
    row_gate = jnp.where(real, gate.reshape(-1)[src], 0.0)
    xb = xt.astype(BF16)
    grp_blocks = -(-n_blocks // ROW_GROUPS)
    h_rows = None
    for b0 in range(0, n_blocks, grp_blocks):
        b1 = min(b0 + grp_blocks, n_blocks)
        x_rows = xb[row_token[b0 * blk:b1 * blk]]
        h_rows = grouped_swiglu_up(x_rows, block_expert[b0:b1], jnp.clip(n_used - b0, 0, b1 - b0),
                                   w_gate, w_up, blk, b0, n_blocks, h_prev=h_rows)
    y_rows = grouped_down(h_rows, row_gate[:, None], block_expert, n_used, w_down, blk)
    slot = jnp.argsort(order).astype(jnp.int32)
    e_of = flat_e.astype(jnp.int32)
    pos = pad_start[e_of] + slot - start[e_of]
    return jnp.sum(y_rows[pos].reshape(n, k, d).astype(F32), axis=1)


def moe(xt, w_router, router_bias, w_exp_gate, w_exp_up, w_exp_down, w_sh_gate, w_sh_up, w_sh_down):
    n = xt.shape[0]
    scores = jax.nn.sigmoid(jnp.matmul(xt, w_router, preferred_element_type=F32))
    biased = scores + router_bias
    per_group = N_EXPERTS // N_GROUPS
    group_score = lax.top_k(biased.reshape(n, N_GROUPS, per_group), 2)[0].sum(-1)
    _, top_groups = lax.top_k(group_score, TOPK_GROUPS)
    group_keep = jnp.any(top_groups[:, :, None] == jnp.arange(N_GROUPS)[None, None, :], axis=1)
    biased = jnp.where(jnp.repeat(group_keep, per_group, axis=1), biased, -jnp.inf)
    _, expert_idx = lax.top_k(biased, TOP_K)
    gate = jnp.take_along_axis(scores, expert_idx, axis=1)
    gate = gate / jnp.sum(gate, -1, keepdims=True) * ROUTED_SCALE
    routed = routed_experts(xt, expert_idx, gate, w_exp_gate, w_exp_up, w_exp_down)
    xb = xt.astype(BF16)
    shared = matmul(swiglu_up(xb, w_sh_gate, w_sh_up), w_sh_down)
    return routed + shared


def kernel(x_prompt, x_sample, cache_k, cache_v, cache_idx_k, page_table, state_delta, state_conv, w_in, w_conv, a_log, dt_bias, delta_norm_g, w_branch_attn, w_branch_delta, w_out, ln1_g, ln1_b, w_router, router_bias, w_exp_gate, w_exp_up, w_exp_down, w_sh_gate, w_sh_up, w_sh_down, ln2_g, ln2_b):
    b, s, d = x_prompt.shape
    db, t = x_sample.shape[:2]
    assert DEPTH == 1 and b == 1
    past = page_table.shape[1] * cache_k.shape[2]
    n_p, n_s = b * s, db * t
    (l_in, l_conv, l_alog, l_dtb, l_ng, l_ba, l_bd, l_out, l_g1, l_b1,
     l_r, l_rb, l_eg, l_eu, l_ed, l_sg, l_su, l_sd, l_g2, l_b2) = [a[0] for a in (
         w_in, w_conv, a_log, dt_bias, delta_norm_g, w_branch_attn, w_branch_delta, w_out,
         ln1_g, ln1_b, w_router, router_bias, w_exp_gate, w_exp_up, w_exp_down,
         w_sh_gate, w_sh_up, w_sh_down, ln2_g, ln2_b)]

    x_all = jnp.concatenate([x_prompt.reshape(n_p, d), x_sample.reshape(n_s, d)], axis=0)
    xb = x_all.astype(BF16)
    w_in_t = jnp.swapaxes(l_in, 0, 1)
    p_attn = matmul_rows(xb, w_in_t, 0, ATTN_COLS)
    p_delta = matmul_rows(xb, w_in_t, IN_SPLITS[5], DELTA_COLS)
    p_gate = matmul_rows(xb, w_in_t, IN_SPLITS[8], GATE_COLS)
    ap = attention_inputs(p_attn[:n_p].reshape(b, s, -1), jnp.arange(s))
    at = attention_inputs(p_attn[n_p:].reshape(db, t, -1), past + jnp.arange(t))

    a_p = prompt_sparse_attention(ap.q, ap.k, ap.v, ap.iq, ap.iw, ap.ik)
    a_s = sample_sparse_attention(at.q, at.k, at.v, at.iq, at.iw, at.ik,
                                  cache_k[0], cache_v[0], cache_idx_k[0], page_table)
    d_p, sd_p, sc_p = prompt_delta_branch(p_delta, p_gate, n_p, l_conv, l_alog, l_dtb, l_ng)
    pd_s = p_delta[n_p:].reshape(db, t, -1)
    nh = N_DELTA_HEADS
    d_s, sd_s, sc_s = gated_delta_branch(pd_s[..., :CONV_CH], pd_s[..., CONV_CH:CONV_CH + nh], pd_s[..., CONV_CH + nh:],
                                         p_gate[n_p:, :DELTA_V_WIDTH].reshape(db, t, -1), state_conv[0], state_delta[0],
                                         l_conv, l_alog, l_dtb, l_ng)

    a_all = jnp.concatenate([a_p.reshape(n_p, -1).astype(BF16), a_s.reshape(n_s, -1).astype(BF16)], axis=0)
    d_all = jnp.concatenate([d_p, d_s.reshape(n_s, -1).astype(BF16)], axis=0)
    merged = merge_branches(a_all, d_all, p_gate, DELTA_V_WIDTH, l_ba, l_bd)
    h = layer_norm(ALPHA * x_all + matmul(merged, l_out), l_g1, l_b1)
    y = layer_norm(ALPHA * h + moe(h, l_r, l_rb, l_eg, l_eu, l_ed, l_sg, l_su, l_sd), l_g2, l_b2)
    return (y[:n_p].reshape(b, s, d), y[n_p:].reshape(db, t, d),
            ap.k[None], ap.v[None], ap.ik[None], sd_p[None, None], sc_p[None, None],
            at.k[None], at.v[None], at.ik[None], sd_s[None], sc_s[None])
```

```python
import collections
import functools

import jax
import jax.numpy as jnp
import numpy as np
from jax import lax
from jax.experimental import pallas as pl
from jax.experimental.pallas import tpu as pltpu

D_MODEL = 4096
DEPTH = 1
N_HEADS = 16
N_KV_HEADS = 4
HEAD_DIM = 128
ROPE_THETA = 500000.0
N_IDX_HEADS = 16
IDX_DIM = 128
IDX_TOPK = 256
Q_BLOCK = 128
N_DELTA_HEADS = 16
DELTA_DK = 128
DELTA_DV = 128
CONV_WIDTH = 4
DELTA_CHUNK = 64
N_EXPERTS = 64
TOP_K = 8
N_GROUPS = 8
TOPK_GROUPS = 4
EXPERT_DIM = 1024
ROUTED_SCALE = 2.5
EXPERT_BLOCK = 512
ROW_GROUPS = 4
ALPHA = (2 * DEPTH) ** 0.25
LN_EPS = 1e-5
ATTN_WIDTH = N_HEADS * HEAD_DIM
KV_WIDTH = N_KV_HEADS * HEAD_DIM
IDXQ_WIDTH = N_IDX_HEADS * IDX_DIM
DELTA_QK_WIDTH = N_DELTA_HEADS * DELTA_DK
DELTA_V_WIDTH = N_DELTA_HEADS * DELTA_DV
CONV_CH = 2 * DELTA_QK_WIDTH + DELTA_V_WIDTH
IN_SIZES = (ATTN_WIDTH, KV_WIDTH, KV_WIDTH, IDXQ_WIDTH, IDX_DIM, N_IDX_HEADS, CONV_CH,
            N_DELTA_HEADS, N_DELTA_HEADS, DELTA_V_WIDTH, 2 * D_MODEL)
IN_SPLITS = tuple(int(s) for s in np.cumsum(IN_SIZES)[:-1])

V7X_VMEM_LIMIT = 56 * 1024 * 1024
BF16 = jnp.bfloat16
F32 = jnp.float32
SUBLANES = 8


def _mm_kernel(x_ref, w_ref, o_ref, wbf_ref):
    @pl.when(pl.program_id(1) == 0)
    def _():
        wbf_ref[...] = w_ref[...].astype(BF16)

    o_ref[...] = jnp.dot(x_ref[...], wbf_ref[...], preferred_element_type=F32).astype(o_ref.dtype)


def _row_tile(m):
    for tm in (1024, 1088, 512, 256, 128):
        if m % tm == 0:
            return tm
    raise ValueError(f"no row tile for {m} rows")


def matmul(x, w, out_dtype=F32, tn=512):
    m, k = x.shape
    n = w.shape[1]
    tm = _row_tile(m)
    tn = min(tn, n)
    return pl.pallas_call(
        _mm_kernel,
        grid=(pl.cdiv(n, tn), m // tm),
        in_specs=[pl.BlockSpec((tm, k), lambda j, i: (i, 0)),
                  pl.BlockSpec((k, tn), lambda j, i: (0, j))],
        out_specs=pl.BlockSpec((tm, tn), lambda j, i: (i, j)),
        out_shape=jax.ShapeDtypeStruct((m, n), out_dtype),
        scratch_shapes=[pltpu.VMEM((k, tn), BF16)],
        compiler_params=pltpu.CompilerParams(dimension_semantics=("arbitrary", "arbitrary"),
                                             vmem_limit_bytes=V7X_VMEM_LIMIT),
        name="dense_matmul",
    )(x, w)


def _swiglu_kernel(x_ref, wg_ref, wu_ref, o_ref, wg_bf, wu_bf):
    @pl.when(pl.program_id(1) == 0)
    def _():
        wg_bf[...] = wg_ref[...].astype(BF16)
        wu_bf[...] = wu_ref[...].astype(BF16)

    x = x_ref[...]
    g = jnp.dot(x, wg_bf[...], preferred_element_type=F32)
    u = jnp.dot(x, wu_bf[...], preferred_element_type=F32)
    o_ref[...] = (g * jax.nn.sigmoid(g) * u).astype(o_ref.dtype)


def swiglu_up(x, w_gate, w_up, tn=256):
    m, k = x.shape
    f = w_gate.shape[1]
    tm = _row_tile(m)
    tn = min(tn, f)
    assert f % tn == 0 and w_up.shape == w_gate.shape
    return pl.pallas_call(
        _swiglu_kernel,
        grid=(f // tn, m // tm),
        in_specs=[pl.BlockSpec((tm, k), lambda j, i: (i, 0)),
                  pl.BlockSpec((k, tn), lambda j, i: (0, j)),
                  pl.BlockSpec((k, tn), lambda j, i: (0, j))],
        out_specs=pl.BlockSpec((tm, tn), lambda j, i: (i, j)),
        out_shape=jax.ShapeDtypeStruct((m, f), BF16),
        scratch_shapes=[pltpu.VMEM((k, tn), BF16), pltpu.VMEM((k, tn), BF16)],
        compiler_params=pltpu.CompilerParams(dimension_semantics=("arbitrary", "arbitrary"),
                                             vmem_limit_bytes=V7X_VMEM_LIMIT),
        name="swiglu_up",
    )(x, w_gate, w_up)


def _merge_kernel(a_ref, d_ref, ga_ref, gd_ref, wa_ref, wd_ref, o_ref, wa_bf, wd_bf):
    @pl.when(pl.program_id(1) == 0)
    def _():
        wa_bf[...] = wa_ref[...].astype(BF16)
        wd_bf[...] = wd_ref[...].astype(BF16)

    pa = jnp.dot(a_ref[...], wa_bf[...], preferred_element_type=F32)
    pd = jnp.dot(d_ref[...], wd_bf[...], preferred_element_type=F32)
    o_ref[...] = (jax.nn.sigmoid(ga_ref[...]) * pa + jax.nn.sigmoid(gd_ref[...]) * pd).astype(o_ref.dtype)


def merge_branches(a, d, p_gate, gate_col0, w_a, w_d, tn=512):
    m, k = a.shape
    n = w_a.shape[1]
    tm = _row_tile(m)
    assert gate_col0 % tn == 0 and n % tn == 0 and w_d.shape == w_a.shape and d.shape == a.shape
    ga0, gd0 = gate_col0 // tn, (gate_col0 + n) // tn
    return pl.pallas_call(
        _merge_kernel,
        grid=(n // tn, m // tm),
        in_specs=[pl.BlockSpec((tm, k), lambda j, i: (i, 0)),
                  pl.BlockSpec((tm, k), lambda j, i: (i, 0)),
                  pl.BlockSpec((tm, tn), lambda j, i: (i, ga0 + j)),
                  pl.BlockSpec((tm, tn), lambda j, i: (i, gd0 + j)),
                  pl.BlockSpec((k, tn), lambda j, i: (0, j)),
                  pl.BlockSpec((k, tn), lambda j, i: (0, j))],
        out_specs=pl.BlockSpec((tm, tn), lambda j, i: (i, j)),
        out_shape=jax.ShapeDtypeStruct((m, n), BF16),
        scratch_shapes=[pltpu.VMEM((k, tn), BF16), pltpu.VMEM((k, tn), BF16)],
        compiler_params=pltpu.CompilerParams(dimension_semantics=("arbitrary", "arbitrary"),
                                             vmem_limit_bytes=V7X_VMEM_LIMIT),
        name="merge_branches",
    )(a, d, p_gate, p_gate, w_a, w_d)


def _mm_rows_kernel(x_ref, wt_ref, o_ref, wbf_ref):
    @pl.when(pl.program_id(1) == 0)
    def _():
        wbf_ref[...] = wt_ref[...].astype(BF16)

    o_ref[...] = lax.dot_general(x_ref[...], wbf_ref[...], (((1,), (1,)), ((), ())),
                                 preferred_element_type=F32).astype(o_ref.dtype)


def matmul_rows(x, wt, row0, nrows, out_dtype=F32, tn=512):
    m, k = x.shape
    tm = _row_tile(m)
    tn = min(tn, nrows)
    assert row0 % SUBLANES == 0 and wt.shape[1] == k
    w_spec = pl.BlockSpec((pl.Element(tn, (0, tn)), pl.Element(k)),
                          lambda j, i: (pl.multiple_of(row0 + j * tn, SUBLANES), 0))
    return pl.pallas_call(
        _mm_rows_kernel,
        grid=(pl.cdiv(nrows, tn), m // tm),
        in_specs=[pl.BlockSpec((tm, k), lambda j, i: (i, 0)), w_spec],
        out_specs=pl.BlockSpec((tm, tn), lambda j, i: (i, j)),
        out_shape=jax.ShapeDtypeStruct((m, nrows), out_dtype),
        scratch_shapes=[pltpu.VMEM((tn, k), BF16)],
        compiler_params=pltpu.CompilerParams(dimension_semantics=("arbitrary", "arbitrary"),
                                             vmem_limit_bytes=V7X_VMEM_LIMIT),
        name="dense_matmul_rows",
    )(x, wt)


def _expert_changed(be_ref, i):
    prev = be_ref[jnp.maximum(i - 1, 0)]
    return jnp.logical_or(i == 0, be_ref[i] != prev)


def _moe_up_kernel(be_ref, nu_ref, x_ref, wg_ref, wu_ref, *rest):
    h_ref, wg_bf, wu_bf = rest[-3:]
    i = pl.program_id(1)

    @pl.when(i < nu_ref[0])
    def _():
        @pl.when(_expert_changed(be_ref, i))
        def _():
            wg_bf[...] = wg_ref[...].astype(BF16)
            wu_bf[...] = wu_ref[...].astype(BF16)

        x = x_ref[...]
        g = jnp.dot(x, wg_bf[...], preferred_element_type=F32)
        u = jnp.dot(x, wu_bf[...], preferred_element_type=F32)
        h_ref[...] = (g * jax.nn.sigmoid(g) * u).astype(h_ref.dtype)


def _moe_down_kernel(be_ref, nu_ref, h_ref, gate_ref, wd_ref, y_ref, wd_bf):
    i = pl.program_id(1)

    @pl.when(i < nu_ref[0])
    def _():
        @pl.when(_expert_changed(be_ref, i))
        def _():
            wd_bf[...] = wd_ref[...].astype(BF16)

        y = jnp.dot(h_ref[...], wd_bf[...], preferred_element_type=F32)
        y_ref[...] = (y * gate_ref[...]).astype(y_ref.dtype)


def grouped_swiglu_up(x_rows, block_expert, n_used, w_gate, w_up, blk, block0, total_blocks, h_prev=None, tf=512):
    rows, d = x_rows.shape
    f = w_gate.shape[2]
    tf = min(tf, f)
    n_blocks = rows // blk
    in_specs = [pl.BlockSpec((blk, d), lambda j, i, be, nu: (i, 0)),
                pl.BlockSpec((None, d, tf), lambda j, i, be, nu: (be[i], 0, j)),
                pl.BlockSpec((None, d, tf), lambda j, i, be, nu: (be[i], 0, j))]
    operands = [block_expert, n_used, x_rows, w_gate, w_up]
    aliases = {}
    if h_prev is not None:
        in_specs.append(pl.BlockSpec(memory_space=pl.ANY))
        operands.append(h_prev)
        aliases = {len(operands) - 1: 0}
    grid_spec = pltpu.PrefetchScalarGridSpec(
        num_scalar_prefetch=2,
        grid=(f // tf, n_blocks),
        in_specs=in_specs,
        out_specs=pl.BlockSpec((blk, tf), lambda j, i, be, nu: (block0 + i, j)),
        scratch_shapes=[pltpu.VMEM((d, tf), BF16), pltpu.VMEM((d, tf), BF16)],
    )
    return pl.pallas_call(
        _moe_up_kernel, grid_spec=grid_spec,
        out_shape=jax.ShapeDtypeStruct((total_blocks * blk, f), BF16),
        input_output_aliases=aliases,
        compiler_params=pltpu.CompilerParams(dimension_semantics=("arbitrary", "arbitrary"),
                                             vmem_limit_bytes=V7X_VMEM_LIMIT),
        name="moe_up",
    )(*operands)


def grouped_down(h_rows, row_gate, block_expert, n_used, w_down, blk, tn=2048):
    rows, f = h_rows.shape
    d = w_down.shape[2]
    tn = min(tn, d)
    n_blocks = rows // blk
    grid_spec = pltpu.PrefetchScalarGridSpec(
        num_scalar_prefetch=2,
        grid=(d // tn, n_blocks),
        in_specs=[pl.BlockSpec((blk, f), lambda j, i, be, nu: (i, 0)),
                  pl.BlockSpec((blk, 1), lambda j, i, be, nu: (i, 0)),
                  pl.BlockSpec((None, f, tn), lambda j, i, be, nu: (be[i], 0, j))],
        out_specs=pl.BlockSpec((blk, tn), lambda j, i, be, nu: (i, j)),
        scratch_shapes=[pltpu.VMEM((f, tn), BF16)],
    )
    return pl.pallas_call(
        _moe_down_kernel, grid_spec=grid_spec,
        out_shape=jax.ShapeDtypeStruct((rows, d), BF16),
        compiler_params=pltpu.CompilerParams(dimension_semantics=("arbitrary", "arbitrary"),
                                             vmem_limit_bytes=V7X_VMEM_LIMIT),
        name="moe_down",
    )(block_expert, n_used, h_rows, row_gate, w_down)


def layer_norm(x, g, b):
    xc = x - jnp.mean(x, -1, keepdims=True)
    var = jnp.mean(xc * xc, -1, keepdims=True)
    return xc * lax.rsqrt(var + LN_EPS) * g + b


def l2norm(x):
    return x * lax.rsqrt(jnp.sum(x * x, -1, keepdims=True) + 1e-6)


def partial_rope(x, pos):
    rot = x.shape[-1] // 4
    half = rot // 2
    inv_freq = ROPE_THETA ** (-jnp.arange(half, dtype=F32) / half)
    ang = pos.astype(F32)[:, None] * inv_freq[None, :]
    cos = jnp.cos(ang)[:, None, :]
    sin = jnp.sin(ang)[:, None, :]
    x1, x2, rest = x[..., :half], x[..., half:rot], x[..., rot:]
    return jnp.concatenate([x1 * cos - x2 * sin, x2 * cos + x1 * sin, rest], axis=-1)


ATTN_COLS = IN_SPLITS[5]
DELTA_COLS = IN_SPLITS[8] - IN_SPLITS[5]
GATE_COLS = int(sum(IN_SIZES)) - IN_SPLITS[8]
AttnInputs = collections.namedtuple("AttnInputs", "q k v iq ik iw")


def attention_inputs(p, pos):
    b, t, _ = p.shape
    q, k, v, iq, ik, iw = jnp.split(p, IN_SPLITS[:5], axis=-1)
    q = partial_rope(q.reshape(b, t, N_HEADS, HEAD_DIM), pos)
    k = partial_rope(k.reshape(b, t, N_KV_HEADS, HEAD_DIM), pos)
    v = v.reshape(b, t, N_KV_HEADS, HEAD_DIM)
    iq = partial_rope(iq.reshape(b, t, N_IDX_HEADS, IDX_DIM), pos)
    ik = partial_rope(ik[:, :, None, :], pos)[:, :, 0, :]
    iw = iw * (N_IDX_HEADS ** -0.5 * IDX_DIM ** -0.5)
    return AttnInputs(q, k, v, iq, ik, iw)


KEY_CHUNK = 512
IDX_HEAD_GROUP = 4
MASKED = -1e30
SOFTMAX_LOG2_SCALE = HEAD_DIM ** -0.5 * float(np.log2(np.e))
INT32_MIN = -2 ** 31
INT32_MAX = 2 ** 31 - 1
NEG_INF_KEY = int(np.int32(np.uint32(0xFF800000) ^ np.uint32(0x7FFFFFFF)))


def _sort_key(x):
    bits = lax.bitcast_convert_type(x, jnp.int32)
    return bits ^ (jnp.right_shift(bits, 31) & jnp.int32(INT32_MAX))


def _lane_tile(x, n):
    return x if n == 1 else jnp.concatenate([x] * n, axis=1)


def _count_rows(keys_ref, nkc, preds):
    _, rows, kc = keys_ref.shape

    def body(c, parts):
        keys = keys_ref[c]
        out = []
        for pred, part in zip(preds, parts):
            hit = pred(keys, c)
            for j in range(kc // 128):
                part = part + hit[:, j * 128:(j + 1) * 128]
            out.append(part)
        return tuple(out)

    parts = lax.fori_loop(0, nkc, body, tuple(jnp.zeros((rows, 128), F32) for _ in preds))
    return [jnp.broadcast_to(jnp.sum(part, axis=1, keepdims=True), (rows, 128)) for part in parts]


def _topk_selection(keys_ref, nkc, n_sel, radix_bits):
    _, rows, kc = keys_ref.shape
    reps = kc // 128
    col = lax.broadcasted_iota(jnp.int32, (rows, kc), 1)

    def thr_digit(it, carry):
        thr, cnt_thr = carry
        shift = 32 - radix_bits * (it + 1)
        cands = [thr + jnp.left_shift(jnp.int32(j), shift) for j in range(1, 2 ** radix_bits)]
        cands_w = [_lane_tile(cand, reps) for cand in cands]
        cnts = _count_rows(keys_ref, nkc, [lambda keys, c, cw=cw: jnp.where(keys >= cw, 1.0, 0.0) for cw in cands_w])
        for cand, cnt in zip(cands, cnts):
            ok = cnt >= n_sel
            thr = jnp.where(ok, cand, thr)
            cnt_thr = jnp.where(ok, cnt, cnt_thr)
        return thr, cnt_thr

    thr0 = jnp.full((rows, 128), INT32_MIN, jnp.int32)
    cnt0 = jnp.broadcast_to(jnp.asarray(nkc * kc).astype(F32), (rows, 128))
    thr, cnt_thr = lax.fori_loop(0, 32 // radix_bits, thr_digit, (thr0, cnt0))

    short = thr == NEG_INF_KEY
    thr_w = _lane_tile(thr, reps)
    cnt_gt, = _count_rows(keys_ref, nkc, [lambda keys, c: jnp.where(keys > thr_w, 1.0, 0.0)])
    need = n_sel - cnt_gt
    tied = jnp.logical_and(cnt_thr > n_sel, jnp.logical_not(short))

    def tie_limit():
        def idx_bit(it, x):
            cand = x + jnp.left_shift(jnp.int32(1), 30 - it)
            cand_w = _lane_tile(cand, reps)
            cnt, = _count_rows(keys_ref, nkc, [lambda keys, c: jnp.where(
                keys == thr_w, jnp.where(c * kc + col < cand_w, 1.0, 0.0), 0.0)])
            return jnp.where(cnt < need, cand, x)
        return lax.fori_loop(0, 31, idx_bit, jnp.zeros((rows, 128), jnp.int32))

    any_tied = jnp.max(jnp.where(tied, 1.0, 0.0)) > 0.0
    lim = lax.cond(any_tied, tie_limit, lambda: jnp.full((rows, 128), INT32_MAX, jnp.int32))
    lim = jnp.where(short, -1, jnp.where(tied, lim, INT32_MAX))
    return thr, lim


def _selected(keys, index, thr_w, lim_w, yes, no):
    keep_tie = jnp.where(index <= lim_w, yes, no)
    return jnp.where(keys > thr_w, yes, jnp.where(keys == thr_w, keep_tie, no))


def _dsa_prompt_kernel(iq_ref, iw_ref, ikt_ref, q_ref, kt_ref, v_ref, o_ref,
                       keys_ref, m_ref, l_ref, acc_ref, *, n_sel):
    qb, kc = Q_BLOCK, KEY_CHUNK
    reps = kc // 128
    grp = N_HEADS // N_KV_HEADS
    i = pl.program_id(0)
    nkc = (i * qb + qb + kc - 1) // kc
    qpos = i * qb + lax.broadcasted_iota(jnp.int32, (qb, kc), 0)
    col = lax.broadcasted_iota(jnp.int32, (qb, kc), 1)

    def score_chunk(c, carry):
        ikc = ikt_ref[c]
        acc = jnp.zeros((qb, kc), F32)
        for h0 in range(0, N_IDX_HEADS, IDX_HEAD_GROUP):
            iq_rows = iq_ref[h0:h0 + IDX_HEAD_GROUP].reshape(IDX_HEAD_GROUP * qb, IDX_DIM)
            s = jnp.dot(iq_rows, ikc, preferred_element_type=F32)
            for j in range(IDX_HEAD_GROUP):
                acc = acc + jnp.maximum(s[j * qb:(j + 1) * qb], 0.0) * iw_ref[:, h0 + j:h0 + j + 1]
        acc = jnp.where(c * kc + col <= qpos, acc, -jnp.inf)
        keys_ref[c] = _sort_key(acc)
        return carry

    lax.fori_loop(0, nkc, score_chunk, 0)

    thr, lim = _topk_selection(keys_ref, nkc, n_sel, radix_bits=1)
    thr_w, lim_w = _lane_tile(thr, reps), _lane_tile(lim, reps)

    m_ref[...] = jnp.full(m_ref.shape, MASKED, F32)
    l_ref[...] = jnp.zeros(l_ref.shape, F32)
    acc_ref[...] = jnp.zeros(acc_ref.shape, F32)
    rows = grp * qb

    def attend_chunk(c, carry):
        bias = _selected(keys_ref[c], c * kc + col, thr_w, lim_w, 0.0, MASKED)
        for n in range(N_KV_HEADS):
            r = pl.ds(n * rows, rows)
            qn = q_ref[n * grp:(n + 1) * grp].reshape(rows, HEAD_DIM)
            s = jnp.dot(qn, kt_ref[n, c], preferred_element_type=F32) * SOFTMAX_LOG2_SCALE
            s = (s.reshape(grp, qb, kc) + bias[None]).reshape(rows, kc)
            m_prev = m_ref[r, :]
            m_new = jnp.maximum(m_prev, jnp.max(s, axis=1, keepdims=True))
            alpha = jnp.exp2(m_prev - m_new)
            p = jnp.exp2(s - _lane_tile(m_new, reps))
            l_ref[r, :] = alpha * l_ref[r, :] + jnp.sum(p, axis=1, keepdims=True)
            acc_ref[r, :] = alpha * acc_ref[r, :] + jnp.dot(p.astype(BF16), v_ref[n, c],
                                                           preferred_element_type=F32)
            m_ref[r, :] = m_new
        return carry

    lax.fori_loop(0, nkc, attend_chunk, 0)
    for h in range(N_HEADS):
        r = pl.ds(h * qb, qb)
        o_ref[:, h * HEAD_DIM:(h + 1) * HEAD_DIM] = (acc_ref[r, :] / l_ref[r, :]).astype(o_ref.dtype)


def prompt_sparse_attention(q, k, v, iq, iw, ik):
    b, s = q.shape[:2]
    assert b == 1 and s % KEY_CHUNK == 0 and KEY_CHUNK % Q_BLOCK == 0 and KEY_CHUNK >= IDX_TOPK
    n_sel = min(IDX_TOPK, s // 4)
    nb, nc, kc = s // Q_BLOCK, s // KEY_CHUNK, KEY_CHUNK

    def head_major(a):
        return a.astype(BF16).reshape(nb, Q_BLOCK, a.shape[2], a.shape[3]).transpose(0, 2, 1, 3)

    ikt = ik.astype(BF16).reshape(nc, kc, IDX_DIM).transpose(0, 2, 1)
    kt = k.astype(BF16).reshape(nc, kc, N_KV_HEADS, HEAD_DIM).transpose(2, 0, 3, 1)
    vc = v.astype(BF16).reshape(nc, kc, N_KV_HEADS, HEAD_DIM).transpose(2, 0, 1, 3)
    resident = dict(pipeline_mode=pl.Buffered(1))
    out = pl.pallas_call(
        functools.partial(_dsa_prompt_kernel, n_sel=n_sel),
        grid=(nb,),
        in_specs=[pl.BlockSpec((None, N_IDX_HEADS, Q_BLOCK, IDX_DIM), lambda i: (i, 0, 0, 0)),
                  pl.BlockSpec((Q_BLOCK, N_IDX_HEADS), lambda i: (i, 0)),
                  pl.BlockSpec((nc, IDX_DIM, kc), lambda i: (0, 0, 0), **resident),
                  pl.BlockSpec((None, N_HEADS, Q_BLOCK, HEAD_DIM), lambda i: (i, 0, 0, 0)),
                  pl.BlockSpec((N_KV_HEADS, nc, HEAD_DIM, kc), lambda i: (0, 0, 0, 0), **resident),
                  pl.BlockSpec((N_KV_HEADS, nc, kc, HEAD_DIM), lambda i: (0, 0, 0, 0), **resident)],
        out_specs=pl.BlockSpec((Q_BLOCK, N_HEADS * HEAD_DIM), lambda i: (i, 0)),
        out_shape=jax.ShapeDtypeStruct((s, N_HEADS * HEAD_DIM), BF16),
        scratch_shapes=[pltpu.VMEM((nc, Q_BLOCK, kc), jnp.int32),
                        pltpu.VMEM((N_HEADS * Q_BLOCK, 128), F32),
                        pltpu.VMEM((N_HEADS * Q_BLOCK, 128), F32),
                        pltpu.VMEM((N_HEADS * Q_BLOCK, HEAD_DIM), F32)],
        compiler_params=pltpu.CompilerParams(dimension_semantics=("arbitrary",),
                                             vmem_limit_bytes=V7X_VMEM_LIMIT),
        name="dsa_prompt",
    )(head_major(iq), iw.reshape(s, N_IDX_HEADS), ikt, head_major(q), kt, vc)
    return out.reshape(b, s, N_HEADS * HEAD_DIM)


T_PAD = 8


def _dsa_sample_kernel(pt_ref, iq_ref, iw_ref, q_ref, iknew_ref, knew_ref, vnew_ref, *rest, n_sel, n_pages):
    del pt_ref
    ik_pages, k_pages, v_pages = rest[:n_pages], rest[n_pages:2 * n_pages], rest[2 * n_pages:3 * n_pages]
    o_ref, keys_ref, s_ref = rest[3 * n_pages:]
    page = ik_pages[0].shape[0]
    kv_rows = k_pages[0].shape[0]
    rows = N_HEADS * T_PAD
    nkc = n_pages + 1
    nt = (((1,), (1,)), ((), ()))
    row_t = lax.broadcasted_iota(jnp.int32, (T_PAD, page), 0)
    col = lax.broadcasted_iota(jnp.int32, (T_PAD, page), 1)

    for c in range(nkc):
        ikc = ik_pages[c][...].astype(BF16) if c < n_pages else iknew_ref[...]
        s = lax.dot_general(iq_ref[...], ikc, nt, preferred_element_type=F32)
        x = jnp.maximum(s, 0.0) * iw_ref[...]
        sc = jnp.sum(x.reshape(T_PAD, N_IDX_HEADS, page), axis=1)
        if c == n_pages:
            sc = jnp.where(col <= row_t, sc, -jnp.inf)
        keys_ref[c] = _sort_key(sc)

    thr, lim = _topk_selection(keys_ref, nkc, n_sel, radix_bits=4)

    scale = SOFTMAX_LOG2_SCALE
    spread = (lax.broadcasted_iota(jnp.int32, (page, kv_rows), 1) // N_KV_HEADS
              == lax.broadcasted_iota(jnp.int32, (page, kv_rows), 0)).astype(BF16)
    own_head = (lax.broadcasted_iota(jnp.int32, (rows, kv_rows), 0) // (rows // N_KV_HEADS)
                == lax.broadcasted_iota(jnp.int32, (rows, kv_rows), 1) % N_KV_HEADS).astype(F32)
    for c in range(nkc):
        kc_ = k_pages[c][...].astype(BF16) if c < n_pages else knew_ref[...]
        s = lax.dot_general(q_ref[...], kc_, nt, preferred_element_type=F32) * scale
        sel = _selected(keys_ref[c], c * page + col, thr, lim, 1.0, 0.0).astype(BF16)
        sel = jnp.dot(sel, spread, preferred_element_type=F32)
        keep = jnp.broadcast_to(sel[None], (N_HEADS, T_PAD, kv_rows)).reshape(rows, kv_rows) * own_head
        s_ref[:, c * kv_rows:(c + 1) * kv_rows] = jnp.where(keep > 0.5, s, MASKED)
    s = s_ref[...]
    p = jnp.exp2(s - jnp.max(s, axis=1, keepdims=True))
    inv_l = 1.0 / jnp.sum(p, axis=1, keepdims=True)
    p = p.astype(BF16)
    o = jnp.zeros((rows, HEAD_DIM), F32)
    for c in range(nkc):
        vc_ = v_pages[c][...].astype(BF16) if c < n_pages else vnew_ref[...]
        o = o + jnp.dot(p[:, c * kv_rows:(c + 1) * kv_rows], vc_, preferred_element_type=F32)
    o_ref[...] = o * inv_l


def sample_sparse_attention(q, k_new, v_new, iq, iw, ik_new, cache_k, cache_v, cache_idx_k, page_table):
    db, t = q.shape[:2]
    n_pool, page = cache_k.shape[:2]
    n_pages = page_table.shape[1]
    n_keys = n_pages * page + t
    n_sel = min(IDX_TOPK, n_keys // 4)
    grp = N_HEADS // N_KV_HEADS
    assert t <= T_PAD <= page and page == 128 and HEAD_DIM == 128 and IDX_DIM == 128

    def pad_t(a, to):
        return jnp.pad(a, [(0, 0), (0, to - a.shape[1])] + [(0, 0)] * (a.ndim - 2))

    kv_rows = page * N_KV_HEADS
    rows = N_HEADS * T_PAD
    iq_rows = pad_t(iq, T_PAD).astype(BF16).reshape(db, T_PAD * N_IDX_HEADS, IDX_DIM)
    iw_rows = pad_t(iw, T_PAD).reshape(db, T_PAD * N_IDX_HEADS, 1)
    q_rows = pad_t(q, T_PAD).astype(BF16).transpose(0, 2, 1, 3).reshape(db, rows, HEAD_DIM)
    iknew = pad_t(ik_new, page).astype(BF16)
    knew = pad_t(k_new, page).astype(BF16).reshape(db, kv_rows, HEAD_DIM)
    vnew = pad_t(v_new, page).astype(BF16).reshape(db, kv_rows, HEAD_DIM)
    ck = cache_k.reshape(n_pool, kv_rows, HEAD_DIM)
    cv = cache_v.reshape(n_pool, kv_rows, HEAD_DIM)

    def per_batch(*blk):
        return pl.BlockSpec((None,) + blk, lambda b, pt: (b,) + (0,) * len(blk))

    def paged(nrows):
        return [pl.BlockSpec((None, nrows, HEAD_DIM), lambda b, pt, j=j: (pt[b, j], 0, 0)) for j in range(n_pages)]

    grid_spec = pltpu.PrefetchScalarGridSpec(
        num_scalar_prefetch=1,
        grid=(db,),
        in_specs=[per_batch(T_PAD * N_IDX_HEADS, IDX_DIM), per_batch(T_PAD * N_IDX_HEADS, 1),
                  per_batch(rows, HEAD_DIM), per_batch(page, IDX_DIM), per_batch(kv_rows, HEAD_DIM),
                  per_batch(kv_rows, HEAD_DIM)]
        + paged(page) + paged(kv_rows) + paged(kv_rows),
        out_specs=per_batch(rows, HEAD_DIM),
        scratch_shapes=[pltpu.VMEM((n_pages + 1, T_PAD, page), jnp.int32),
                        pltpu.VMEM((rows, (n_pages + 1) * kv_rows), F32)],
    )
    o = pl.pallas_call(
        functools.partial(_dsa_sample_kernel, n_sel=n_sel, n_pages=n_pages),
        grid_spec=grid_spec,
        out_shape=jax.ShapeDtypeStruct((db, rows, HEAD_DIM), F32),
        compiler_params=pltpu.CompilerParams(dimension_semantics=("arbitrary",),
                                             vmem_limit_bytes=V7X_VMEM_LIMIT),
        name="dsa_sample",
    )(page_table, iq_rows, iw_rows, q_rows, iknew, knew, vnew,
      *([cache_idx_k] * n_pages), *([ck] * n_pages), *([cv] * n_pages))
    o = o.reshape(db, N_KV_HEADS, grp, T_PAD, HEAD_DIM)[:, :, :, :t]
    return o.transpose(0, 3, 1, 2, 4).reshape(db, t, N_HEADS * HEAD_DIM)


def causal_conv(xpad, w_conv):
    c = xpad.shape[-1]
    return lax.conv_general_dilated(xpad, w_conv[:, None, :], window_strides=(1,),
                                    padding='VALID', dimension_numbers=('NWC', 'WIO', 'NWC'),
                                    feature_group_count=c)


def chunk_gated_delta(q, k, v, beta, g, s0):
    b, t, h, dk = q.shape
    dv = v.shape[-1]
    c = min(DELTA_CHUNK, t)
    n = -(-t // c)
    pad = n * c - t

    def chunks(a):
        a = jnp.pad(a, [(0, 0), (0, pad)] + [(0, 0)] * (a.ndim - 2))
        a = a.reshape((b, n, c) + a.shape[2:])
        return jnp.moveaxis(a, (1, 3), (0, 2))

    qc, kc, vc, bc, gc = [chunks(a) for a in (q, k, v, beta, g)]
    gc = jnp.cumsum(gc, axis=-1)
    tri = jnp.tril(jnp.ones((c, c), bool))
    strict = jnp.tril(jnp.ones((c, c), bool), -1)
    decay = jnp.exp(jnp.where(tri, gc[..., :, None] - gc[..., None, :], -jnp.inf))
    kb = kc * bc[..., None]
    a_low = jnp.where(strict, jnp.einsum('nbhid,nbhjd->nbhij', kb, kc) * decay, 0.0)
    rhs = jnp.concatenate([vc * bc[..., None], kb * jnp.exp(gc)[..., None]], axis=-1)
    sol = lax.linalg.triangular_solve(a_low + jnp.eye(c, dtype=F32), rhs, left_side=True,
                                      lower=True, unit_diagonal=True)
    u, w = sol[..., :dv], sol[..., dv:]
    qk = jnp.where(tri, jnp.einsum('nbhid,nbhjd->nbhij', qc, kc) * decay, 0.0)

    def step(state, xs):
        q_i, k_i, u_i, w_i, g_i, qk_i = xs
        v_corr = u_i - jnp.einsum('bhck,bhkv->bhcv', w_i, state)
        o_i = (jnp.einsum('bhck,bhkv->bhcv', q_i * jnp.exp(g_i)[..., None], state)
               + jnp.einsum('bhij,bhjv->bhiv', qk_i, v_corr))
        g_last = g_i[..., -1:]
        state = (state * jnp.exp(g_last)[..., None]
                 + jnp.einsum('bhck,bhcv->bhkv', k_i * jnp.exp(g_last - g_i)[..., None], v_corr))
        return state, o_i

    s_final, o = lax.scan(step, s0, (qc, kc, u, w, gc, qk))
    o = jnp.moveaxis(o, (0, 2), (1, 3)).reshape(b, n * c, h, dv)[:, :t]
    return o, s_final


def gated_delta_branch(qkv, b_d, a_d, z, conv_buf, s0, w_conv, a_log, dt_bias, norm_g):
    b, t, _ = qkv.shape
    xpad = jnp.concatenate([conv_buf, qkv], axis=1)
    new_buf = xpad[:, -(CONV_WIDTH - 1):]
    hc = jax.nn.silu(causal_conv(xpad, w_conv))
    q, k, v = jnp.split(hc, (DELTA_QK_WIDTH, 2 * DELTA_QK_WIDTH), axis=-1)
    q = l2norm(q.reshape(b, t, N_DELTA_HEADS, DELTA_DK)) * DELTA_DK ** -0.5
    k = l2norm(k.reshape(b, t, N_DELTA_HEADS, DELTA_DK))
    v = v.reshape(b, t, N_DELTA_HEADS, DELTA_DV)
    beta = jax.nn.sigmoid(b_d)
    g = -jnp.exp(a_log) * jax.nn.softplus(a_d + dt_bias)
    o, s_new = chunk_gated_delta(q, k, v, beta, g, s0)
    o = o * lax.rsqrt(jnp.mean(o * o, -1, keepdims=True) + 1e-6) * norm_g
    o = o * jax.nn.silu(z.reshape(b, t, N_DELTA_HEADS, DELTA_DV))
    return o.reshape(b, t, DELTA_V_WIDTH), s_new, new_buf


DELTA_TOKEN_BLOCK = 256


def _delta_pre_kernel(x_ref, halo_ref, w_ref, q_ref, k_ref, v_ref, xe_ref):
    i = pl.program_id(0)
    tb = x_ref.shape[0]
    xe_ref[0:SUBLANES, :] = jnp.where(i == 0, 0.0, halo_ref[...])
    xe_ref[SUBLANES:SUBLANES + tb, :] = x_ref[...]
    outs = (q_ref, k_ref, v_ref)
    for col in range(3 * N_DELTA_HEADS):
        cs = slice(col * DELTA_DK, (col + 1) * DELTA_DK)
        y = sum(xe_ref[SUBLANES - (CONV_WIDTH - 1) + tap:SUBLANES - (CONV_WIDTH - 1) + tap + tb, cs] * w_ref[tap:tap + 1, cs]
                for tap in range(CONV_WIDTH))
        y = y * jax.nn.sigmoid(y)
        if col < 2 * N_DELTA_HEADS:
            y = y * lax.rsqrt(jnp.sum(y * y, axis=1, keepdims=True) + 1e-6)
        if col < N_DELTA_HEADS:
            y = y * DELTA_DK ** -0.5
        for cc in range(tb // DELTA_CHUNK):
            outs[col // N_DELTA_HEADS][cc, col % N_DELTA_HEADS] = y[cc * DELTA_CHUNK:(cc + 1) * DELTA_CHUNK]


def delta_pre(qkv, t, w_conv):
    c = w_conv.shape[1]
    tb = DELTA_TOKEN_BLOCK
    assert t % tb == 0 and tb % DELTA_CHUNK == 0 and DELTA_DK == DELTA_DV
    per_blk = tb // DELTA_CHUNK
    out_spec = pl.BlockSpec((per_blk, N_DELTA_HEADS, DELTA_CHUNK, DELTA_DK), lambda i: (i, 0, 0, 0))
    out_shape = jax.ShapeDtypeStruct((t // DELTA_CHUNK, N_DELTA_HEADS, DELTA_CHUNK, DELTA_DK), F32)
    return pl.pallas_call(
        _delta_pre_kernel,
        grid=(t // tb,),
        in_specs=[pl.BlockSpec((tb, c), lambda i: (i, 0)),
                  pl.BlockSpec((SUBLANES, c), lambda i: (jnp.maximum(i * (tb // SUBLANES) - 1, 0), 0)),
                  pl.BlockSpec((CONV_WIDTH, c), lambda i: (0, 0))],
        out_specs=[out_spec] * 3,
        out_shape=[out_shape] * 3,
        scratch_shapes=[pltpu.VMEM((tb + SUBLANES, c), F32)],
        compiler_params=pltpu.CompilerParams(dimension_semantics=("arbitrary",),
                                             vmem_limit_bytes=V7X_VMEM_LIMIT),
        name="delta_pre",
    )(qkv, qkv, w_conv)


def _delta_scan_kernel(wq_ref, kt_ref, u_ref, qk_ref, el_ref, z_ref, ng_ref, o_ref, sfin_ref, s_ref):
    n = pl.program_id(0)
    c = DELTA_CHUNK

    @pl.when(n == 0)
    def _():
        s_ref[...] = jnp.zeros(s_ref.shape, F32)

    for h in range(N_DELTA_HEADS):
        s = s_ref[h]
        ws = jnp.dot(wq_ref[h], s.astype(BF16), preferred_element_type=F32)
        v_corr = (u_ref[h] - ws[:c]).astype(BF16)
        o = ws[c:] + jnp.dot(qk_ref[h], v_corr, preferred_element_type=F32)
        s_ref[h] = s * el_ref[h] + jnp.dot(kt_ref[h], v_corr, preferred_element_type=F32)
        o = o * lax.rsqrt(jnp.mean(o * o, axis=1, keepdims=True) + 1e-6) * ng_ref[...]
        zz = z_ref[:, h * DELTA_DV:(h + 1) * DELTA_DV]
        o_ref[:, h * DELTA_DV:(h + 1) * DELTA_DV] = (o * (zz * jax.nn.sigmoid(zz))).astype(o_ref.dtype)

    @pl.when(n == pl.num_programs(0) - 1)
    def _():
        sfin_ref[...] = s_ref[...]


def prompt_delta_branch(p_delta, p_gate, t, w_conv, a_log, dt_bias, norm_g):
    assert t >= CONV_WIDTH - 1
    c, h = DELTA_CHUNK, N_DELTA_HEADS
    n = t // c
    new_buf = p_delta[t - (CONV_WIDTH - 1):t, :CONV_CH]
    qc, kc, vc = delta_pre(p_delta, t, w_conv)
    beta = jax.nn.sigmoid(p_delta[:t, CONV_CH:CONV_CH + h])
    g = -jnp.exp(a_log) * jax.nn.softplus(p_delta[:t, CONV_CH + h:CONV_CH + 2 * h] + dt_bias)
    bc = beta.reshape(n, c, h).transpose(0, 2, 1)
    gc = jnp.cumsum(g.reshape(n, c, h).transpose(0, 2, 1), axis=-1)
    tri = jnp.tril(jnp.ones((c, c), bool))
    strict = jnp.tril(jnp.ones((c, c), bool), -1)
    decay = jnp.exp(jnp.where(tri, gc[..., :, None] - gc[..., None, :], -jnp.inf))
    kb = kc * bc[..., None]
    a_low = jnp.where(strict, jnp.einsum('nhid,nhjd->nhij', kb, kc) * decay, 0.0)
    rhs = jnp.concatenate([vc * bc[..., None], kb * jnp.exp(gc)[..., None]], axis=-1)
    sol = lax.linalg.triangular_solve(a_low + jnp.eye(c, dtype=F32), rhs, left_side=True,
                                      lower=True, unit_diagonal=True)
    u, w = sol[..., :DELTA_DV], sol[..., DELTA_DV:]
    qk = jnp.where(tri, jnp.einsum('nhid,nhjd->nhij', qc, kc) * decay, 0.0)
    g_last = gc[..., -1:]
    wq = jnp.concatenate([w, qc * jnp.exp(gc)[..., None]], axis=-2).astype(BF16)
    kt = jnp.swapaxes(kc * jnp.exp(g_last - gc)[..., None], -1, -2).astype(BF16)
    e_last = jnp.broadcast_to(jnp.exp(g_last)[..., None], (n, h, 1, DELTA_DV))

    def per_chunk(*blk):
        return pl.BlockSpec((None,) + blk, lambda i: (i,) + (0,) * len(blk))

    d_out, s_fin = pl.pallas_call(
        _delta_scan_kernel,
        grid=(n,),
        in_specs=[per_chunk(h, 2 * c, DELTA_DK), per_chunk(h, DELTA_DK, c), per_chunk(h, c, DELTA_DV),
                  per_chunk(h, c, c), per_chunk(h, 1, DELTA_DV),
                  pl.BlockSpec((c, h * DELTA_DV), lambda i: (i, 0)),
                  pl.BlockSpec((1, DELTA_DV), lambda i: (0, 0))],
        out_specs=[pl.BlockSpec((c, h * DELTA_DV), lambda i: (i, 0)),
                   pl.BlockSpec((h, DELTA_DK, DELTA_DV), lambda i: (0, 0, 0))],
        out_shape=[jax.ShapeDtypeStruct((t, h * DELTA_DV), BF16),
                   jax.ShapeDtypeStruct((h, DELTA_DK, DELTA_DV), F32)],
        scratch_shapes=[pltpu.VMEM((h, DELTA_DK, DELTA_DV), F32)],
        compiler_params=pltpu.CompilerParams(dimension_semantics=("arbitrary",),
                                             vmem_limit_bytes=V7X_VMEM_LIMIT),
        name="delta_scan",
    )(wq, kt, u, qk.astype(BF16), e_last, p_gate, norm_g.reshape(1, DELTA_DV))
    return d_out, s_fin, new_buf


def routed_experts(xt, expert_idx, gate, w_gate, w_up, w_down):
    n, d = xt.shape
    k = expert_idx.shape[1]
    n_exp = w_gate.shape[0]
    blk = EXPERT_BLOCK
    flat_e = expert_idx.reshape(-1)
    order = jnp.argsort(flat_e).astype(jnp.int32)
    counts = jnp.bincount(flat_e, length=n_exp).astype(jnp.int32)
    padded = (counts + blk - 1) // blk * blk
    pad_end = jnp.cumsum(padded)
    pad_start = pad_end - padded
    start = jnp.cumsum(counts) - counts
    n_blocks = -(-(n * k) // blk) + n_exp
    rows = n_blocks * blk
    blk_row0 = jnp.arange(n_blocks, dtype=jnp.int32) * blk
    block_expert = jnp.minimum(jnp.sum(pad_end[None, :] <= blk_row0[:, None], axis=1), n_exp - 1).astype(jnp.int32)
    n_used = (pad_end[-1:] // blk).astype(jnp.int32)
    r = jnp.arange(rows, dtype=jnp.int32)
    rank = (r.reshape(n_blocks, blk) - pad_start[block_expert][:, None])
    real = (rank < counts[block_expert][:, None]).reshape(rows)
    src = order[jnp.clip(start[block_expert][:, None] + rank, 0, n * k - 1).reshape(rows)]
    row_token = jnp.where(real, src // k, r % n)
    row_gate = jnp.where(real, gate.reshape(-1)[src], 0.0)
    xb = xt.astype(BF16)
    grp_blocks = -(-n_blocks // ROW_GROUPS)
    h_rows = None
    for b0 in range(0, n_blocks, grp_blocks):
        b1 = min(b0 + grp_blocks, n_blocks)
        x_rows = xb[row_token[b0 * blk:b1 * blk]]
        h_rows = grouped_swiglu_up(x_rows, block_expert[b0:b1], jnp.clip(n_used - b0, 0, b1 - b0),
                                   w_gate, w_up, blk, b0, n_blocks, h_prev=h_rows)
    y_rows = grouped_down(h_rows, row_gate[:, None], block_expert, n_used, w_down, blk)
    slot = jnp.argsort(order).astype(jnp.int32)
    e_of = flat_e.astype(jnp.int32)
    pos = pad_start[e_of] + slot - start[e_of]
    return y_rows[pos].reshape(n, k * d)


def moe(xt, w_router, router_bias, w_exp_gate, w_exp_up, w_exp_down, w_sh_gate, w_sh_up, w_sh_down):
    n = xt.shape[0]
    scores = jax.nn.sigmoid(jnp.matmul(xt, w_router, preferred_element_type=F32))
    biased = scores + router_bias
    per_group = N_EXPERTS // N_GROUPS
    group_score = lax.top_k(biased.reshape(n, N_GROUPS, per_group), 2)[0].sum(-1)
    _, top_groups = lax.top_k(group_score, TOPK_GROUPS)
    group_keep = jnp.any(top_groups[:, :, None] == jnp.arange(N_GROUPS)[None, None, :], axis=1)
    biased = jnp.where(jnp.repeat(group_keep, per_group, axis=1), biased, -jnp.inf)
    _, expert_idx = lax.top_k(biased, TOP_K)
    gate = jnp.take_along_axis(scores, expert_idx, axis=1)
    gate = gate / jnp.sum(gate, -1, keepdims=True) * ROUTED_SCALE
    routed_parts = routed_experts(xt, expert_idx, gate, w_exp_gate, w_exp_up, w_exp_down)
    xb = xt.astype(BF16)
    shared = matmul(swiglu_up(xb, w_sh_gate, w_sh_up), w_sh_down)
    return routed_parts, shared


def _combine_ln_kernel(y_ref, sh_ref, h_ref, g_ref, b_ref, o_ref):
    d = h_ref.shape[1]
    acc = ALPHA * h_ref[...] + sh_ref[...]
    for j in range(y_ref.shape[1] // d):
        acc = acc + y_ref[:, j * d:(j + 1) * d].astype(F32)
    xc = acc - jnp.mean(acc, axis=1, keepdims=True)
    var = jnp.mean(xc * xc, axis=1, keepdims=True)
    o_ref[...] = xc * lax.rsqrt(var + LN_EPS) * g_ref[...] + b_ref[...]


def combine_layer_norm(routed_parts, shared, h, g, b, tm=128):
    n, d = h.shape
    assert n % tm == 0 and routed_parts.shape[1] % d == 0
    row = lambda i: (i, 0)
    return pl.pallas_call(
        _combine_ln_kernel,
        grid=(n // tm,),
        in_specs=[pl.BlockSpec((tm, routed_parts.shape[1]), row), pl.BlockSpec((tm, d), row), pl.BlockSpec((tm, d), row),
                  pl.BlockSpec((1, d), lambda i: (0, 0)), pl.BlockSpec((1, d), lambda i: (0, 0))],
        out_specs=pl.BlockSpec((tm, d), row),
        out_shape=jax.ShapeDtypeStruct((n, d), F32),
        compiler_params=pltpu.CompilerParams(dimension_semantics=("arbitrary",),
                                             vmem_limit_bytes=V7X_VMEM_LIMIT),
        name="combine_layer_norm",
    )(routed_parts, shared, h, g.reshape(1, d), b.reshape(1, d))


def kernel(x_prompt, x_sample, cache_k, cache_v, cache_idx_k, page_table, state_delta, state_conv, w_in, w_conv, a_log, dt_bias, delta_norm_g, w_branch_attn, w_branch_delta, w_out, ln1_g, ln1_b, w_router, router_bias, w_exp_gate, w_exp_up, w_exp_down, w_sh_gate, w_sh_up, w_sh_down, ln2_g, ln2_b):
    b, s, d = x_prompt.shape
    db, t = x_sample.shape[:2]
    assert DEPTH == 1 and b == 1
    past = page_table.shape[1] * cache_k.shape[2]
    n_p, n_s = b * s, db * t
    (l_in, l_conv, l_alog, l_dtb, l_ng, l_ba, l_bd, l_out, l_g1, l_b1,
     l_r, l_rb, l_eg, l_eu, l_ed, l_sg, l_su, l_sd, l_g2, l_b2) = [a[0] for a in (
         w_in, w_conv, a_log, dt_bias, delta_norm_g, w_branch_attn, w_branch_delta, w_out,
         ln1_g, ln1_b, w_router, router_bias, w_exp_gate, w_exp_up, w_exp_down,
         w_sh_gate, w_sh_up, w_sh_down, ln2_g, ln2_b)]

    x_all = jnp.concatenate([x_prompt.reshape(n_p, d), x_sample.reshape(n_s, d)], axis=0)
    xb = x_all.astype(BF16)
    w_in_t = jnp.swapaxes(l_in, 0, 1)
    p_attn = matmul_rows(xb, w_in_t, 0, ATTN_COLS)
    p_delta = matmul_rows(xb, w_in_t, IN_SPLITS[5], DELTA_COLS)
    p_gate = matmul_rows(xb, w_in_t, IN_SPLITS[8], GATE_COLS)
    ap = attention_inputs(p_attn[:n_p].reshape(b, s, -1), jnp.arange(s))
    at = attention_inputs(p_attn[n_p:].reshape(db, t, -1), past + jnp.arange(t))

    a_p = prompt_sparse_attention(ap.q, ap.k, ap.v, ap.iq, ap.iw, ap.ik)
    a_s = sample_sparse_attention(at.q, at.k, at.v, at.iq, at.iw, at.ik,
                                  cache_k[0], cache_v[0], cache_idx_k[0], page_table)
    d_p, sd_p, sc_p = prompt_delta_branch(p_delta, p_gate, n_p, l_conv, l_alog, l_dtb, l_ng)
    pd_s = p_delta[n_p:].reshape(db, t, -1)
    nh = N_DELTA_HEADS
    d_s, sd_s, sc_s = gated_delta_branch(pd_s[..., :CONV_CH], pd_s[..., CONV_CH:CONV_CH + nh], pd_s[..., CONV_CH + nh:],
                                         p_gate[n_p:, :DELTA_V_WIDTH].reshape(db, t, -1), state_conv[0], state_delta[0],
                                         l_conv, l_alog, l_dtb, l_ng)

    a_all = jnp.concatenate([a_p.reshape(n_p, -1).astype(BF16), a_s.reshape(n_s, -1).astype(BF16)], axis=0)
    d_all = jnp.concatenate([d_p, d_s.reshape(n_s, -1).astype(BF16)], axis=0)
    merged = merge_branches(a_all, d_all, p_gate, DELTA_V_WIDTH, l_ba, l_bd)
    h = layer_norm(ALPHA * x_all + matmul(merged, l_out), l_g1, l_b1)
    routed_parts, shared = moe(h, l_r, l_rb, l_eg, l_eu, l_ed, l_sg, l_su, l_sd)
    y = combine_layer_norm(routed_parts, shared, h, l_g2, l_b2)
    return (y[:n_p].reshape(b, s, d), y[n_p:].reshape(db, t, d),
            ap.k[None], ap.v[None], ap.ik[None], sd_p[None, None], sc_p[None, None],
            at.k[None], at.v[None], at.ik[None], sd_s[None], sc_s[None])
```

```python
import collections
import functools

import jax
import jax.numpy as jnp
import numpy as np
from jax import lax
from jax.experimental import pallas as pl
from jax.experimental.pallas import tpu as pltpu

D_MODEL = 4096
DEPTH = 1
N_HEADS = 16
N_KV_HEADS = 4
HEAD_DIM = 128
ROPE_THETA = 500000.0
N_IDX_HEADS = 16
IDX_DIM = 128
IDX_TOPK = 256
Q_BLOCK = 128
N_DELTA_HEADS = 16
DELTA_DK = 128
DELTA_DV = 128
CONV_WIDTH = 4
DELTA_CHUNK = 64
N_EXPERTS = 64
TOP_K = 8
N_GROUPS = 8
TOPK_GROUPS = 4
EXPERT_DIM = 1024
ROUTED_SCALE = 2.5
EXPERT_BLOCK = 512
ROW_GROUPS = 2
ALPHA = (2 * DEPTH) ** 0.25
LN_EPS = 1e-5
ATTN_WIDTH = N_HEADS * HEAD_DIM
KV_WIDTH = N_KV_HEADS * HEAD_DIM
IDXQ_WIDTH = N_IDX_HEADS * IDX_DIM
DELTA_QK_WIDTH = N_DELTA_HEADS * DELTA_DK
DELTA_V_WIDTH = N_DELTA_HEADS * DELTA_DV
CONV_CH = 2 * DELTA_QK_WIDTH + DELTA_V_WIDTH
IN_SIZES = (ATTN_WIDTH, KV_WIDTH, KV_WIDTH, IDXQ_WIDTH, IDX_DIM, N_IDX_HEADS, CONV_CH,
            N_DELTA_HEADS, N_DELTA_HEADS, DELTA_V_WIDTH, 2 * D_MODEL)
IN_SPLITS = tuple(int(s) for s in np.cumsum(IN_SIZES)[:-1])

V7X_VMEM_LIMIT = 56 * 1024 * 1024
BF16 = jnp.bfloat16
F32 = jnp.float32
SUBLANES = 8


def _mm_kernel(x_ref, w_ref, o_ref, wbf_ref):
    @pl.when(pl.program_id(1) == 0)
    def _():
        wbf_ref[...] = w_ref[...].astype(BF16)

    o_ref[...] = jnp.dot(x_ref[...], wbf_ref[...], preferred_element_type=F32).astype(o_ref.dtype)


def _row_tile(m):
    for tm in (1024, 1088, 512, 256, 128):
        if m % tm == 0:
            return tm
    raise ValueError(f"no row tile for {m} rows")


def matmul(x, w, out_dtype=F32, tn=512):
    m, k = x.shape
    n = w.shape[1]
    tm = _row_tile(m)
    tn = min(tn, n)
    return pl.pallas_call(
        _mm_kernel,
        grid=(pl.cdiv(n, tn), m // tm),
        in_specs=[pl.BlockSpec((tm, k), lambda j, i: (i, 0)),
                  pl.BlockSpec((k, tn), lambda j, i: (0, j))],
        out_specs=pl.BlockSpec((tm, tn), lambda j, i: (i, j)),
        out_shape=jax.ShapeDtypeStruct((m, n), out_dtype),
        scratch_shapes=[pltpu.VMEM((k, tn), BF16)],
        compiler_params=pltpu.CompilerParams(dimension_semantics=("arbitrary", "arbitrary"),
                                             vmem_limit_bytes=V7X_VMEM_LIMIT),
        name="dense_matmul",
    )(x, w)


def _swiglu_kernel(x_ref, wg_ref, wu_ref, o_ref, wg_bf, wu_bf):
    @pl.when(pl.program_id(1) == 0)
    def _():
        wg_bf[...] = wg_ref[...].astype(BF16)
        wu_bf[...] = wu_ref[...].astype(BF16)

    x = x_ref[...]
    g = jnp.dot(x, wg_bf[...], preferred_element_type=F32)
    u = jnp.dot(x, wu_bf[...], preferred_element_type=F32)
    o_ref[...] = (g * jax.nn.sigmoid(g) * u).astype(o_ref.dtype)


def swiglu_up(x, w_gate, w_up, tn=256):
    m, k = x.shape
    f = w_gate.shape[1]
    tm = _row_tile(m)
    tn = min(tn, f)
    assert f % tn == 0 and w_up.shape == w_gate.shape
    return pl.pallas_call(
        _swiglu_kernel,
        grid=(f // tn, m // tm),
        in_specs=[pl.BlockSpec((tm, k), lambda j, i: (i, 0)),
                  pl.BlockSpec((k, tn), lambda j, i: (0, j)),
                  pl.BlockSpec((k, tn), lambda j, i: (0, j))],
        out_specs=pl.BlockSpec((tm, tn), lambda j, i: (i, j)),
        out_shape=jax.ShapeDtypeStruct((m, f), BF16),
        scratch_shapes=[pltpu.VMEM((k, tn), BF16), pltpu.VMEM((k, tn), BF16)],
        compiler_params=pltpu.CompilerParams(dimension_semantics=("arbitrary", "arbitrary"),
                                             vmem_limit_bytes=V7X_VMEM_LIMIT),
        name="swiglu_up",
    )(x, w_gate, w_up)


def _merge_kernel(a_ref, d_ref, ga_ref, gd_ref, wa_ref, wd_ref, o_ref, wa_bf, wd_bf):
    @pl.when(pl.program_id(1) == 0)
    def _():
        wa_bf[...] = wa_ref[...].astype(BF16)
        wd_bf[...] = wd_ref[...].astype(BF16)

    pa = jnp.dot(a_ref[...], wa_bf[...], preferred_element_type=F32)
    pd = jnp.dot(d_ref[...], wd_bf[...], preferred_element_type=F32)
    o_ref[...] = (jax.nn.sigmoid(ga_ref[...]) * pa + jax.nn.sigmoid(gd_ref[...]) * pd).astype(o_ref.dtype)


def merge_branches(a, d, p_gate, gate_col0, w_a, w_d, tn=512):
    m, k = a.shape
    n = w_a.shape[1]
    tm = _row_tile(m)
    assert gate_col0 % tn == 0 and n % tn == 0 and w_d.shape == w_a.shape and d.shape == a.shape
    ga0, gd0 = gate_col0 // tn, (gate_col0 + n) // tn
    return pl.pallas_call(
        _merge_kernel,
        grid=(n // tn, m // tm),
        in_specs=[pl.BlockSpec((tm, k), lambda j, i: (i, 0)),
                  pl.BlockSpec((tm, k), lambda j, i: (i, 0)),
                  pl.BlockSpec((tm, tn), lambda j, i: (i, ga0 + j)),
                  pl.BlockSpec((tm, tn), lambda j, i: (i, gd0 + j)),
                  pl.BlockSpec((k, tn), lambda j, i: (0, j)),
                  pl.BlockSpec((k, tn), lambda j, i: (0, j))],
        out_specs=pl.BlockSpec((tm, tn), lambda j, i: (i, j)),
        out_shape=jax.ShapeDtypeStruct((m, n), BF16),
        scratch_shapes=[pltpu.VMEM((k, tn), BF16), pltpu.VMEM((k, tn), BF16)],
        compiler_params=pltpu.CompilerParams(dimension_semantics=("arbitrary", "arbitrary"),
                                             vmem_limit_bytes=V7X_VMEM_LIMIT),
        name="merge_branches",
    )(a, d, p_gate, p_gate, w_a, w_d)


def _mm_rows_kernel(x_ref, wt_ref, o_ref, wbf_ref):
    @pl.when(pl.program_id(1) == 0)
    def _():
        wbf_ref[...] = wt_ref[...].astype(BF16)

    o_ref[...] = lax.dot_general(x_ref[...], wbf_ref[...], (((1,), (1,)), ((), ())),
                                 preferred_element_type=F32).astype(o_ref.dtype)


def matmul_rows(x, wt, row0, nrows, out_dtype=F32, tn=512):
    m, k = x.shape
    tm = _row_tile(m)
    tn = min(tn, nrows)
    assert row0 % SUBLANES == 0 and wt.shape[1] == k
    w_spec = pl.BlockSpec((pl.Element(tn, (0, tn)), pl.Element(k)),
                          lambda j, i: (pl.multiple_of(row0 + j * tn, SUBLANES), 0))
    return pl.pallas_call(
        _mm_rows_kernel,
        grid=(pl.cdiv(nrows, tn), m // tm),
        in_specs=[pl.BlockSpec((tm, k), lambda j, i: (i, 0)), w_spec],
        out_specs=pl.BlockSpec((tm, tn), lambda j, i: (i, j)),
        out_shape=jax.ShapeDtypeStruct((m, nrows), out_dtype),
        scratch_shapes=[pltpu.VMEM((tn, k), BF16)],
        compiler_params=pltpu.CompilerParams(dimension_semantics=("arbitrary", "arbitrary"),
                                             vmem_limit_bytes=V7X_VMEM_LIMIT),
        name="dense_matmul_rows",
    )(x, wt)


def _expert_changed(be_ref, i):
    prev = be_ref[jnp.maximum(i - 1, 0)]
    return jnp.logical_or(i == 0, be_ref[i] != prev)


def _moe_up_kernel(be_ref, nu_ref, x_ref, wg_ref, wu_ref, *rest):
    h_ref, wg_bf, wu_bf = rest[-3:]
    i = pl.program_id(1)

    @pl.when(i < nu_ref[0])
    def _():
        @pl.when(_expert_changed(be_ref, i))
        def _():
            wg_bf[...] = wg_ref[...].astype(BF16)
            wu_bf[...] = wu_ref[...].astype(BF16)

        x = x_ref[...]
        g = jnp.dot(x, wg_bf[...], preferred_element_type=F32)
        u = jnp.dot(x, wu_bf[...], preferred_element_type=F32)
        h_ref[...] = (g * jax.nn.sigmoid(g) * u).astype(h_ref.dtype)


def _moe_down_kernel(be_ref, nu_ref, h_ref, gate_ref, wd_ref, y_ref, wd_bf):
    i = pl.program_id(1)

    @pl.when(i < nu_ref[0])
    def _():
        @pl.when(_expert_changed(be_ref, i))
        def _():
            wd_bf[...] = wd_ref[...].astype(BF16)

        y = jnp.dot(h_ref[...], wd_bf[...], preferred_element_type=F32)
        y_ref[...] = (y * gate_ref[...]).astype(y_ref.dtype)


def grouped_swiglu_up(x_rows, block_expert, n_used, w_gate, w_up, blk, block0, total_blocks, h_prev=None, tf=512):
    rows, d = x_rows.shape
    f = w_gate.shape[2]
    tf = min(tf, f)
    n_blocks = rows // blk
    in_specs = [pl.BlockSpec((blk, d), lambda j, i, be, nu: (i, 0)),
                pl.BlockSpec((None, d, tf), lambda j, i, be, nu: (be[i], 0, j)),
                pl.BlockSpec((None, d, tf), lambda j, i, be, nu: (be[i], 0, j))]
    operands = [block_expert, n_used, x_rows, w_gate, w_up]
    aliases = {}
    if h_prev is not None:
        in_specs.append(pl.BlockSpec(memory_space=pl.ANY))
        operands.append(h_prev)
        aliases = {len(operands) - 1: 0}
    grid_spec = pltpu.PrefetchScalarGridSpec(
        num_scalar_prefetch=2,
        grid=(f // tf, n_blocks),
        in_specs=in_specs,
        out_specs=pl.BlockSpec((blk, tf), lambda j, i, be, nu: (block0 + i, j)),
        scratch_shapes=[pltpu.VMEM((d, tf), BF16), pltpu.VMEM((d, tf), BF16)],
    )
    return pl.pallas_call(
        _moe_up_kernel, grid_spec=grid_spec,
        out_shape=jax.ShapeDtypeStruct((total_blocks * blk, f), BF16),
        input_output_aliases=aliases,
        compiler_params=pltpu.CompilerParams(dimension_semantics=("arbitrary", "arbitrary"),
                                             vmem_limit_bytes=V7X_VMEM_LIMIT),
        name="moe_up",
    )(*operands)


def grouped_down(h_rows, row_gate, block_expert, n_used, w_down, blk, tn=2048):
    rows, f = h_rows.shape
    d = w_down.shape[2]
    tn = min(tn, d)
    n_blocks = rows // blk
    grid_spec = pltpu.PrefetchScalarGridSpec(
        num_scalar_prefetch=2,
        grid=(d // tn, n_blocks),
        in_specs=[pl.BlockSpec((blk, f), lambda j, i, be, nu: (i, 0)),
                  pl.BlockSpec((blk, 1), lambda j, i, be, nu: (i, 0)),
                  pl.BlockSpec((None, f, tn), lambda j, i, be, nu: (be[i], 0, j))],
        out_specs=pl.BlockSpec((blk, tn), lambda j, i, be, nu: (i, j)),
        scratch_shapes=[pltpu.VMEM((f, tn), BF16)],
    )
    return pl.pallas_call(
        _moe_down_kernel, grid_spec=grid_spec,
        out_shape=jax.ShapeDtypeStruct((rows, d), BF16),
        compiler_params=pltpu.CompilerParams(dimension_semantics=("arbitrary", "arbitrary"),
                                             vmem_limit_bytes=V7X_VMEM_LIMIT),
        name="moe_down",
    )(block_expert, n_used, h_rows, row_gate, w_down)


def layer_norm(x, g, b):
    xc = x - jnp.mean(x, -1, keepdims=True)
    var = jnp.mean(xc * xc, -1, keepdims=True)
    return xc * lax.rsqrt(var + LN_EPS) * g + b


def l2norm(x):
    return x * lax.rsqrt(jnp.sum(x * x, -1, keepdims=True) + 1e-6)


def partial_rope(x, pos):
    rot = x.shape[-1] // 4
    half = rot // 2
    inv_freq = ROPE_THETA ** (-jnp.arange(half, dtype=F32) / half)
    ang = pos.astype(F32)[:, None] * inv_freq[None, :]
    cos = jnp.cos(ang)[:, None, :]
    sin = jnp.sin(ang)[:, None, :]
    x1, x2, rest = x[..., :half], x[..., half:rot], x[..., rot:]
    return jnp.concatenate([x1 * cos - x2 * sin, x2 * cos + x1 * sin, rest], axis=-1)


ATTN_COLS = IN_SPLITS[5]
DELTA_COLS = IN_SPLITS[8] - IN_SPLITS[5]
GATE_COLS = int(sum(IN_SIZES)) - IN_SPLITS[8]
AttnInputs = collections.namedtuple("AttnInputs", "q k v iq ik iw")


def attention_inputs(p, pos):
    b, t, _ = p.shape
    q, k, v, iq, ik, iw = jnp.split(p, IN_SPLITS[:5], axis=-1)
    q = partial_rope(q.reshape(b, t, N_HEADS, HEAD_DIM), pos)
    k = partial_rope(k.reshape(b, t, N_KV_HEADS, HEAD_DIM), pos)
    v = v.reshape(b, t, N_KV_HEADS, HEAD_DIM)
    iq = partial_rope(iq.reshape(b, t, N_IDX_HEADS, IDX_DIM), pos)
    ik = partial_rope(ik[:, :, None, :], pos)[:, :, 0, :]
    iw = iw * (N_IDX_HEADS ** -0.5 * IDX_DIM ** -0.5)
    return AttnInputs(q, k, v, iq, ik, iw)


KEY_CHUNK = 512
IDX_HEAD_GROUP = 4
MASKED = -1e30
SOFTMAX_LOG2_SCALE = HEAD_DIM ** -0.5 * float(np.log2(np.e))
INT32_MIN = -2 ** 31
INT32_MAX = 2 ** 31 - 1
NEG_INF_KEY = int(np.int32(np.uint32(0xFF800000) ^ np.uint32(0x7FFFFFFF)))


def _sort_key(x):
    bits = lax.bitcast_convert_type(x, jnp.int32)
    return bits ^ (jnp.right_shift(bits, 31) & jnp.int32(INT32_MAX))


def _lane_tile(x, n):
    return x if n == 1 else jnp.concatenate([x] * n, axis=1)


def _count_rows(keys_ref, nkc, preds):
    _, rows, kc = keys_ref.shape

    def body(c, parts):
        keys = keys_ref[c]
        out = []
        for pred, part in zip(preds, parts):
            hit = pred(keys, c)
            for j in range(kc // 128):
                part = part + hit[:, j * 128:(j + 1) * 128]
            out.append(part)
        return tuple(out)

    parts = lax.fori_loop(0, nkc, body, tuple(jnp.zeros((rows, 128), F32) for _ in preds))
    return [jnp.broadcast_to(jnp.sum(part, axis=1, keepdims=True), (rows, 128)) for part in parts]


def _topk_selection(keys_ref, nkc, n_sel, radix_bits):
    _, rows, kc = keys_ref.shape
    reps = kc // 128
    col = lax.broadcasted_iota(jnp.int32, (rows, kc), 1)

    def thr_digit(it, carry):
        thr, cnt_thr = carry
        shift = 32 - radix_bits * (it + 1)
        cands = [thr + jnp.left_shift(jnp.int32(j), shift) for j in range(1, 2 ** radix_bits)]
        cands_w = [_lane_tile(cand, reps) for cand in cands]
        cnts = _count_rows(keys_ref, nkc, [lambda keys, c, cw=cw: jnp.where(keys >= cw, 1.0, 0.0) for cw in cands_w])
        for cand, cnt in zip(cands, cnts):
            ok = cnt >= n_sel
            thr = jnp.where(ok, cand, thr)
            cnt_thr = jnp.where(ok, cnt, cnt_thr)
        return thr, cnt_thr

    thr0 = jnp.full((rows, 128), INT32_MIN, jnp.int32)
    cnt0 = jnp.broadcast_to(jnp.asarray(nkc * kc).astype(F32), (rows, 128))
    thr, cnt_thr = lax.fori_loop(0, 32 // radix_bits, thr_digit, (thr0, cnt0))

    short = thr == NEG_INF_KEY
    thr_w = _lane_tile(thr, reps)
    cnt_gt, = _count_rows(keys_ref, nkc, [lambda keys, c: jnp.where(keys > thr_w, 1.0, 0.0)])
    need = n_sel - cnt_gt
    tied = jnp.logical_and(cnt_thr > n_sel, jnp.logical_not(short))

    def tie_limit():
        def idx_bit(it, x):
            cand = x + jnp.left_shift(jnp.int32(1), 30 - it)
            cand_w = _lane_tile(cand, reps)
            cnt, = _count_rows(keys_ref, nkc, [lambda keys, c: jnp.where(
                keys == thr_w, jnp.where(c * kc + col < cand_w, 1.0, 0.0), 0.0)])
            return jnp.where(cnt < need, cand, x)
        return lax.fori_loop(0, 31, idx_bit, jnp.zeros((rows, 128), jnp.int32))

    any_tied = jnp.max(jnp.where(tied, 1.0, 0.0)) > 0.0
    lim = lax.cond(any_tied, tie_limit, lambda: jnp.full((rows, 128), INT32_MAX, jnp.int32))
    lim = jnp.where(short, -1, jnp.where(tied, lim, INT32_MAX))
    return thr, lim


def _selected(keys, index, thr_w, lim_w, yes, no):
    keep_tie = jnp.where(index <= lim_w, yes, no)
    return jnp.where(keys > thr_w, yes, jnp.where(keys == thr_w, keep_tie, no))


def _dsa_prompt_kernel(iq_ref, iw_ref, ikt_ref, q_ref, kt_ref, v_ref, o_ref,
                       keys_ref, m_ref, l_ref, acc_ref, *, n_sel):
    qb, kc = Q_BLOCK, KEY_CHUNK
    reps = kc // 128
    grp = N_HEADS // N_KV_HEADS
    i = pl.program_id(0)
    nkc = (i * qb + qb + kc - 1) // kc
    qpos = i * qb + lax.broadcasted_iota(jnp.int32, (qb, kc), 0)
    col = lax.broadcasted_iota(jnp.int32, (qb, kc), 1)

    def score_chunk(c, carry):
        ikc = ikt_ref[c]
        acc = jnp.zeros((qb, kc), F32)
        for h0 in range(0, N_IDX_HEADS, IDX_HEAD_GROUP):
            iq_rows = iq_ref[h0:h0 + IDX_HEAD_GROUP].reshape(IDX_HEAD_GROUP * qb, IDX_DIM)
            s = jnp.dot(iq_rows, ikc, preferred_element_type=F32)
            for j in range(IDX_HEAD_GROUP):
                acc = acc + jnp.maximum(s[j * qb:(j + 1) * qb], 0.0) * iw_ref[:, h0 + j:h0 + j + 1]
        acc = jnp.where(c * kc + col <= qpos, acc, -jnp.inf)
        keys_ref[c] = _sort_key(acc)
        return carry

    lax.fori_loop(0, nkc, score_chunk, 0)

    thr, lim = _topk_selection(keys_ref, nkc, n_sel, radix_bits=1)
    thr_w, lim_w = _lane_tile(thr, reps), _lane_tile(lim, reps)

    m_ref[...] = jnp.full(m_ref.shape, MASKED, F32)
    l_ref[...] = jnp.zeros(l_ref.shape, F32)
    acc_ref[...] = jnp.zeros(acc_ref.shape, F32)
    rows = grp * qb

    def attend_chunk(c, carry):
        bias = _selected(keys_ref[c], c * kc + col, thr_w, lim_w, 0.0, MASKED)
        for n in range(N_KV_HEADS):
            r = pl.ds(n * rows, rows)
            qn = q_ref[n * grp:(n + 1) * grp].reshape(rows, HEAD_DIM)
            s = jnp.dot(qn, kt_ref[n, c], preferred_element_type=F32) * SOFTMAX_LOG2_SCALE
            s = (s.reshape(grp, qb, kc) + bias[None]).reshape(rows, kc)
            m_prev = m_ref[r, :]
            m_new = jnp.maximum(m_prev, jnp.max(s, axis=1, keepdims=True))
            alpha = jnp.exp2(m_prev - m_new)
            p = jnp.exp2(s - _lane_tile(m_new, reps))
            l_ref[r, :] = alpha * l_ref[r, :] + jnp.sum(p, axis=1, keepdims=True)
            acc_ref[r, :] = alpha * acc_ref[r, :] + jnp.dot(p.astype(BF16), v_ref[n, c],
                                                           preferred_element_type=F32)
            m_ref[r, :] = m_new
        return carry

    lax.fori_loop(0, nkc, attend_chunk, 0)
    for h in range(N_HEADS):
        r = pl.ds(h * qb, qb)
        o_ref[:, h * HEAD_DIM:(h + 1) * HEAD_DIM] = (acc_ref[r, :] / l_ref[r, :]).astype(o_ref.dtype)


def prompt_sparse_attention(q, k, v, iq, iw, ik):
    b, s = q.shape[:2]
    assert b == 1 and s % KEY_CHUNK == 0 and KEY_CHUNK % Q_BLOCK == 0 and KEY_CHUNK >= IDX_TOPK
    n_sel = min(IDX_TOPK, s // 4)
    nb, nc, kc = s // Q_BLOCK, s // KEY_CHUNK, KEY_CHUNK

    def head_major(a):
        return a.astype(BF16).reshape(nb, Q_BLOCK, a.shape[2], a.shape[3]).transpose(0, 2, 1, 3)

    ikt = ik.astype(BF16).reshape(nc, kc, IDX_DIM).transpose(0, 2, 1)
    kt = k.astype(BF16).reshape(nc, kc, N_KV_HEADS, HEAD_DIM).transpose(2, 0, 3, 1)
    vc = v.astype(BF16).reshape(nc, kc, N_KV_HEADS, HEAD_DIM).transpose(2, 0, 1, 3)
    resident = dict(pipeline_mode=pl.Buffered(1))
    out = pl.pallas_call(
        functools.partial(_dsa_prompt_kernel, n_sel=n_sel),
        grid=(nb,),
        in_specs=[pl.BlockSpec((None, N_IDX_HEADS, Q_BLOCK, IDX_DIM), lambda i: (i, 0, 0, 0)),
                  pl.BlockSpec((Q_BLOCK, N_IDX_HEADS), lambda i: (i, 0)),
                  pl.BlockSpec((nc, IDX_DIM, kc), lambda i: (0, 0, 0), **resident),
                  pl.BlockSpec((None, N_HEADS, Q_BLOCK, HEAD_DIM), lambda i: (i, 0, 0, 0)),
                  pl.BlockSpec((N_KV_HEADS, nc, HEAD_DIM, kc), lambda i: (0, 0, 0, 0), **resident),
                  pl.BlockSpec((N_KV_HEADS, nc, kc, HEAD_DIM), lambda i: (0, 0, 0, 0), **resident)],
        out_specs=pl.BlockSpec((Q_BLOCK, N_HEADS * HEAD_DIM), lambda i: (i, 0)),
        out_shape=jax.ShapeDtypeStruct((s, N_HEADS * HEAD_DIM), BF16),
        scratch_shapes=[pltpu.VMEM((nc, Q_BLOCK, kc), jnp.int32),
                        pltpu.VMEM((N_HEADS * Q_BLOCK, 128), F32),
                        pltpu.VMEM((N_HEADS * Q_BLOCK, 128), F32),
                        pltpu.VMEM((N_HEADS * Q_BLOCK, HEAD_DIM), F32)],
        compiler_params=pltpu.CompilerParams(dimension_semantics=("arbitrary",),
                                             vmem_limit_bytes=V7X_VMEM_LIMIT),
        name="dsa_prompt",
    )(head_major(iq), iw.reshape(s, N_IDX_HEADS), ikt, head_major(q), kt, vc)
    return out.reshape(b, s, N_HEADS * HEAD_DIM)


T_PAD = 8


def _dsa_sample_kernel(pt_ref, iq_ref, iw_ref, q_ref, iknew_ref, knew_ref, vnew_ref, *rest, n_sel, n_pages):
    del pt_ref
    ik_pages, k_pages, v_pages = rest[:n_pages], rest[n_pages:2 * n_pages], rest[2 * n_pages:3 * n_pages]
    o_ref, keys_ref, s_ref = rest[3 * n_pages:]
    page = ik_pages[0].shape[0]
    kv_rows = k_pages[0].shape[0]
    rows = N_HEADS * T_PAD
    nkc = n_pages + 1
    nt = (((1,), (1,)), ((), ()))
    row_t = lax.broadcasted_iota(jnp.int32, (T_PAD, page), 0)
    col = lax.broadcasted_iota(jnp.int32, (T_PAD, page), 1)

    for c in range(nkc):
        ikc = ik_pages[c][...].astype(BF16) if c < n_pages else iknew_ref[...]
        s = lax.dot_general(iq_ref[...], ikc, nt, preferred_element_type=F32)
        x = jnp.maximum(s, 0.0) * iw_ref[...]
        sc = jnp.sum(x.reshape(T_PAD, N_IDX_HEADS, page), axis=1)
        if c == n_pages:
            sc = jnp.where(col <= row_t, sc, -jnp.inf)
        keys_ref[c] = _sort_key(sc)

    thr, lim = _topk_selection(keys_ref, nkc, n_sel, radix_bits=4)

    scale = SOFTMAX_LOG2_SCALE
    spread = (lax.broadcasted_iota(jnp.int32, (page, kv_rows), 1) // N_KV_HEADS
              == lax.broadcasted_iota(jnp.int32, (page, kv_rows), 0)).astype(BF16)
    own_head = (lax.broadcasted_iota(jnp.int32, (rows, kv_rows), 0) // (rows // N_KV_HEADS)
                == lax.broadcasted_iota(jnp.int32, (rows, kv_rows), 1) % N_KV_HEADS).astype(F32)
    for c in range(nkc):
        kc_ = k_pages[c][...].astype(BF16) if c < n_pages else knew_ref[...]
        s = lax.dot_general(q_ref[...], kc_, nt, preferred_element_type=F32) * scale
        sel = _selected(keys_ref[c], c * page + col, thr, lim, 1.0, 0.0).astype(BF16)
        sel = jnp.dot(sel, spread, preferred_element_type=F32)
        keep = jnp.broadcast_to(sel[None], (N_HEADS, T_PAD, kv_rows)).reshape(rows, kv_rows) * own_head
        s_ref[:, c * kv_rows:(c + 1) * kv_rows] = jnp.where(keep > 0.5, s, MASKED)
    s = s_ref[...]
    p = jnp.exp2(s - jnp.max(s, axis=1, keepdims=True))
    inv_l = 1.0 / jnp.sum(p, axis=1, keepdims=True)
    p = p.astype(BF16)
    o = jnp.zeros((rows, HEAD_DIM), F32)
    for c in range(nkc):
        vc_ = v_pages[c][...].astype(BF16) if c < n_pages else vnew_ref[...]
        o = o + jnp.dot(p[:, c * kv_rows:(c + 1) * kv_rows], vc_, preferred_element_type=F32)
    o_ref[...] = o * inv_l


def sample_sparse_attention(q, k_new, v_new, iq, iw, ik_new, cache_k, cache_v, cache_idx_k, page_table):
    db, t = q.shape[:2]
    n_pool, page = cache_k.shape[:2]
    n_pages = page_table.shape[1]
    n_keys = n_pages * page + t
    n_sel = min(IDX_TOPK, n_keys // 4)
    grp = N_HEADS // N_KV_HEADS
    assert t <= T_PAD <= page and page == 128 and HEAD_DIM == 128 and IDX_DIM == 128

    def pad_t(a, to):
        return jnp.pad(a, [(0, 0), (0, to - a.shape[1])] + [(0, 0)] * (a.ndim - 2))

    kv_rows = page * N_KV_HEADS
    rows = N_HEADS * T_PAD
    iq_rows = pad_t(iq, T_PAD).astype(BF16).reshape(db, T_PAD * N_IDX_HEADS, IDX_DIM)
    iw_rows = pad_t(iw, T_PAD).reshape(db, T_PAD * N_IDX_HEADS, 1)
    q_rows = pad_t(q, T_PAD).astype(BF16).transpose(0, 2, 1, 3).reshape(db, rows, HEAD_DIM)
    iknew = pad_t(ik_new, page).astype(BF16)
    knew = pad_t(k_new, page).astype(BF16).reshape(db, kv_rows, HEAD_DIM)
    vnew = pad_t(v_new, page).astype(BF16).reshape(db, kv_rows, HEAD_DIM)
    ck = cache_k.reshape(n_pool, kv_rows, HEAD_DIM)
    cv = cache_v.reshape(n_pool, kv_rows, HEAD_DIM)

    def per_batch(*blk):
        return pl.BlockSpec((None,) + blk, lambda b, pt: (b,) + (0,) * len(blk))

    def paged(nrows):
        return [pl.BlockSpec((None, nrows, HEAD_DIM), lambda b, pt, j=j: (pt[b, j], 0, 0)) for j in range(n_pages)]

    grid_spec = pltpu.PrefetchScalarGridSpec(
        num_scalar_prefetch=1,
        grid=(db,),
        in_specs=[per_batch(T_PAD * N_IDX_HEADS, IDX_DIM), per_batch(T_PAD * N_IDX_HEADS, 1),
                  per_batch(rows, HEAD_DIM), per_batch(page, IDX_DIM), per_batch(kv_rows, HEAD_DIM),
                  per_batch(kv_rows, HEAD_DIM)]
        + paged(page) + paged(kv_rows) + paged(kv_rows),
        out_specs=per_batch(rows, HEAD_DIM),
        scratch_shapes=[pltpu.VMEM((n_pages + 1, T_PAD, page), jnp.int32),
                        pltpu.VMEM((rows, (n_pages + 1) * kv_rows), F32)],
    )
    o = pl.pallas_call(
        functools.partial(_dsa_sample_kernel, n_sel=n_sel, n_pages=n_pages),
        grid_spec=grid_spec,
        out_shape=jax.ShapeDtypeStruct((db, rows, HEAD_DIM), F32),
        compiler_params=pltpu.CompilerParams(dimension_semantics=("arbitrary",),
                                             vmem_limit_bytes=V7X_VMEM_LIMIT),
        name="dsa_sample",
    )(page_table, iq_rows, iw_rows, q_rows, iknew, knew, vnew,
      *([cache_idx_k] * n_pages), *([ck] * n_pages), *([cv] * n_pages))
    o = o.reshape(db, N_KV_HEADS, grp, T_PAD, HEAD_DIM)[:, :, :, :t]
    return o.transpose(0, 3, 1, 2, 4).reshape(db, t, N_HEADS * HEAD_DIM)


def causal_conv(xpad, w_conv):
    c = xpad.shape[-1]
    return lax.conv_general_dilated(xpad, w_conv[:, None, :], window_strides=(1,),
                                    padding='VALID', dimension_numbers=('NWC', 'WIO', 'NWC'),
                                    feature_group_count=c)


def chunk_gated_delta(q, k, v, beta, g, s0):
    b, t, h, dk = q.shape
    dv = v.shape[-1]
    c = min(DELTA_CHUNK, t)
    n = -(-t // c)
    pad = n * c - t

    def chunks(a):
        a = jnp.pad(a, [(0, 0), (0, pad)] + [(0, 0)] * (a.ndim - 2))
        a = a.reshape((b, n, c) + a.shape[2:])
        return jnp.moveaxis(a, (1, 3), (0, 2))

    qc, kc, vc, bc, gc = [chunks(a) for a in (q, k, v, beta, g)]
    gc = jnp.cumsum(gc, axis=-1)
    tri = jnp.tril(jnp.ones((c, c), bool))
    strict = jnp.tril(jnp.ones((c, c), bool), -1)
    decay = jnp.exp(jnp.where(tri, gc[..., :, None] - gc[..., None, :], -jnp.inf))
    kb = kc * bc[..., None]
    a_low = jnp.where(strict, jnp.einsum('nbhid,nbhjd->nbhij', kb, kc) * decay, 0.0)
    rhs = jnp.concatenate([vc * bc[..., None], kb * jnp.exp(gc)[..., None]], axis=-1)
    sol = lax.linalg.triangular_solve(a_low + jnp.eye(c, dtype=F32), rhs, left_side=True,
                                      lower=True, unit_diagonal=True)
    u, w = sol[..., :dv], sol[..., dv:]
    qk = jnp.where(tri, jnp.einsum('nbhid,nbhjd->nbhij', qc, kc) * decay, 0.0)

    def step(state, xs):
        q_i, k_i, u_i, w_i, g_i, qk_i = xs
        v_corr = u_i - jnp.einsum('bhck,bhkv->bhcv', w_i, state)
        o_i = (jnp.einsum('bhck,bhkv->bhcv', q_i * jnp.exp(g_i)[..., None], state)
               + jnp.einsum('bhij,bhjv->bhiv', qk_i, v_corr))
        g_last = g_i[..., -1:]
        state = (state * jnp.exp(g_last)[..., None]
                 + jnp.einsum('bhck,bhcv->bhkv', k_i * jnp.exp(g_last - g_i)[..., None], v_corr))
        return state, o_i

    s_final, o = lax.scan(step, s0, (qc, kc, u, w, gc, qk))
    o = jnp.moveaxis(o, (0, 2), (1, 3)).reshape(b, n * c, h, dv)[:, :t]
    return o, s_final


def gated_delta_branch(qkv, b_d, a_d, z, conv_buf, s0, w_conv, a_log, dt_bias, norm_g):
    b, t, _ = qkv.shape
    xpad = jnp.concatenate([conv_buf, qkv], axis=1)
    new_buf = xpad[:, -(CONV_WIDTH - 1):]
    hc = jax.nn.silu(causal_conv(xpad, w_conv))
    q, k, v = jnp.split(hc, (DELTA_QK_WIDTH, 2 * DELTA_QK_WIDTH), axis=-1)
    q = l2norm(q.reshape(b, t, N_DELTA_HEADS, DELTA_DK)) * DELTA_DK ** -0.5
    k = l2norm(k.reshape(b, t, N_DELTA_HEADS, DELTA_DK))
    v = v.reshape(b, t, N_DELTA_HEADS, DELTA_DV)
    beta = jax.nn.sigmoid(b_d)
    g = -jnp.exp(a_log) * jax.nn.softplus(a_d + dt_bias)
    o, s_new = chunk_gated_delta(q, k, v, beta, g, s0)
    o = o * lax.rsqrt(jnp.mean(o * o, -1, keepdims=True) + 1e-6) * norm_g
    o = o * jax.nn.silu(z.reshape(b, t, N_DELTA_HEADS, DELTA_DV))
    return o.reshape(b, t, DELTA_V_WIDTH), s_new, new_buf


DELTA_TOKEN_BLOCK = 256


def _delta_pre_kernel(x_ref, halo_ref, w_ref, q_ref, k_ref, v_ref, xe_ref):
    i = pl.program_id(0)
    tb = x_ref.shape[0]
    xe_ref[0:SUBLANES, :] = jnp.where(i == 0, 0.0, halo_ref[...])
    xe_ref[SUBLANES:SUBLANES + tb, :] = x_ref[...]
    outs = (q_ref, k_ref, v_ref)
    for col in range(3 * N_DELTA_HEADS):
        cs = slice(col * DELTA_DK, (col + 1) * DELTA_DK)
        y = sum(xe_ref[SUBLANES - (CONV_WIDTH - 1) + tap:SUBLANES - (CONV_WIDTH - 1) + tap + tb, cs] * w_ref[tap:tap + 1, cs]
                for tap in range(CONV_WIDTH))
        y = y * jax.nn.sigmoid(y)
        if col < 2 * N_DELTA_HEADS:
            y = y * lax.rsqrt(jnp.sum(y * y, axis=1, keepdims=True) + 1e-6)
        if col < N_DELTA_HEADS:
            y = y * DELTA_DK ** -0.5
        for cc in range(tb // DELTA_CHUNK):
            outs[col // N_DELTA_HEADS][cc, col % N_DELTA_HEADS] = y[cc * DELTA_CHUNK:(cc + 1) * DELTA_CHUNK]


def delta_pre(qkv, t, w_conv):
    c = w_conv.shape[1]
    tb = DELTA_TOKEN_BLOCK
    assert t % tb == 0 and tb % DELTA_CHUNK == 0 and DELTA_DK == DELTA_DV
    per_blk = tb // DELTA_CHUNK
    out_spec = pl.BlockSpec((per_blk, N_DELTA_HEADS, DELTA_CHUNK, DELTA_DK), lambda i: (i, 0, 0, 0))
    out_shape = jax.ShapeDtypeStruct((t // DELTA_CHUNK, N_DELTA_HEADS, DELTA_CHUNK, DELTA_DK), F32)
    return pl.pallas_call(
        _delta_pre_kernel,
        grid=(t // tb,),
        in_specs=[pl.BlockSpec((tb, c), lambda i: (i, 0)),
                  pl.BlockSpec((SUBLANES, c), lambda i: (jnp.maximum(i * (tb // SUBLANES) - 1, 0), 0)),
                  pl.BlockSpec((CONV_WIDTH, c), lambda i: (0, 0))],
        out_specs=[out_spec] * 3,
        out_shape=[out_shape] * 3,
        scratch_shapes=[pltpu.VMEM((tb + SUBLANES, c), F32)],
        compiler_params=pltpu.CompilerParams(dimension_semantics=("arbitrary",),
                                             vmem_limit_bytes=V7X_VMEM_LIMIT),
        name="delta_pre",
    )(qkv, qkv, w_conv)


def _delta_scan_kernel(wq_ref, kt_ref, u_ref, qk_ref, el_ref, z_ref, ng_ref, o_ref, sfin_ref, s_ref):
    n = pl.program_id(0)
    c = DELTA_CHUNK

    @pl.when(n == 0)
    def _():
        s_ref[...] = jnp.zeros(s_ref.shape, F32)

    for h in range(N_DELTA_HEADS):
        s = s_ref[h]
        ws = jnp.dot(wq_ref[h], s.astype(BF16), preferred_element_type=F32)
        v_corr = (u_ref[h] - ws[:c]).astype(BF16)
        o = ws[c:] + jnp.dot(qk_ref[h], v_corr, preferred_element_type=F32)
        s_ref[h] = s * el_ref[h] + jnp.dot(kt_ref[h], v_corr, preferred_element_type=F32)
        o = o * lax.rsqrt(jnp.mean(o * o, axis=1, keepdims=True) + 1e-6) * ng_ref[...]
        zz = z_ref[:, h * DELTA_DV:(h + 1) * DELTA_DV]
        o_ref[:, h * DELTA_DV:(h + 1) * DELTA_DV] = (o * (zz * jax.nn.sigmoid(zz))).astype(o_ref.dtype)

    @pl.when(n == pl.num_programs(0) - 1)
    def _():
        sfin_ref[...] = s_ref[...]


def prompt_delta_branch(p_delta, p_gate, t, w_conv, a_log, dt_bias, norm_g):
    assert t >= CONV_WIDTH - 1
    c, h = DELTA_CHUNK, N_DELTA_HEADS
    n = t // c
    new_buf = p_delta[t - (CONV_WIDTH - 1):t, :CONV_CH]
    qc, kc, vc = delta_pre(p_delta, t, w_conv)
    beta = jax.nn.sigmoid(p_delta[:t, CONV_CH:CONV_CH + h])
    g = -jnp.exp(a_log) * jax.nn.softplus(p_delta[:t, CONV_CH + h:CONV_CH + 2 * h] + dt_bias)
    bc = beta.reshape(n, c, h).transpose(0, 2, 1)
    gc = jnp.cumsum(g.reshape(n, c, h).transpose(0, 2, 1), axis=-1)
    tri = jnp.tril(jnp.ones((c, c), bool))
    strict = jnp.tril(jnp.ones((c, c), bool), -1)
    decay = jnp.exp(jnp.where(tri, gc[..., :, None] - gc[..., None, :], -jnp.inf))
    kb = kc * bc[..., None]
    a_low = jnp.where(strict, jnp.einsum('nhid,nhjd->nhij', kb, kc) * decay, 0.0)
    rhs = jnp.concatenate([vc * bc[..., None], kb * jnp.exp(gc)[..., None]], axis=-1)
    sol = lax.linalg.triangular_solve(a_low + jnp.eye(c, dtype=F32), rhs, left_side=True,
                                      lower=True, unit_diagonal=True)
    u, w = sol[..., :DELTA_DV], sol[..., DELTA_DV:]
    qk = jnp.where(tri, jnp.einsum('nhid,nhjd->nhij', qc, kc) * decay, 0.0)
    g_last = gc[..., -1:]
    wq = jnp.concatenate([w, qc * jnp.exp(gc)[..., None]], axis=-2).astype(BF16)
    kt = jnp.swapaxes(kc * jnp.exp(g_last - gc)[..., None], -1, -2).astype(BF16)
    e_last = jnp.broadcast_to(jnp.exp(g_last)[..., None], (n, h, 1, DELTA_DV))

    def per_chunk(*blk):
        return pl.BlockSpec((None,) + blk, lambda i: (i,) + (0,) * len(blk))

    d_out, s_fin = pl.pallas_call(
        _delta_scan_kernel,
        grid=(n,),
        in_specs=[per_chunk(h, 2 * c, DELTA_DK), per_chunk(h, DELTA_DK, c), per_chunk(h, c, DELTA_DV),
                  per_chunk(h, c, c), per_chunk(h, 1, DELTA_DV),
                  pl.BlockSpec((c, h * DELTA_DV), lambda i: (i, 0)),
                  pl.BlockSpec((1, DELTA_DV), lambda i: (0, 0))],
        out_specs=[pl.BlockSpec((c, h * DELTA_DV), lambda i: (i, 0)),
                   pl.BlockSpec((h, DELTA_DK, DELTA_DV), lambda i: (0, 0, 0))],
        out_shape=[jax.ShapeDtypeStruct((t, h * DELTA_DV), BF16),
                   jax.ShapeDtypeStruct((h, DELTA_DK, DELTA_DV), F32)],
        scratch_shapes=[pltpu.VMEM((h, DELTA_DK, DELTA_DV), F32)],
        compiler_params=pltpu.CompilerParams(dimension_semantics=("arbitrary",),
                                             vmem_limit_bytes=V7X_VMEM_LIMIT),
        name="delta_scan",
    )(wq, kt, u, qk.astype(BF16), e_last, p_gate, norm_g.reshape(1, DELTA_DV))
    return d_out, s_fin, new_buf


def routed_experts(xt, expert_idx, gate, w_gate, w_up, w_down):
    n, d = xt.shape
    k = expert_idx.shape[1]
    n_exp = w_gate.shape[0]
    blk = EXPERT_BLOCK
    flat_e = expert_idx.reshape(-1)
    order = jnp.argsort(flat_e).astype(jnp.int32)
    counts = jnp.bincount(flat_e, length=n_exp).astype(jnp.int32)
    padded = (counts + blk - 1) // blk * blk
    pad_end = jnp.cumsum(padded)
    pad_start = pad_end - padded
    start = jnp.cumsum(counts) - counts
    n_blocks = -(-(n * k) // blk) + n_exp
    rows = n_blocks * blk
    blk_row0 = jnp.arange(n_blocks, dtype=jnp.int32) * blk
    block_expert = jnp.minimum(jnp.sum(pad_end[None, :] <= blk_row0[:, None], axis=1), n_exp - 1).astype(jnp.int32)
    n_used = (pad_end[-1:] // blk).astype(jnp.int32)
    r = jnp.arange(rows, dtype=jnp.int32)
    rank = (r.reshape(n_blocks, blk) - pad_start[block_expert][:, None])
    real = (rank < counts[block_expert][:, None]).reshape(rows)
    src = order[jnp.clip(start[block_expert][:, None] + rank, 0, n * k - 1).reshape(rows)]
    row_token = jnp.where(real, src // k, r % n)
    row_gate = jnp.where(real, gate.reshape(-1)[src], 0.0)
    xb = xt.astype(BF16)
    grp_blocks = -(-n_blocks // ROW_GROUPS)
    h_rows = None
    for b0 in range(0, n_blocks, grp_blocks):
        b1 = min(b0 + grp_blocks, n_blocks)
        x_rows = xb[row_token[b0 * blk:b1 * blk]]
        h_rows = grouped_swiglu_up(x_rows, block_expert[b0:b1], jnp.clip(n_used - b0, 0, b1 - b0),
                                   w_gate, w_up, blk, b0, n_blocks, h_prev=h_rows)
    y_rows = grouped_down(h_rows, row_gate[:, None], block_expert, n_used, w_down, blk)
    slot = jnp.argsort(order).astype(jnp.int32)
    e_of = flat_e.astype(jnp.int32)
    pos = pad_start[e_of] + slot - start[e_of]
    return y_rows[pos.reshape(n, k).T.reshape(-1)].reshape(k, n, d)


def moe(xt, w_router, router_bias, w_exp_gate, w_exp_up, w_exp_down, w_sh_gate, w_sh_up, w_sh_down):
    n = xt.shape[0]
    scores = jax.nn.sigmoid(jnp.matmul(xt, w_router, preferred_element_type=F32))
    biased = scores + router_bias
    per_group = N_EXPERTS // N_GROUPS
    group_score = lax.top_k(biased.reshape(n, N_GROUPS, per_group), 2)[0].sum(-1)
    _, top_groups = lax.top_k(group_score, TOPK_GROUPS)
    group_keep = jnp.any(top_groups[:, :, None] == jnp.arange(N_GROUPS)[None, None, :], axis=1)
    biased = jnp.where(jnp.repeat(group_keep, per_group, axis=1), biased, -jnp.inf)
    _, expert_idx = lax.top_k(biased, TOP_K)
    gate = jnp.take_along_axis(scores, expert_idx, axis=1)
    gate = gate / jnp.sum(gate, -1, keepdims=True) * ROUTED_SCALE
    routed_parts = routed_experts(xt, expert_idx, gate, w_exp_gate, w_exp_up, w_exp_down)
    xb = xt.astype(BF16)
    shared = matmul(swiglu_up(xb, w_sh_gate, w_sh_up), w_sh_down)
    return routed_parts, shared


def _combine_ln_kernel(y_ref, sh_ref, h_ref, g_ref, b_ref, o_ref):
    acc = ALPHA * h_ref[...] + sh_ref[...]
    for j in range(y_ref.shape[0]):
        acc = acc + y_ref[j].astype(F32)
    xc = acc - jnp.mean(acc, axis=1, keepdims=True)
    var = jnp.mean(xc * xc, axis=1, keepdims=True)
    o_ref[...] = xc * lax.rsqrt(var + LN_EPS) * g_ref[...] + b_ref[...]


def combine_layer_norm(routed_parts, shared, h, g, b, tm=128):
    n, d = h.shape
    k = routed_parts.shape[0]
    assert n % tm == 0 and routed_parts.shape[1:] == (n, d)
    row = lambda i: (i, 0)
    return pl.pallas_call(
        _combine_ln_kernel,
        grid=(n // tm,),
        in_specs=[pl.BlockSpec((k, tm, d), lambda i: (0, i, 0)), pl.BlockSpec((tm, d), row), pl.BlockSpec((tm, d), row),
                  pl.BlockSpec((1, d), lambda i: (0, 0)), pl.BlockSpec((1, d), lambda i: (0, 0))],
        out_specs=pl.BlockSpec((tm, d), row),
        out_shape=jax.ShapeDtypeStruct((n, d), F32),
        compiler_params=pltpu.CompilerParams(dimension_semantics=("arbitrary",),
                                             vmem_limit_bytes=V7X_VMEM_LIMIT),
        name="combine_layer_norm",
    )(routed_parts, shared, h, g.reshape(1, d), b.reshape(1, d))


def kernel(x_prompt, x_sample, cache_k, cache_v, cache_idx_k, page_table, state_delta, state_conv, w_in, w_conv, a_log, dt_bias, delta_norm_g, w_branch_attn, w_branch_delta, w_out, ln1_g, ln1_b, w_router, router_bias, w_exp_gate, w_exp_up, w_exp_down, w_sh_gate, w_sh_up, w_sh_down, ln2_g, ln2_b):
    b, s, d = x_prompt.shape
    db, t = x_sample.shape[:2]
    assert DEPTH == 1 and b == 1
    past = page_table.shape[1] * cache_k.shape[2]
    n_p, n_s = b * s, db * t
    (l_in, l_conv, l_alog, l_dtb, l_ng, l_ba, l_bd, l_out, l_g1, l_b1,
     l_r, l_rb, l_eg, l_eu, l_ed, l_sg, l_su, l_sd, l_g2, l_b2) = [a[0] for a in (
         w_in, w_conv, a_log, dt_bias, delta_norm_g, w_branch_attn, w_branch_delta, w_out,
         ln1_g, ln1_b, w_router, router_bias, w_exp_gate, w_exp_up, w_exp_down,
         w_sh_gate, w_sh_up, w_sh_down, ln2_g, ln2_b)]

    x_all = jnp.concatenate([x_prompt.reshape(n_p, d), x_sample.reshape(n_s, d)], axis=0)
    xb = x_all.astype(BF16)
    w_in_t = jnp.swapaxes(l_in, 0, 1)
    p_attn = matmul_rows(xb, w_in_t, 0, ATTN_COLS)
    p_delta = matmul_rows(xb, w_in_t, IN_SPLITS[5], DELTA_COLS)
    p_gate = matmul_rows(xb, w_in_t, IN_SPLITS[8], GATE_COLS)
    ap = attention_inputs(p_attn[:n_p].reshape(b, s, -1), jnp.arange(s))
    at = attention_inputs(p_attn[n_p:].reshape(db, t, -1), past + jnp.arange(t))

    a_p = prompt_sparse_attention(ap.q, ap.k, ap.v, ap.iq, ap.iw, ap.ik)
    a_s = sample_sparse_attention(at.q, at.k, at.v, at.iq, at.iw, at.ik,
                                  cache_k[0], cache_v[0], cache_idx_k[0], page_table)
    d_p, sd_p, sc_p = prompt_delta_branch(p_delta, p_gate, n_p, l_conv, l_alog, l_dtb, l_ng)
    pd_s = p_delta[n_p:].reshape(db, t, -1)
    nh = N_DELTA_HEADS
    d_s, sd_s, sc_s = gated_delta_branch(pd_s[..., :CONV_CH], pd_s[..., CONV_CH:CONV_CH + nh], pd_s[..., CONV_CH + nh:],
                                         p_gate[n_p:, :DELTA_V_WIDTH].reshape(db, t, -1), state_conv[0], state_delta[0],
                                         l_conv, l_alog, l_dtb, l_ng)

    a_all = jnp.concatenate([a_p.reshape(n_p, -1).astype(BF16), a_s.reshape(n_s, -1).astype(BF16)], axis=0)
    d_all = jnp.concatenate([d_p, d_s.reshape(n_s, -1).astype(BF16)], axis=0)
    merged = merge_branches(a_all, d_all, p_gate, DELTA_V_WIDTH, l_ba, l_bd)
    h = layer_norm(ALPHA * x_all + matmul(merged, l_out), l_g1, l_b1)
    routed_parts, shared = moe(h, l_r, l_rb, l_eg, l_eu, l_ed, l_sg, l_su, l_sd)
    y = combine_layer_norm(routed_parts, shared, h, l_g2, l_b2)
    return (y[:n_p].reshape(b, s, d), y[n_p:].reshape(db, t, d),
            ap.k[None], ap.v[None], ap.ik[None], sd_p[None, None], sc_p[None, None],
            at.k[None], at.v[None], at.ik[None], sd_s[None], sc_s[None])
```

```python
import collections
import functools

import jax
import jax.numpy as jnp
import numpy as np
from jax import lax
from jax.experimental import pallas as pl
from jax.experimental.pallas import tpu as pltpu

D_MODEL = 4096
DEPTH = 1
N_HEADS = 16
N_KV_HEADS = 4
HEAD_DIM = 128
ROPE_THETA = 500000.0
N_IDX_HEADS = 16
IDX_DIM = 128
IDX_TOPK = 256
Q_BLOCK = 128
N_DELTA_HEADS = 16
DELTA_DK = 128
DELTA_DV = 128
CONV_WIDTH = 4
DELTA_CHUNK = 64
N_EXPERTS = 64
TOP_K = 8
N_GROUPS = 8
TOPK_GROUPS = 4
EXPERT_DIM = 1024
ROUTED_SCALE = 2.5
EXPERT_BLOCK = 512
ROW_GROUPS = 2
ALPHA = (2 * DEPTH) ** 0.25
LN_EPS = 1e-5
ATTN_WIDTH = N_HEADS * HEAD_DIM
KV_WIDTH = N_KV_HEADS * HEAD_DIM
IDXQ_WIDTH = N_IDX_HEADS * IDX_DIM
DELTA_QK_WIDTH = N_DELTA_HEADS * DELTA_DK
DELTA_V_WIDTH = N_DELTA_HEADS * DELTA_DV
CONV_CH = 2 * DELTA_QK_WIDTH + DELTA_V_WIDTH
IN_SIZES = (ATTN_WIDTH, KV_WIDTH, KV_WIDTH, IDXQ_WIDTH, IDX_DIM, N_IDX_HEADS, CONV_CH,
            N_DELTA_HEADS, N_DELTA_HEADS, DELTA_V_WIDTH, 2 * D_MODEL)
IN_SPLITS = tuple(int(s) for s in np.cumsum(IN_SIZES)[:-1])

V7X_VMEM_LIMIT = 56 * 1024 * 1024
BF16 = jnp.bfloat16
F32 = jnp.float32
SUBLANES = 8


def _mm_kernel(x_ref, w_ref, o_ref, wbf_ref):
    @pl.when(pl.program_id(1) == 0)
    def _():
        wbf_ref[...] = w_ref[...].astype(BF16)

    o_ref[...] = jnp.dot(x_ref[...], wbf_ref[...], preferred_element_type=F32).astype(o_ref.dtype)


def _row_tile(m):
    for tm in (1024, 1088, 512, 256, 128):
        if m % tm == 0:
            return tm
    raise ValueError(f"no row tile for {m} rows")


def matmul(x, w, out_dtype=F32, tn=512):
    m, k = x.shape
    n = w.shape[1]
    tm = _row_tile(m)
    tn = min(tn, n)
    return pl.pallas_call(
        _mm_kernel,
        grid=(pl.cdiv(n, tn), m // tm),
        in_specs=[pl.BlockSpec((tm, k), lambda j, i: (i, 0)),
                  pl.BlockSpec((k, tn), lambda j, i: (0, j))],
        out_specs=pl.BlockSpec((tm, tn), lambda j, i: (i, j)),
        out_shape=jax.ShapeDtypeStruct((m, n), out_dtype),
        scratch_shapes=[pltpu.VMEM((k, tn), BF16)],
        compiler_params=pltpu.CompilerParams(dimension_semantics=("arbitrary", "arbitrary"),
                                             vmem_limit_bytes=V7X_VMEM_LIMIT),
        name="dense_matmul",
    )(x, w)


def _swiglu_kernel(x_ref, wg_ref, wu_ref, o_ref, wg_bf, wu_bf):
    @pl.when(pl.program_id(1) == 0)
    def _():
        wg_bf[...] = wg_ref[...].astype(BF16)
        wu_bf[...] = wu_ref[...].astype(BF16)

    x = x_ref[...]
    g = jnp.dot(x, wg_bf[...], preferred_element_type=F32)
    u = jnp.dot(x, wu_bf[...], preferred_element_type=F32)
    o_ref[...] = (g * jax.nn.sigmoid(g) * u).astype(o_ref.dtype)


def swiglu_up(x, w_gate, w_up, tn=256):
    m, k = x.shape
    f = w_gate.shape[1]
    tm = _row_tile(m)
    tn = min(tn, f)
    assert f % tn == 0 and w_up.shape == w_gate.shape
    return pl.pallas_call(
        _swiglu_kernel,
        grid=(f // tn, m // tm),
        in_specs=[pl.BlockSpec((tm, k), lambda j, i: (i, 0)),
                  pl.BlockSpec((k, tn), lambda j, i: (0, j)),
                  pl.BlockSpec((k, tn), lambda j, i: (0, j))],
        out_specs=pl.BlockSpec((tm, tn), lambda j, i: (i, j)),
        out_shape=jax.ShapeDtypeStruct((m, f), BF16),
        scratch_shapes=[pltpu.VMEM((k, tn), BF16), pltpu.VMEM((k, tn), BF16)],
        compiler_params=pltpu.CompilerParams(dimension_semantics=("arbitrary", "arbitrary"),
                                             vmem_limit_bytes=V7X_VMEM_LIMIT),
        name="swiglu_up",
    )(x, w_gate, w_up)


def _merge_kernel(a_ref, d_ref, ga_ref, gd_ref, wa_ref, wd_ref, o_ref, wa_bf, wd_bf):
    @pl.when(pl.program_id(1) == 0)
    def _():
        wa_bf[...] = wa_ref[...].astype(BF16)
        wd_bf[...] = wd_ref[...].astype(BF16)

    pa = jnp.dot(a_ref[...], wa_bf[...], preferred_element_type=F32)
    pd = jnp.dot(d_ref[...], wd_bf[...], preferred_element_type=F32)
    o_ref[...] = (jax.nn.sigmoid(ga_ref[...]) * pa + jax.nn.sigmoid(gd_ref[...]) * pd).astype(o_ref.dtype)


def merge_branches(a, d, p_gate, gate_col0, w_a, w_d, tn=512):
    m, k = a.shape
    n = w_a.shape[1]
    tm = _row_tile(m)
    assert gate_col0 % tn == 0 and n % tn == 0 and w_d.shape == w_a.shape and d.shape == a.shape
    ga0, gd0 = gate_col0 // tn, (gate_col0 + n) // tn
    return pl.pallas_call(
        _merge_kernel,
        grid=(n // tn, m // tm),
        in_specs=[pl.BlockSpec((tm, k), lambda j, i: (i, 0)),
                  pl.BlockSpec((tm, k), lambda j, i: (i, 0)),
                  pl.BlockSpec((tm, tn), lambda j, i: (i, ga0 + j)),
                  pl.BlockSpec((tm, tn), lambda j, i: (i, gd0 + j)),
                  pl.BlockSpec((k, tn), lambda j, i: (0, j)),
                  pl.BlockSpec((k, tn), lambda j, i: (0, j))],
        out_specs=pl.BlockSpec((tm, tn), lambda j, i: (i, j)),
        out_shape=jax.ShapeDtypeStruct((m, n), BF16),
        scratch_shapes=[pltpu.VMEM((k, tn), BF16), pltpu.VMEM((k, tn), BF16)],
        compiler_params=pltpu.CompilerParams(dimension_semantics=("arbitrary", "arbitrary"),
                                             vmem_limit_bytes=V7X_VMEM_LIMIT),
        name="merge_branches",
    )(a, d, p_gate, p_gate, w_a, w_d)


def _mm_rows_kernel(x_ref, wt_ref, o_ref, wbf_ref):
    @pl.when(pl.program_id(1) == 0)
    def _():
        wbf_ref[...] = wt_ref[...].astype(BF16)

    o_ref[...] = lax.dot_general(x_ref[...], wbf_ref[...], (((1,), (1,)), ((), ())),
                                 preferred_element_type=F32).astype(o_ref.dtype)


def matmul_rows(x, wt, row0, nrows, out_dtype=F32, tn=512):
    m, k = x.shape
    tm = _row_tile(m)
    tn = min(tn, nrows)
    assert row0 % SUBLANES == 0 and wt.shape[1] == k
    w_spec = pl.BlockSpec((pl.Element(tn, (0, tn)), pl.Element(k)),
                          lambda j, i: (pl.multiple_of(row0 + j * tn, SUBLANES), 0))
    return pl.pallas_call(
        _mm_rows_kernel,
        grid=(pl.cdiv(nrows, tn), m // tm),
        in_specs=[pl.BlockSpec((tm, k), lambda j, i: (i, 0)), w_spec],
        out_specs=pl.BlockSpec((tm, tn), lambda j, i: (i, j)),
        out_shape=jax.ShapeDtypeStruct((m, nrows), out_dtype),
        scratch_shapes=[pltpu.VMEM((tn, k), BF16)],
        compiler_params=pltpu.CompilerParams(dimension_semantics=("arbitrary", "arbitrary"),
                                             vmem_limit_bytes=V7X_VMEM_LIMIT),
        name="dense_matmul_rows",
    )(x, wt)


def _expert_changed(be_ref, i):
    prev = be_ref[jnp.maximum(i - 1, 0)]
    return jnp.logical_or(i == 0, be_ref[i] != prev)


def _moe_up_kernel(be_ref, nu_ref, x_ref, wg_ref, wu_ref, *rest):
    h_ref, wg_bf, wu_bf = rest[-3:]
    i = pl.program_id(1)

    @pl.when(i < nu_ref[0])
    def _():
        @pl.when(_expert_changed(be_ref, i))
        def _():
            wg_bf[...] = wg_ref[...].astype(BF16)
            wu_bf[...] = wu_ref[...].astype(BF16)

        x = x_ref[...]
        g = jnp.dot(x, wg_bf[...], preferred_element_type=F32)
        u = jnp.dot(x, wu_bf[...], preferred_element_type=F32)
        h_ref[...] = (g * jax.nn.sigmoid(g) * u).astype(h_ref.dtype)


def _moe_down_kernel(be_ref, nu_ref, h_hbm, gate_hbm, wd_hbm, y_hbm, wd_bf, *, blk, tn):
    f = h_hbm.shape[1]

    def body(idx, h_ref, gate_ref, wd_ref, y_ref):
        _, i = idx

        @pl.when(_expert_changed(be_ref, i))
        def _():
            wd_bf[...] = wd_ref[...].astype(BF16)

        y = jnp.dot(h_ref[...], wd_bf[...], preferred_element_type=F32)
        y_ref[...] = (y * gate_ref[...]).astype(y_ref.dtype)

    pltpu.emit_pipeline(
        body,
        grid=(y_hbm.shape[1] // tn, nu_ref[0]),
        in_specs=[pl.BlockSpec((blk, f), lambda j, i: (i, 0)),
                  pl.BlockSpec((blk, 1), lambda j, i: (i, 0)),
                  pl.BlockSpec((None, f, tn), lambda j, i: (be_ref[i], 0, j),
                               pipeline_mode=pl.Buffered(2, use_lookahead=True))],
        out_specs=[pl.BlockSpec((blk, tn), lambda j, i: (i, j))],
        _explicit_indices=True,
    )(h_hbm, gate_hbm, wd_hbm, y_hbm)


def grouped_swiglu_up(x_rows, block_expert, n_used, w_gate, w_up, blk, block0, total_blocks, h_prev=None, tf=512):
    rows, d = x_rows.shape
    f = w_gate.shape[2]
    tf = min(tf, f)
    n_blocks = rows // blk
    in_specs = [pl.BlockSpec((blk, d), lambda j, i, be, nu: (i, 0)),
                pl.BlockSpec((None, d, tf), lambda j, i, be, nu: (be[i], 0, j)),
                pl.BlockSpec((None, d, tf), lambda j, i, be, nu: (be[i], 0, j))]
    operands = [block_expert, n_used, x_rows, w_gate, w_up]
    aliases = {}
    if h_prev is not None:
        in_specs.append(pl.BlockSpec(memory_space=pl.ANY))
        operands.append(h_prev)
        aliases = {len(operands) - 1: 0}
    grid_spec = pltpu.PrefetchScalarGridSpec(
        num_scalar_prefetch=2,
        grid=(f // tf, n_blocks),
        in_specs=in_specs,
        out_specs=pl.BlockSpec((blk, tf), lambda j, i, be, nu: (block0 + i, j)),
        scratch_shapes=[pltpu.VMEM((d, tf), BF16), pltpu.VMEM((d, tf), BF16)],
    )
    return pl.pallas_call(
        _moe_up_kernel, grid_spec=grid_spec,
        out_shape=jax.ShapeDtypeStruct((total_blocks * blk, f), BF16),
        input_output_aliases=aliases,
        compiler_params=pltpu.CompilerParams(dimension_semantics=("arbitrary", "arbitrary"),
                                             vmem_limit_bytes=V7X_VMEM_LIMIT),
        name="moe_up",
    )(*operands)


def grouped_down(h_rows, row_gate, block_expert, n_used, w_down, blk, tn=2048):
    rows, f = h_rows.shape
    d = w_down.shape[2]
    tn = min(tn, d)
    assert d % tn == 0 and rows % blk == 0
    in_hbm = pl.BlockSpec(memory_space=pl.ANY)
    grid_spec = pltpu.PrefetchScalarGridSpec(
        num_scalar_prefetch=2,
        grid=(),
        in_specs=[in_hbm] * 3,
        out_specs=pl.BlockSpec(memory_space=pl.ANY),
        scratch_shapes=[pltpu.VMEM((f, tn), BF16)],
    )
    return pl.pallas_call(
        functools.partial(_moe_down_kernel, blk=blk, tn=tn), grid_spec=grid_spec,
        out_shape=jax.ShapeDtypeStruct((rows, d), BF16),
        compiler_params=pltpu.CompilerParams(vmem_limit_bytes=V7X_VMEM_LIMIT),
        name="moe_down",
    )(block_expert, n_used, h_rows, row_gate, w_down)


def layer_norm(x, g, b):
    xc = x - jnp.mean(x, -1, keepdims=True)
    var = jnp.mean(xc * xc, -1, keepdims=True)
    return xc * lax.rsqrt(var + LN_EPS) * g + b


def l2norm(x):
    return x * lax.rsqrt(jnp.sum(x * x, -1, keepdims=True) + 1e-6)


def partial_rope(x, pos):
    rot = x.shape[-1] // 4
    half = rot // 2
    inv_freq = ROPE_THETA ** (-jnp.arange(half, dtype=F32) / half)
    ang = pos.astype(F32)[:, None] * inv_freq[None, :]
    cos = jnp.cos(ang)[:, None, :]
    sin = jnp.sin(ang)[:, None, :]
    x1, x2, rest = x[..., :half], x[..., half:rot], x[..., rot:]
    return jnp.concatenate([x1 * cos - x2 * sin, x2 * cos + x1 * sin, rest], axis=-1)


ATTN_COLS = IN_SPLITS[5]
DELTA_COLS = IN_SPLITS[8] - IN_SPLITS[5]
GATE_COLS = int(sum(IN_SIZES)) - IN_SPLITS[8]
AttnInputs = collections.namedtuple("AttnInputs", "q k v iq ik iw")


def attention_inputs(p, pos):
    b, t, _ = p.shape
    q, k, v, iq, ik, iw = jnp.split(p, IN_SPLITS[:5], axis=-1)
    q = partial_rope(q.reshape(b, t, N_HEADS, HEAD_DIM), pos)
    k = partial_rope(k.reshape(b, t, N_KV_HEADS, HEAD_DIM), pos)
    v = v.reshape(b, t, N_KV_HEADS, HEAD_DIM)
    iq = partial_rope(iq.reshape(b, t, N_IDX_HEADS, IDX_DIM), pos)
    ik = partial_rope(ik[:, :, None, :], pos)[:, :, 0, :]
    iw = iw * (N_IDX_HEADS ** -0.5 * IDX_DIM ** -0.5)
    return AttnInputs(q, k, v, iq, ik, iw)


KEY_CHUNK = 512
IDX_HEAD_GROUP = 4
MASKED = -1e30
SOFTMAX_LOG2_SCALE = HEAD_DIM ** -0.5 * float(np.log2(np.e))
INT32_MIN = -2 ** 31
INT32_MAX = 2 ** 31 - 1
NEG_INF_KEY = int(np.int32(np.uint32(0xFF800000) ^ np.uint32(0x7FFFFFFF)))


def _sort_key(x):
    bits = lax.bitcast_convert_type(x, jnp.int32)
    return bits ^ (jnp.right_shift(bits, 31) & jnp.int32(INT32_MAX))


def _lane_tile(x, n):
    return x if n == 1 else jnp.concatenate([x] * n, axis=1)


def _count_rows(keys_ref, nkc, preds):
    _, rows, kc = keys_ref.shape

    def body(c, parts):
        keys = keys_ref[c]
        out = []
        for pred, part in zip(preds, parts):
            hit = pred(keys, c)
            for j in range(kc // 128):
                part = part + hit[:, j * 128:(j + 1) * 128]
            out.append(part)
        return tuple(out)

    parts = lax.fori_loop(0, nkc, body, tuple(jnp.zeros((rows, 128), F32) for _ in preds))
    return [jnp.broadcast_to(jnp.sum(part, axis=1, keepdims=True), (rows, 128)) for part in parts]


def _topk_selection(keys_ref, nkc, n_sel, radix_bits):
    _, rows, kc = keys_ref.shape
    reps = kc // 128
    col = lax.broadcasted_iota(jnp.int32, (rows, kc), 1)

    def thr_digit(it, carry):
        thr, cnt_thr = carry
        shift = 32 - radix_bits * (it + 1)
        cands = [thr + jnp.left_shift(jnp.int32(j), shift) for j in range(1, 2 ** radix_bits)]
        cands_w = [_lane_tile(cand, reps) for cand in cands]
        cnts = _count_rows(keys_ref, nkc, [lambda keys, c, cw=cw: jnp.where(keys >= cw, 1.0, 0.0) for cw in cands_w])
        for cand, cnt in zip(cands, cnts):
            ok = cnt >= n_sel
            thr = jnp.where(ok, cand, thr)
            cnt_thr = jnp.where(ok, cnt, cnt_thr)
        return thr, cnt_thr

    thr0 = jnp.full((rows, 128), INT32_MIN, jnp.int32)
    cnt0 = jnp.broadcast_to(jnp.asarray(nkc * kc).astype(F32), (rows, 128))
    thr, cnt_thr = lax.fori_loop(0, 32 // radix_bits, thr_digit, (thr0, cnt0))

    short = thr == NEG_INF_KEY
    thr_w = _lane_tile(thr, reps)
    cnt_gt, = _count_rows(keys_ref, nkc, [lambda keys, c: jnp.where(keys > thr_w, 1.0, 0.0)])
    need = n_sel - cnt_gt
    tied = jnp.logical_and(cnt_thr > n_sel, jnp.logical_not(short))

    def tie_limit():
        def idx_bit(it, x):
            cand = x + jnp.left_shift(jnp.int32(1), 30 - it)
            cand_w = _lane_tile(cand, reps)
            cnt, = _count_rows(keys_ref, nkc, [lambda keys, c: jnp.where(
                keys == thr_w, jnp.where(c * kc + col < cand_w, 1.0, 0.0), 0.0)])
            return jnp.where(cnt < need, cand, x)
        return lax.fori_loop(0, 31, idx_bit, jnp.zeros((rows, 128), jnp.int32))

    any_tied = jnp.max(jnp.where(tied, 1.0, 0.0)) > 0.0
    lim = lax.cond(any_tied, tie_limit, lambda: jnp.full((rows, 128), INT32_MAX, jnp.int32))
    lim = jnp.where(short, -1, jnp.where(tied, lim, INT32_MAX))
    return thr, lim


def _selected(keys, index, thr_w, lim_w, yes, no):
    keep_tie = jnp.where(index <= lim_w, yes, no)
    return jnp.where(keys > thr_w, yes, jnp.where(keys == thr_w, keep_tie, no))


def _dsa_prompt_kernel(iq_ref, iw_ref, ikt_ref, q_ref, kt_ref, v_ref, o_ref,
                       keys_ref, m_ref, l_ref, acc_ref, *, n_sel):
    qb, kc = Q_BLOCK, KEY_CHUNK
    reps = kc // 128
    grp = N_HEADS // N_KV_HEADS
    i = pl.program_id(0)
    nkc = (i * qb + qb + kc - 1) // kc
    qpos = i * qb + lax.broadcasted_iota(jnp.int32, (qb, kc), 0)
    col = lax.broadcasted_iota(jnp.int32, (qb, kc), 1)

    def score_chunk(c, carry):
        ikc = ikt_ref[c]
        acc = jnp.zeros((qb, kc), F32)
        for h0 in range(0, N_IDX_HEADS, IDX_HEAD_GROUP):
            iq_rows = iq_ref[h0:h0 + IDX_HEAD_GROUP].reshape(IDX_HEAD_GROUP * qb, IDX_DIM)
            s = jnp.dot(iq_rows, ikc, preferred_element_type=F32)
            for j in range(IDX_HEAD_GROUP):
                acc = acc + jnp.maximum(s[j * qb:(j + 1) * qb], 0.0) * iw_ref[:, h0 + j:h0 + j + 1]
        acc = jnp.where(c * kc + col <= qpos, acc, -jnp.inf)
        keys_ref[c] = _sort_key(acc)
        return carry

    lax.fori_loop(0, nkc, score_chunk, 0)

    thr, lim = _topk_selection(keys_ref, nkc, n_sel, radix_bits=1)
    thr_w, lim_w = _lane_tile(thr, reps), _lane_tile(lim, reps)

    m_ref[...] = jnp.full(m_ref.shape, MASKED, F32)
    l_ref[...] = jnp.zeros(l_ref.shape, F32)
    acc_ref[...] = jnp.zeros(acc_ref.shape, F32)
    rows = grp * qb

    def attend_chunk(c, carry):
        bias = _selected(keys_ref[c], c * kc + col, thr_w, lim_w, 0.0, MASKED)
        for n in range(N_KV_HEADS):
            r = pl.ds(n * rows, rows)
            qn = q_ref[n * grp:(n + 1) * grp].reshape(rows, HEAD_DIM)
            s = jnp.dot(qn, kt_ref[n, c], preferred_element_type=F32) * SOFTMAX_LOG2_SCALE
            s = (s.reshape(grp, qb, kc) + bias[None]).reshape(rows, kc)
            m_prev = m_ref[r, :]
            m_new = jnp.maximum(m_prev, jnp.max(s, axis=1, keepdims=True))
            alpha = jnp.exp2(m_prev - m_new)
            p = jnp.exp2(s - _lane_tile(m_new, reps))
            l_ref[r, :] = alpha * l_ref[r, :] + jnp.sum(p, axis=1, keepdims=True)
            acc_ref[r, :] = alpha * acc_ref[r, :] + jnp.dot(p.astype(BF16), v_ref[n, c],
                                                           preferred_element_type=F32)
            m_ref[r, :] = m_new
        return carry

    lax.fori_loop(0, nkc, attend_chunk, 0)
    for h in range(N_HEADS):
        r = pl.ds(h * qb, qb)
        o_ref[:, h * HEAD_DIM:(h + 1) * HEAD_DIM] = (acc_ref[r, :] / l_ref[r, :]).astype(o_ref.dtype)


def prompt_sparse_attention(q, k, v, iq, iw, ik):
    b, s = q.shape[:2]
    assert b == 1 and s % KEY_CHUNK == 0 and KEY_CHUNK % Q_BLOCK == 0 and KEY_CHUNK >= IDX_TOPK
    n_sel = min(IDX_TOPK, s // 4)
    nb, nc, kc = s // Q_BLOCK, s // KEY_CHUNK, KEY_CHUNK

    def head_major(a):
        return a.astype(BF16).reshape(nb, Q_BLOCK, a.shape[2], a.shape[3]).transpose(0, 2, 1, 3)

    ikt = ik.astype(BF16).reshape(nc, kc, IDX_DIM).transpose(0, 2, 1)
    kt = k.astype(BF16).reshape(nc, kc, N_KV_HEADS, HEAD_DIM).transpose(2, 0, 3, 1)
    vc = v.astype(BF16).reshape(nc, kc, N_KV_HEADS, HEAD_DIM).transpose(2, 0, 1, 3)
    resident = dict(pipeline_mode=pl.Buffered(1))
    out = pl.pallas_call(
        functools.partial(_dsa_prompt_kernel, n_sel=n_sel),
        grid=(nb,),
        in_specs=[pl.BlockSpec((None, N_IDX_HEADS, Q_BLOCK, IDX_DIM), lambda i: (i, 0, 0, 0)),
                  pl.BlockSpec((Q_BLOCK, N_IDX_HEADS), lambda i: (i, 0)),
                  pl.BlockSpec((nc, IDX_DIM, kc), lambda i: (0, 0, 0), **resident),
                  pl.BlockSpec((None, N_HEADS, Q_BLOCK, HEAD_DIM), lambda i: (i, 0, 0, 0)),
                  pl.BlockSpec((N_KV_HEADS, nc, HEAD_DIM, kc), lambda i: (0, 0, 0, 0), **resident),
                  pl.BlockSpec((N_KV_HEADS, nc, kc, HEAD_DIM), lambda i: (0, 0, 0, 0), **resident)],
        out_specs=pl.BlockSpec((Q_BLOCK, N_HEADS * HEAD_DIM), lambda i: (i, 0)),
        out_shape=jax.ShapeDtypeStruct((s, N_HEADS * HEAD_DIM), BF16),
        scratch_shapes=[pltpu.VMEM((nc, Q_BLOCK, kc), jnp.int32),
                        pltpu.VMEM((N_HEADS * Q_BLOCK, 128), F32),
                        pltpu.VMEM((N_HEADS * Q_BLOCK, 128), F32),
                        pltpu.VMEM((N_HEADS * Q_BLOCK, HEAD_DIM), F32)],
        compiler_params=pltpu.CompilerParams(dimension_semantics=("arbitrary",),
                                             vmem_limit_bytes=V7X_VMEM_LIMIT),
        name="dsa_prompt",
    )(head_major(iq), iw.reshape(s, N_IDX_HEADS), ikt, head_major(q), kt, vc)
    return out.reshape(b, s, N_HEADS * HEAD_DIM)


T_PAD = 8


def _dsa_sample_kernel(pt_ref, iq_ref, iw_ref, q_ref, iknew_ref, knew_ref, vnew_ref, *rest, n_sel, n_pages):
    del pt_ref
    ik_pages, k_pages, v_pages = rest[:n_pages], rest[n_pages:2 * n_pages], rest[2 * n_pages:3 * n_pages]
    o_ref, keys_ref, s_ref = rest[3 * n_pages:]
    page = ik_pages[0].shape[0]
    kv_rows = k_pages[0].shape[0]
    rows = N_HEADS * T_PAD
    nkc = n_pages + 1
    nt = (((1,), (1,)), ((), ()))
    row_t = lax.broadcasted_iota(jnp.int32, (T_PAD, page), 0)
    col = lax.broadcasted_iota(jnp.int32, (T_PAD, page), 1)

    for c in range(nkc):
        ikc = ik_pages[c][...].astype(BF16) if c < n_pages else iknew_ref[...]
        s = lax.dot_general(iq_ref[...], ikc, nt, preferred_element_type=F32)
        x = jnp.maximum(s, 0.0) * iw_ref[...]
        sc = jnp.sum(x.reshape(T_PAD, N_IDX_HEADS, page), axis=1)
        if c == n_pages:
            sc = jnp.where(col <= row_t, sc, -jnp.inf)
        keys_ref[c] = _sort_key(sc)

    thr, lim = _topk_selection(keys_ref, nkc, n_sel, radix_bits=4)

    scale = SOFTMAX_LOG2_SCALE
    spread = (lax.broadcasted_iota(jnp.int32, (page, kv_rows), 1) // N_KV_HEADS
              == lax.broadcasted_iota(jnp.int32, (page, kv_rows), 0)).astype(BF16)
    own_head = (lax.broadcasted_iota(jnp.int32, (rows, kv_rows), 0) // (rows // N_KV_HEADS)
                == lax.broadcasted_iota(jnp.int32, (rows, kv_rows), 1) % N_KV_HEADS).astype(F32)
    for c in range(nkc):
        kc_ = k_pages[c][...].astype(BF16) if c < n_pages else knew_ref[...]
        s = lax.dot_general(q_ref[...], kc_, nt, preferred_element_type=F32) * scale
        sel = _selected(keys_ref[c], c * page + col, thr, lim, 1.0, 0.0).astype(BF16)
        sel = jnp.dot(sel, spread, preferred_element_type=F32)
        keep = jnp.broadcast_to(sel[None], (N_HEADS, T_PAD, kv_rows)).reshape(rows, kv_rows) * own_head
        s_ref[:, c * kv_rows:(c + 1) * kv_rows] = jnp.where(keep > 0.5, s, MASKED)
    s = s_ref[...]
    p = jnp.exp2(s - jnp.max(s, axis=1, keepdims=True))
    inv_l = 1.0 / jnp.sum(p, axis=1, keepdims=True)
    p = p.astype(BF16)
    o = jnp.zeros((rows, HEAD_DIM), F32)
    for c in range(nkc):
        vc_ = v_pages[c][...].astype(BF16) if c < n_pages else vnew_ref[...]
        o = o + jnp.dot(p[:, c * kv_rows:(c + 1) * kv_rows], vc_, preferred_element_type=F32)
    o_ref[...] = o * inv_l


def sample_sparse_attention(q, k_new, v_new, iq, iw, ik_new, cache_k, cache_v, cache_idx_k, page_table):
    db, t = q.shape[:2]
    n_pool, page = cache_k.shape[:2]
    n_pages = page_table.shape[1]
    n_keys = n_pages * page + t
    n_sel = min(IDX_TOPK, n_keys // 4)
    grp = N_HEADS // N_KV_HEADS
    assert t <= T_PAD <= page and page == 128 and HEAD_DIM == 128 and IDX_DIM == 128

    def pad_t(a, to):
        return jnp.pad(a, [(0, 0), (0, to - a.shape[1])] + [(0, 0)] * (a.ndim - 2))

    kv_rows = page * N_KV_HEADS
    rows = N_HEADS * T_PAD
    iq_rows = pad_t(iq, T_PAD).astype(BF16).reshape(db, T_PAD * N_IDX_HEADS, IDX_DIM)
    iw_rows = pad_t(iw, T_PAD).reshape(db, T_PAD * N_IDX_HEADS, 1)
    q_rows = pad_t(q, T_PAD).astype(BF16).transpose(0, 2, 1, 3).reshape(db, rows, HEAD_DIM)
    iknew = pad_t(ik_new, page).astype(BF16)
    knew = pad_t(k_new, page).astype(BF16).reshape(db, kv_rows, HEAD_DIM)
    vnew = pad_t(v_new, page).astype(BF16).reshape(db, kv_rows, HEAD_DIM)
    ck = cache_k.reshape(n_pool, kv_rows, HEAD_DIM)
    cv = cache_v.reshape(n_pool, kv_rows, HEAD_DIM)

    def per_batch(*blk):
        return pl.BlockSpec((None,) + blk, lambda b, pt: (b,) + (0,) * len(blk))

    def paged(nrows):
        return [pl.BlockSpec((None, nrows, HEAD_DIM), lambda b, pt, j=j: (pt[b, j], 0, 0)) for j in range(n_pages)]

    grid_spec = pltpu.PrefetchScalarGridSpec(
        num_scalar_prefetch=1,
        grid=(db,),
        in_specs=[per_batch(T_PAD * N_IDX_HEADS, IDX_DIM), per_batch(T_PAD * N_IDX_HEADS, 1),
                  per_batch(rows, HEAD_DIM), per_batch(page, IDX_DIM), per_batch(kv_rows, HEAD_DIM),
                  per_batch(kv_rows, HEAD_DIM)]
        + paged(page) + paged(kv_rows) + paged(kv_rows),
        out_specs=per_batch(rows, HEAD_DIM),
        scratch_shapes=[pltpu.VMEM((n_pages + 1, T_PAD, page), jnp.int32),
                        pltpu.VMEM((rows, (n_pages + 1) * kv_rows), F32)],
    )
    o = pl.pallas_call(
        functools.partial(_dsa_sample_kernel, n_sel=n_sel, n_pages=n_pages),
        grid_spec=grid_spec,
        out_shape=jax.ShapeDtypeStruct((db, rows, HEAD_DIM), F32),
        compiler_params=pltpu.CompilerParams(dimension_semantics=("arbitrary",),
                                             vmem_limit_bytes=V7X_VMEM_LIMIT),
        name="dsa_sample",
    )(page_table, iq_rows, iw_rows, q_rows, iknew, knew, vnew,
      *([cache_idx_k] * n_pages), *([ck] * n_pages), *([cv] * n_pages))
    o = o.reshape(db, N_KV_HEADS, grp, T_PAD, HEAD_DIM)[:, :, :, :t]
    return o.transpose(0, 3, 1, 2, 4).reshape(db, t, N_HEADS * HEAD_DIM)


def causal_conv(xpad, w_conv):
    c = xpad.shape[-1]
    return lax.conv_general_dilated(xpad, w_conv[:, None, :], window_strides=(1,),
                                    padding='VALID', dimension_numbers=('NWC', 'WIO', 'NWC'),
                                    feature_group_count=c)


def chunk_gated_delta(q, k, v, beta, g, s0):
    b, t, h, dk = q.shape
    dv = v.shape[-1]
    c = min(DELTA_CHUNK, t)
    n = -(-t // c)
    pad = n * c - t

    def chunks(a):
        a = jnp.pad(a, [(0, 0), (0, pad)] + [(0, 0)] * (a.ndim - 2))
        a = a.reshape((b, n, c) + a.shape[2:])
        return jnp.moveaxis(a, (1, 3), (0, 2))

    qc, kc, vc, bc, gc = [chunks(a) for a in (q, k, v, beta, g)]
    gc = jnp.cumsum(gc, axis=-1)
    tri = jnp.tril(jnp.ones((c, c), bool))
    strict = jnp.tril(jnp.ones((c, c), bool), -1)
    decay = jnp.exp(jnp.where(tri, gc[..., :, None] - gc[..., None, :], -jnp.inf))
    kb = kc * bc[..., None]
    a_low = jnp.where(strict, jnp.einsum('nbhid,nbhjd->nbhij', kb, kc) * decay, 0.0)
    rhs = jnp.concatenate([vc * bc[..., None], kb * jnp.exp(gc)[..., None]], axis=-1)
    sol = lax.linalg.triangular_solve(a_low + jnp.eye(c, dtype=F32), rhs, left_side=True,
                                      lower=True, unit_diagonal=True)
    u, w = sol[..., :dv], sol[..., dv:]
    qk = jnp.where(tri, jnp.einsum('nbhid,nbhjd->nbhij', qc, kc) * decay, 0.0)

    def step(state, xs):
        q_i, k_i, u_i, w_i, g_i, qk_i = xs
        v_corr = u_i - jnp.einsum('bhck,bhkv->bhcv', w_i, state)
        o_i = (jnp.einsum('bhck,bhkv->bhcv', q_i * jnp.exp(g_i)[..., None], state)
               + jnp.einsum('bhij,bhjv->bhiv', qk_i, v_corr))
        g_last = g_i[..., -1:]
        state = (state * jnp.exp(g_last)[..., None]
                 + jnp.einsum('bhck,bhcv->bhkv', k_i * jnp.exp(g_last - g_i)[..., None], v_corr))
        return state, o_i

    s_final, o = lax.scan(step, s0, (qc, kc, u, w, gc, qk))
    o = jnp.moveaxis(o, (0, 2), (1, 3)).reshape(b, n * c, h, dv)[:, :t]
    return o, s_final


def gated_delta_branch(qkv, b_d, a_d, z, conv_buf, s0, w_conv, a_log, dt_bias, norm_g):
    b, t, _ = qkv.shape
    xpad = jnp.concatenate([conv_buf, qkv], axis=1)
    new_buf = xpad[:, -(CONV_WIDTH - 1):]
    hc = jax.nn.silu(causal_conv(xpad, w_conv))
    q, k, v = jnp.split(hc, (DELTA_QK_WIDTH, 2 * DELTA_QK_WIDTH), axis=-1)
    q = l2norm(q.reshape(b, t, N_DELTA_HEADS, DELTA_DK)) * DELTA_DK ** -0.5
    k = l2norm(k.reshape(b, t, N_DELTA_HEADS, DELTA_DK))
    v = v.reshape(b, t, N_DELTA_HEADS, DELTA_DV)
    beta = jax.nn.sigmoid(b_d)
    g = -jnp.exp(a_log) * jax.nn.softplus(a_d + dt_bias)
    o, s_new = chunk_gated_delta(q, k, v, beta, g, s0)
    o = o * lax.rsqrt(jnp.mean(o * o, -1, keepdims=True) + 1e-6) * norm_g
    o = o * jax.nn.silu(z.reshape(b, t, N_DELTA_HEADS, DELTA_DV))
    return o.reshape(b, t, DELTA_V_WIDTH), s_new, new_buf


DELTA_TOKEN_BLOCK = 256


def _delta_pre_kernel(x_ref, halo_ref, w_ref, q_ref, k_ref, v_ref, xe_ref):
    i = pl.program_id(0)
    tb = x_ref.shape[0]
    xe_ref[0:SUBLANES, :] = jnp.where(i == 0, 0.0, halo_ref[...])
    xe_ref[SUBLANES:SUBLANES + tb, :] = x_ref[...]
    outs = (q_ref, k_ref, v_ref)
    for col in range(3 * N_DELTA_HEADS):
        cs = slice(col * DELTA_DK, (col + 1) * DELTA_DK)
        y = sum(xe_ref[SUBLANES - (CONV_WIDTH - 1) + tap:SUBLANES - (CONV_WIDTH - 1) + tap + tb, cs] * w_ref[tap:tap + 1, cs]
                for tap in range(CONV_WIDTH))
        y = y * jax.nn.sigmoid(y)
        if col < 2 * N_DELTA_HEADS:
            y = y * lax.rsqrt(jnp.sum(y * y, axis=1, keepdims=True) + 1e-6)
        if col < N_DELTA_HEADS:
            y = y * DELTA_DK ** -0.5
        for cc in range(tb // DELTA_CHUNK):
            outs[col // N_DELTA_HEADS][cc, col % N_DELTA_HEADS] = y[cc * DELTA_CHUNK:(cc + 1) * DELTA_CHUNK]


def delta_pre(qkv, t, w_conv):
    c = w_conv.shape[1]
    tb = DELTA_TOKEN_BLOCK
    assert t % tb == 0 and tb % DELTA_CHUNK == 0 and DELTA_DK == DELTA_DV
    per_blk = tb // DELTA_CHUNK
    out_spec = pl.BlockSpec((per_blk, N_DELTA_HEADS, DELTA_CHUNK, DELTA_DK), lambda i: (i, 0, 0, 0))
    out_shape = jax.ShapeDtypeStruct((t // DELTA_CHUNK, N_DELTA_HEADS, DELTA_CHUNK, DELTA_DK), F32)
    return pl.pallas_call(
        _delta_pre_kernel,
        grid=(t // tb,),
        in_specs=[pl.BlockSpec((tb, c), lambda i: (i, 0)),
                  pl.BlockSpec((SUBLANES, c), lambda i: (jnp.maximum(i * (tb // SUBLANES) - 1, 0), 0)),
                  pl.BlockSpec((CONV_WIDTH, c), lambda i: (0, 0))],
        out_specs=[out_spec] * 3,
        out_shape=[out_shape] * 3,
        scratch_shapes=[pltpu.VMEM((tb + SUBLANES, c), F32)],
        compiler_params=pltpu.CompilerParams(dimension_semantics=("arbitrary",),
                                             vmem_limit_bytes=V7X_VMEM_LIMIT),
        name="delta_pre",
    )(qkv, qkv, w_conv)


def _delta_scan_kernel(wq_ref, kt_ref, u_ref, qk_ref, el_ref, z_ref, ng_ref, o_ref, sfin_ref, s_ref):
    n = pl.program_id(0)
    c = DELTA_CHUNK

    @pl.when(n == 0)
    def _():
        s_ref[...] = jnp.zeros(s_ref.shape, F32)

    for h in range(N_DELTA_HEADS):
        s = s_ref[h]
        ws = jnp.dot(wq_ref[h], s.astype(BF16), preferred_element_type=F32)
        v_corr = (u_ref[h] - ws[:c]).astype(BF16)
        o = ws[c:] + jnp.dot(qk_ref[h], v_corr, preferred_element_type=F32)
        s_ref[h] = s * el_ref[h] + jnp.dot(kt_ref[h], v_corr, preferred_element_type=F32)
        o = o * lax.rsqrt(jnp.mean(o * o, axis=1, keepdims=True) + 1e-6) * ng_ref[...]
        zz = z_ref[:, h * DELTA_DV:(h + 1) * DELTA_DV]
        o_ref[:, h * DELTA_DV:(h + 1) * DELTA_DV] = (o * (zz * jax.nn.sigmoid(zz))).astype(o_ref.dtype)

    @pl.when(n == pl.num_programs(0) - 1)
    def _():
        sfin_ref[...] = s_ref[...]


def prompt_delta_branch(p_delta, p_gate, t, w_conv, a_log, dt_bias, norm_g):
    assert t >= CONV_WIDTH - 1
    c, h = DELTA_CHUNK, N_DELTA_HEADS
    n = t // c
    new_buf = p_delta[t - (CONV_WIDTH - 1):t, :CONV_CH]
    qc, kc, vc = delta_pre(p_delta, t, w_conv)
    beta = jax.nn.sigmoid(p_delta[:t, CONV_CH:CONV_CH + h])
    g = -jnp.exp(a_log) * jax.nn.softplus(p_delta[:t, CONV_CH + h:CONV_CH + 2 * h] + dt_bias)
    bc = beta.reshape(n, c, h).transpose(0, 2, 1)
    gc = jnp.cumsum(g.reshape(n, c, h).transpose(0, 2, 1), axis=-1)
    tri = jnp.tril(jnp.ones((c, c), bool))
    strict = jnp.tril(jnp.ones((c, c), bool), -1)
    decay = jnp.exp(jnp.where(tri, gc[..., :, None] - gc[..., None, :], -jnp.inf))
    kb = kc * bc[..., None]
    a_low = jnp.where(strict, jnp.einsum('nhid,nhjd->nhij', kb, kc) * decay, 0.0)
    rhs = jnp.concatenate([vc * bc[..., None], kb * jnp.exp(gc)[..., None]], axis=-1)
    sol = lax.linalg.triangular_solve(a_low + jnp.eye(c, dtype=F32), rhs, left_side=True,
                                      lower=True, unit_diagonal=True)
    u, w = sol[..., :DELTA_DV], sol[..., DELTA_DV:]
    qk = jnp.where(tri, jnp.einsum('nhid,nhjd->nhij', qc, kc) * decay, 0.0)
    g_last = gc[..., -1:]
    wq = jnp.concatenate([w, qc * jnp.exp(gc)[..., None]], axis=-2).astype(BF16)
    kt = jnp.swapaxes(kc * jnp.exp(g_last - gc)[..., None], -1, -2).astype(BF16)
    e_last = jnp.broadcast_to(jnp.exp(g_last)[..., None], (n, h, 1, DELTA_DV))

    def per_chunk(*blk):
        return pl.BlockSpec((None,) + blk, lambda i: (i,) + (0,) * len(blk))

    d_out, s_fin = pl.pallas_call(
        _delta_scan_kernel,
        grid=(n,),
        in_specs=[per_chunk(h, 2 * c, DELTA_DK), per_chunk(h, DELTA_DK, c), per_chunk(h, c, DELTA_DV),
                  per_chunk(h, c, c), per_chunk(h, 1, DELTA_DV),
                  pl.BlockSpec((c, h * DELTA_DV), lambda i: (i, 0)),
                  pl.BlockSpec((1, DELTA_DV), lambda i: (0, 0))],
        out_specs=[pl.BlockSpec((c, h * DELTA_DV), lambda i: (i, 0)),
                   pl.BlockSpec((h, DELTA_DK, DELTA_DV), lambda i: (0, 0, 0))],
        out_shape=[jax.ShapeDtypeStruct((t, h * DELTA_DV), BF16),
                   jax.ShapeDtypeStruct((h, DELTA_DK, DELTA_DV), F32)],
        scratch_shapes=[pltpu.VMEM((h, DELTA_DK, DELTA_DV), F32)],
        compiler_params=pltpu.CompilerParams(dimension_semantics=("arbitrary",),
                                             vmem_limit_bytes=V7X_VMEM_LIMIT),
        name="delta_scan",
    )(wq, kt, u, qk.astype(BF16), e_last, p_gate, norm_g.reshape(1, DELTA_DV))
    return d_out, s_fin, new_buf


def routed_experts(xt, expert_idx, gate, w_gate, w_up, w_down):
    n, d = xt.shape
    k = expert_idx.shape[1]
    n_exp = w_gate.shape[0]
    blk = EXPERT_BLOCK
    flat_e = expert_idx.reshape(-1)
    order = jnp.argsort(flat_e).astype(jnp.int32)
    counts = jnp.bincount(flat_e, length=n_exp).astype(jnp.int32)
    padded = (counts + blk - 1) // blk * blk
    pad_end = jnp.cumsum(padded)
    pad_start = pad_end - padded
    start = jnp.cumsum(counts) - counts
    n_blocks = -(-(n * k) // blk) + n_exp
    rows = n_blocks * blk
    blk_row0 = jnp.arange(n_blocks, dtype=jnp.int32) * blk
    block_expert = jnp.minimum(jnp.sum(pad_end[None, :] <= blk_row0[:, None], axis=1), n_exp - 1).astype(jnp.int32)
    n_used = (pad_end[-1:] // blk).astype(jnp.int32)
    r = jnp.arange(rows, dtype=jnp.int32)
    rank = (r.reshape(n_blocks, blk) - pad_start[block_expert][:, None])
    real = (rank < counts[block_expert][:, None]).reshape(rows)
    src = order[jnp.clip(start[block_expert][:, None] + rank, 0, n * k - 1).reshape(rows)]
    row_token = jnp.where(real, src // k, r % n)
    row_gate = jnp.where(real, gate.reshape(-1)[src], 0.0)
    xb = xt.astype(BF16)
    grp_blocks = -(-n_blocks // ROW_GROUPS)
    h_rows = None
    for b0 in range(0, n_blocks, grp_blocks):
        b1 = min(b0 + grp_blocks, n_blocks)
        x_rows = xb[row_token[b0 * blk:b1 * blk]]
        h_rows = grouped_swiglu_up(x_rows, block_expert[b0:b1], jnp.clip(n_used - b0, 0, b1 - b0),
                                   w_gate, w_up, blk, b0, n_blocks, h_prev=h_rows)
    y_rows = grouped_down(h_rows, row_gate[:, None], block_expert, n_used, w_down, blk)
    slot = jnp.argsort(order).astype(jnp.int32)
    e_of = flat_e.astype(jnp.int32)
    pos = pad_start[e_of] + slot - start[e_of]
    return y_rows[pos.reshape(n, k).T.reshape(-1)].reshape(k, n, d)


def moe(xt, w_router, router_bias, w_exp_gate, w_exp_up, w_exp_down, w_sh_gate, w_sh_up, w_sh_down):
    n = xt.shape[0]
    scores = jax.nn.sigmoid(jnp.matmul(xt, w_router, preferred_element_type=F32))
    biased = scores + router_bias
    per_group = N_EXPERTS // N_GROUPS
    group_score = lax.top_k(biased.reshape(n, N_GROUPS, per_group), 2)[0].sum(-1)
    _, top_groups = lax.top_k(group_score, TOPK_GROUPS)
    group_keep = jnp.any(top_groups[:, :, None] == jnp.arange(N_GROUPS)[None, None, :], axis=1)
    biased = jnp.where(jnp.repeat(group_keep, per_group, axis=1), biased, -jnp.inf)
    _, expert_idx = lax.top_k(biased, TOP_K)
    gate = jnp.take_along_axis(scores, expert_idx, axis=1)
    gate = gate / jnp.sum(gate, -1, keepdims=True) * ROUTED_SCALE
    routed_parts = routed_experts(xt, expert_idx, gate, w_exp_gate, w_exp_up, w_exp_down)
    xb = xt.astype(BF16)
    shared = matmul(swiglu_up(xb, w_sh_gate, w_sh_up), w_sh_down)
    return routed_parts, shared


def _combine_ln_kernel(y_ref, sh_ref, h_ref, g_ref, b_ref, o_ref):
    acc = ALPHA * h_ref[...] + sh_ref[...]
    for j in range(y_ref.shape[0]):
        acc = acc + y_ref[j].astype(F32)
    xc = acc - jnp.mean(acc, axis=1, keepdims=True)
    var = jnp.mean(xc * xc, axis=1, keepdims=True)
    o_ref[...] = xc * lax.rsqrt(var + LN_EPS) * g_ref[...] + b_ref[...]


def combine_layer_norm(routed_parts, shared, h, g, b, tm=128):
    n, d = h.shape
    k = routed_parts.shape[0]
    assert n % tm == 0 and routed_parts.shape[1:] == (n, d)
    row = lambda i: (i, 0)
    return pl.pallas_call(
        _combine_ln_kernel,
        grid=(n // tm,),
        in_specs=[pl.BlockSpec((k, tm, d), lambda i: (0, i, 0)), pl.BlockSpec((tm, d), row), pl.BlockSpec((tm, d), row),
                  pl.BlockSpec((1, d), lambda i: (0, 0)), pl.BlockSpec((1, d), lambda i: (0, 0))],
        out_specs=pl.BlockSpec((tm, d), row),
        out_shape=jax.ShapeDtypeStruct((n, d), F32),
        compiler_params=pltpu.CompilerParams(dimension_semantics=("arbitrary",),
                                             vmem_limit_bytes=V7X_VMEM_LIMIT),
        name="combine_layer_norm",
    )(routed_parts, shared, h, g.reshape(1, d), b.reshape(1, d))


def kernel(x_prompt, x_sample, cache_k, cache_v, cache_idx_k, page_table, state_delta, state_conv, w_in, w_conv, a_log, dt_bias, delta_norm_g, w_branch_attn, w_branch_delta, w_out, ln1_g, ln1_b, w_router, router_bias, w_exp_gate, w_exp_up, w_exp_down, w_sh_gate, w_sh_up, w_sh_down, ln2_g, ln2_b):
    b, s, d = x_prompt.shape
    db, t = x_sample.shape[:2]
    assert DEPTH == 1 and b == 1
    past = page_table.shape[1] * cache_k.shape[2]
    n_p, n_s = b * s, db * t
    (l_in, l_conv, l_alog, l_dtb, l_ng, l_ba, l_bd, l_out, l_g1, l_b1,
     l_r, l_rb, l_eg, l_eu, l_ed, l_sg, l_su, l_sd, l_g2, l_b2) = [a[0] for a in (
         w_in, w_conv, a_log, dt_bias, delta_norm_g, w_branch_attn, w_branch_delta, w_out,
         ln1_g, ln1_b, w_router, router_bias, w_exp_gate, w_exp_up, w_exp_down,
         w_sh_gate, w_sh_up, w_sh_down, ln2_g, ln2_b)]

    x_all = jnp.concatenate([x_prompt.reshape(n_p, d), x_sample.reshape(n_s, d)], axis=0)
    xb = x_all.astype(BF16)
    w_in_t = jnp.swapaxes(l_in, 0, 1)
    p_attn = matmul_rows(xb, w_in_t, 0, ATTN_COLS)
    p_delta = matmul_rows(xb, w_in_t, IN_SPLITS[5], DELTA_COLS)
    p_gate = matmul_rows(xb, w_in_t, IN_SPLITS[8], GATE_COLS)
    ap = attention_inputs(p_attn[:n_p].reshape(b, s, -1), jnp.arange(s))
    at = attention_inputs(p_attn[n_p:].reshape(db, t, -1), past + jnp.arange(t))

    a_p = prompt_sparse_attention(ap.q, ap.k, ap.v, ap.iq, ap.iw, ap.ik)
    a_s = sample_sparse_attention(at.q, at.k, at.v, at.iq, at.iw, at.ik,
                                  cache_k[0], cache_v[0], cache_idx_k[0], page_table)
    d_p, sd_p, sc_p = prompt_delta_branch(p_delta, p_gate, n_p, l_conv, l_alog, l_dtb, l_ng)
    pd_s = p_delta[n_p:].reshape(db, t, -1)
    nh = N_DELTA_HEADS
    d_s, sd_s, sc_s = gated_delta_branch(pd_s[..., :CONV_CH], pd_s[..., CONV_CH:CONV_CH + nh], pd_s[..., CONV_CH + nh:],
                                         p_gate[n_p:, :DELTA_V_WIDTH].reshape(db, t, -1), state_conv[0], state_delta[0],
                                         l_conv, l_alog, l_dtb, l_ng)

    a_all = jnp.concatenate([a_p.reshape(n_p, -1).astype(BF16), a_s.reshape(n_s, -1).astype(BF16)], axis=0)
    d_all = jnp.concatenate([d_p, d_s.reshape(n_s, -1).astype(BF16)], axis=0)
    merged = merge_branches(a_all, d_all, p_gate, DELTA_V_WIDTH, l_ba, l_bd)
    h = layer_norm(ALPHA * x_all + matmul(merged, l_out), l_g1, l_b1)
    routed_parts, shared = moe(h, l_r, l_rb, l_eg, l_eu, l_ed, l_sg, l_su, l_sd)
    y = combine_layer_norm(routed_parts, shared, h, l_g2, l_b2)
    return (y[:n_p].reshape(b, s, d), y[n_p:].reshape(db, t, d),
            ap.k[None], ap.v[None], ap.ik[None], sd_p[None, None], sc_p[None, None],
            at.k[None], at.v[None], at.ik[None], sd_s[None], sc_s[None])
```
